```python
import math
import jax
import jax.numpy as jnp
from jax import lax
import numpy as np

D_MODEL = 1024
BATCH = 32
SEQ = 2048
DEPTH = 4
DEC_BATCH = 32
DEC_SEQ = 16
PAST_LEN = 4096

CHUNK = 64
N_MIXERS = 4
N_PER_MIXER = DEPTH // N_MIXERS
ALPHA = (2.0 * DEPTH) ** 0.25
BETA = (8.0 * DEPTH) ** -0.25
LN_EPS = 1e-5

SB_HEADS = 16
SB_HEAD_DIM = D_MODEL // SB_HEADS
SB_QBLOCK = 128

SGU_WIDTH = D_MODEL
SGU_GROUPS = 8
SGU_GROUP_DIM = SGU_WIDTH // SGU_GROUPS
SGU_LEN = 128

GLA_HEADS = 4
GLA_DK = D_MODEL // (2 * GLA_HEADS)
GLA_DV = D_MODEL // GLA_HEADS
GLA_GATE_RANK = 16
GLA_TAU = 16.0
GLA_BLOCK = 16

POOL_WINDOWS = (2, 4, 8, 16)
POOL_GROUPS = len(POOL_WINDOWS)
POOL_GROUP_DIM = D_MODEL // POOL_GROUPS
POOL_HIST = max(POOL_WINDOWS) - 1

MOE_GROUPS = 4
MOE_EXPERTS_PER_GROUP = 4
MOE_EXPERTS = MOE_GROUPS * MOE_EXPERTS_PER_GROUP
MOE_TOP_K = 2
MOE_HIDDEN = 512

kernel_name = 'hybrid_streaming_encoder_step'


def layer_norm(x, g, b):
    xf = x.astype(jnp.float32)
    mu = jnp.mean(xf, axis=-1, keepdims=True)
    var = jnp.mean(jnp.square(xf - mu), axis=-1, keepdims=True)
    return ((xf - mu) * lax.rsqrt(var + LN_EPS) * g + b).astype(x.dtype)


def sb_block(q, k, v, q_start):
    z = jnp.einsum('bqhd,bkhd->bhqk', q, k, preferred_element_type=jnp.float32) * (SB_HEAD_DIM ** -0.5)
    q_pos = q_start + jnp.arange(q.shape[1])
    k_pos = jnp.arange(k.shape[1])
    earlier = k_pos[None, :] < q_pos[:, None]
    log_stay = jnp.where(earlier, jax.nn.log_sigmoid(-z), 0.0)
    later = lax.cumsum(log_stay, axis=3, reverse=True) - log_stay
    w = jnp.where(earlier, jnp.exp(jax.nn.log_sigmoid(z) + later), 0.0)
    return jnp.einsum('bhqk,bkhd->bqhd', w.astype(v.dtype), v)


def sb_mixer(x, past_k, past_v, w_qkv, w_o):
    B, T, _ = x.shape
    qkv = (x @ w_qkv).reshape(B, T, 3, SB_HEADS, SB_HEAD_DIM)
    q, k, v = qkv[:, :, 0], qkv[:, :, 1], qkv[:, :, 2]
    if past_k is None:
        blocks = []
        for i in range(T // SB_QBLOCK):
            lo, hi = i * SB_QBLOCK, (i + 1) * SB_QBLOCK
            blocks.append(sb_block(q[:, lo:hi], k[:, :hi], v[:, :hi], lo))
        o = jnp.concatenate(blocks, axis=1)
    else:
        o = sb_block(q, jnp.concatenate([past_k, k], axis=1), jnp.concatenate([past_v, v], axis=1), past_k.shape[1])
    return o.reshape(B, T, D_MODEL) @ w_o, k, v


def sgu_mixer(x, w_in, ln_g, ln_b, w_s, b_s, w_out):
    B, T, _ = x.shape
    z = jax.nn.gelu(x @ w_in)
    u = z[..., :SGU_WIDTH]
    v = layer_norm(z[..., SGU_WIDTH:], ln_g, ln_b)
    L = min(T, SGU_LEN)
    pos = jnp.arange(L)
    allowed = (pos[None, :] // CHUNK) <= (pos[:, None] // CHUNK)
    w = jnp.where(allowed[None], w_s[:, :L, :L], 0.0)
    vc = v.reshape(B, T // L, L, SGU_GROUPS, SGU_GROUP_DIM)
    mixed = jnp.einsum('gts,bcsgd->bctgd', w, vc) + b_s[:, :L].T[None, None, :, :, None]
    return (u * mixed.reshape(B, T, SGU_WIDTH)) @ w_out, v


def gla_scan(q, k, v, log_a, s0):
    B, T, H, _ = q.shape
    f32 = jnp.float32
    q, k, v, log_a = (a.astype(f32) for a in (q, k, v, log_a))
    pad = (-T) % GLA_BLOCK
    if pad:
        widths = ((0, 0), (0, pad), (0, 0), (0, 0))
        q, k, v, log_a = (jnp.pad(a, widths) for a in (q, k, v, log_a))
    n = (T + pad) // GLA_BLOCK

    def to_blocks(a):
        return a.reshape(B, n, GLA_BLOCK, H, a.shape[-1]).swapaxes(0, 1)

    tri = jnp.tril(jnp.ones((GLA_BLOCK, GLA_BLOCK), dtype=bool))

    def step(s, blk):
        qb, kb, vb, lab = blk
        cum = jnp.cumsum(lab, axis=1)
        q_dec = qb * jnp.exp(cum)
        scores = jnp.einsum('bthd,bshd->bhts', q_dec, kb * jnp.exp(-cum))
        scores = jnp.where(tri, scores, 0.0)
        o = jnp.einsum('bhts,bshv->bthv', scores, vb) + jnp.einsum('bthd,bhdv->bthv', q_dec, s)
        last = cum[:, -1]
        s_new = jnp.exp(last)[..., None] * s + jnp.einsum('bshd,bshv->bhdv', kb * jnp.exp(last[:, None] - cum), vb)
        return s_new, o

    s_fin, o = lax.scan(step, s0.astype(f32), tuple(to_blocks(a) for a in (q, k, v, log_a)))
    o = o.swapaxes(0, 1).reshape(B, n * GLA_BLOCK, H, GLA_DV)[:, :T]
    return o, s_fin


def gla_mixer(x, s0, w_in, w_g1, w_g2, b_g, norm_g, w_o):
    B, T, _ = x.shape
    dq, dv = GLA_HEADS * GLA_DK, GLA_HEADS * GLA_DV
    proj = x @ w_in
    q = proj[..., :dq].reshape(B, T, GLA_HEADS, GLA_DK) * (GLA_DK ** -0.5)
    k = proj[..., dq:2 * dq].reshape(B, T, GLA_HEADS, GLA_DK)
    v = proj[..., 2 * dq:2 * dq + dv].reshape(B, T, GLA_HEADS, GLA_DV)
    r = proj[..., 2 * dq + dv:]
    log_a = (jax.nn.log_sigmoid(((x @ w_g1) @ w_g2 + b_g).astype(jnp.float32)) / GLA_TAU).reshape(B, T, GLA_HEADS, GLA_DK)
    o, s = gla_scan(q, k, v, log_a, s0)
    o = o * lax.rsqrt(jnp.mean(jnp.square(o), axis=-1, keepdims=True) + LN_EPS) * norm_g
    y = (o.reshape(B, T, dv).astype(x.dtype) * jax.nn.silu(r)) @ w_o
    return y, s.astype(x.dtype)


def pool_mixer(x, hist, n_valid_hist, w_pool, scale):
    B, T, _ = x.shape
    xh = jnp.concatenate([hist, x], axis=1)
    csum = jnp.pad(jnp.cumsum(xh.astype(jnp.float32), axis=1), ((0, 0), (1, 0), (0, 0)))
    upto = csum[:, POOL_HIST + 1:]
    count_base = jnp.arange(T) + 1 + n_valid_hist
    parts = []
    for g, win in enumerate(POOL_WINDOWS):
        c = slice(g * POOL_GROUP_DIM, (g + 1) * POOL_GROUP_DIM)
        before = csum[:, POOL_HIST + 1 - win:POOL_HIST + 1 - win + T, c]
        count = jnp.minimum(count_base, win).astype(jnp.float32)[None, :, None]
        parts.append((upto[..., c] - before) / count - x[..., c].astype(jnp.float32))
    d = jnp.stack(parts, axis=2).astype(x.dtype)
    y = jnp.einsum('btgc,gcd->btgd', d, w_pool).reshape(B, T, D_MODEL) * scale
    return y, xh[:, -POOL_HIST:]


def moe_ffn(x, w_group, b_group, w_router, b_router, w_up, w_down):
    def per_stream(xs):
        T = xs.shape[0]
        g_logit = (xs @ w_group + b_group).astype(jnp.float32)
        g_prob = jax.nn.softmax(g_logit, axis=-1)
        g_sel = jnp.argmax(g_logit, axis=-1)
        e_logit = (xs @ w_router + b_router).astype(jnp.float32).reshape(T, MOE_GROUPS, MOE_EXPERTS_PER_GROUP)
        e_logit = jnp.take_along_axis(e_logit, g_sel[:, None, None], axis=1)[:, 0]
        top_val, top_idx = lax.top_k(e_logit, MOE_TOP_K)
        top_w = jax.nn.softmax(top_val, axis=-1) * jnp.take_along_axis(g_prob, g_sel[:, None], axis=1)
        within = jnp.sum(jax.nn.one_hot(top_idx, MOE_EXPERTS_PER_GROUP) * top_w[..., None], axis=1)
        gate = (jax.nn.one_hot(g_sel, MOE_GROUPS)[:, :, None] * within[:, None, :]).reshape(T, MOE_EXPERTS)
        h = jnp.einsum('td,edf->tef', xs, w_up)
        act = jax.nn.silu(h[..., :MOE_HIDDEN]) * h[..., MOE_HIDDEN:] * gate.astype(xs.dtype)[..., None]
        return jnp.einsum('tef,efd->td', act, w_down)
    return lax.map(per_stream, x)


def setup_inputs(seed: int = 0) -> dict:
    key = jax.random.key(seed)
    ks = iter(jax.random.split(key, 40))

    def nrm(shape, scale):
        return scale * jax.random.normal(next(ks), shape, jnp.float32)

    def near_one(shape, scale=0.05):
        return 1.0 + nrm(shape, scale)

    L, F = N_PER_MIXER, MOE_HIDDEN
    dq, dv = GLA_HEADS * GLA_DK, GLA_HEADS * GLA_DV
    return {
        'x_prompt': nrm((BATCH, SEQ, D_MODEL), 1.0),
        'x_sample': nrm((DEC_BATCH, DEC_SEQ, D_MODEL), 1.0),
        'cache_sb_k': nrm((L, DEC_BATCH, PAST_LEN, SB_HEADS, SB_HEAD_DIM), 1.0),
        'cache_sb_v': nrm((L, DEC_BATCH, PAST_LEN, SB_HEADS, SB_HEAD_DIM), 1.0),
        'state_gla': nrm((L, DEC_BATCH, GLA_HEADS, GLA_DK, GLA_DV), 1.0),
        'state_pool': nrm((L, DEC_BATCH, POOL_HIST, D_MODEL), 1.0),
        'sb_w_qkv': nrm((L, D_MODEL, 3 * D_MODEL), D_MODEL ** -0.5),
        'sb_w_o': nrm((L, D_MODEL, D_MODEL), BETA * D_MODEL ** -0.5),
        'sgu_w_in': nrm((L, D_MODEL, 2 * SGU_WIDTH), D_MODEL ** -0.5),
        'sgu_ln_g': near_one((L, SGU_WIDTH)),
        'sgu_ln_b': nrm((L, SGU_WIDTH), 0.02),
        'sgu_w_s': nrm((L, SGU_GROUPS, SGU_LEN, SGU_LEN), SGU_LEN ** -0.5),
        'sgu_b_s': near_one((L, SGU_GROUPS, SGU_LEN), 0.1),
        'sgu_w_out': nrm((L, SGU_WIDTH, D_MODEL), BETA * SGU_WIDTH ** -0.5),
        'gla_w_in': nrm((L, D_MODEL, 2 * dq + 2 * dv), D_MODEL ** -0.5),
        'gla_w_g1': nrm((L, D_MODEL, GLA_GATE_RANK), D_MODEL ** -0.5),
        'gla_w_g2': nrm((L, GLA_GATE_RANK, dq), GLA_GATE_RANK ** -0.5),
        'gla_b_g': nrm((L, dq), 0.1),
        'gla_norm_g': near_one((L, GLA_DV)),
        'gla_w_o': nrm((L, dv, D_MODEL), BETA * dv ** -0.5),
        'pool_w': nrm((L, POOL_GROUPS, POOL_GROUP_DIM, POOL_GROUP_DIM), BETA * POOL_GROUP_DIM ** -0.5),
        'pool_scale': near_one((L, D_MODEL), 0.1),
        'ln1_g': near_one((DEPTH, D_MODEL)),
        'ln1_b': nrm((DEPTH, D_MODEL), 0.02),
        'ln2_g': near_one((DEPTH, D_MODEL)),
        'ln2_b': nrm((DEPTH, D_MODEL), 0.02),
        'moe_w_group': nrm((DEPTH, D_MODEL, MOE_GROUPS), D_MODEL ** -0.5),
        'moe_b_group': nrm((DEPTH, MOE_GROUPS), 0.01),
        'moe_w_router': nrm((DEPTH, D_MODEL, MOE_EXPERTS), D_MODEL ** -0.5),
        'moe_b_router': nrm((DEPTH, MOE_EXPERTS), 0.01),
        'moe_w_up': nrm((DEPTH, MOE_EXPERTS, D_MODEL, 2 * F), D_MODEL ** -0.5),
        'moe_w_down': nrm((DEPTH, MOE_EXPERTS, F, D_MODEL), BETA * F ** -0.5),
    }


def reference(x_prompt, x_sample, cache_sb_k, cache_sb_v, state_gla, state_pool,
              sb_w_qkv, sb_w_o, sgu_w_in, sgu_ln_g, sgu_ln_b, sgu_w_s, sgu_b_s, sgu_w_out,
              gla_w_in, gla_w_g1, gla_w_g2, gla_b_g, gla_norm_g, gla_w_o,
              pool_w, pool_scale, ln1_g, ln1_b, ln2_g, ln2_b,
              moe_w_group, moe_b_group, moe_w_router, moe_b_router, moe_w_up, moe_w_down):

    def run(x, past):
        B = x.shape[0]
        sb_k, sb_v, sgu_v, gla_s, pool_h = [], [], [], [], []
        for i in range(DEPTH):
            m, j = i % N_MIXERS, i // N_MIXERS
            if m == 0:
                pk = None if past is None else past[0][j]
                pv = None if past is None else past[1][j]
                y, k, v = sb_mixer(x, pk, pv, sb_w_qkv[j], sb_w_o[j])
                sb_k.append(k)
                sb_v.append(v)
            elif m == 1:
                y, v = sgu_mixer(x, sgu_w_in[j], sgu_ln_g[j], sgu_ln_b[j], sgu_w_s[j], sgu_b_s[j], sgu_w_out[j])
                sgu_v.append(v)
            elif m == 2:
                s0 = jnp.zeros((B, GLA_HEADS, GLA_DK, GLA_DV), jnp.float32) if past is None else past[2][j]
                y, s = gla_mixer(x, s0, gla_w_in[j], gla_w_g1[j], gla_w_g2[j], gla_b_g[j], gla_norm_g[j], gla_w_o[j])
                gla_s.append(s)
            else:
                if past is None:
                    h0, n_valid = jnp.zeros((B, POOL_HIST, D_MODEL), x.dtype), 0
                else:
                    h0, n_valid = past[3][j], POOL_HIST
                y, h = pool_mixer(x, h0, n_valid, pool_w[j], pool_scale[j])
                pool_h.append(h)
            x = layer_norm(ALPHA * x + y, ln1_g[i], ln1_b[i])
            f = moe_ffn(x, moe_w_group[i], moe_b_group[i], moe_w_router[i], moe_b_router[i], moe_w_up[i], moe_w_down[i])
            x = layer_norm(ALPHA * x + f, ln2_g[i], ln2_b[i])
        return x, sb_k, sb_v, sgu_v, gla_s, pool_h

    y_prompt, kp, vp, _, sp, hp = run(x_prompt, None)
    y_sample, ks, vs, us, ss, hs = run(x_sample, (cache_sb_k, cache_sb_v, state_gla, state_pool))
    return (y_prompt, y_sample, jnp.stack(kp), jnp.stack(vp), jnp.stack(ks), jnp.stack(vs), jnp.stack(us),
            jnp.stack(sp), jnp.stack(ss), jnp.stack(hp), jnp.stack(hs))
```

```python
import functools
import math

import jax
import jax.numpy as jnp
from jax import lax
from jax.experimental import pallas as pl
from jax.experimental.pallas import tpu as pltpu

F32 = jnp.float32
BF16 = jnp.bfloat16

LN_EPS = 1e-5
CHUNK = 64
SGU_LEN = 128
SB_QBLOCK = 128
GLA_BLOCK = 16
GLA_TAU = 16.0
POOL_WINDOWS = (2, 4, 8, 16)
POOL_PAD = 16
MOE_EXPERTS_PER_GROUP = 4
LANES = 128
VMEM_LIMIT = 56 * 1024 * 1024


def _cparams(*sem):
    return pltpu.CompilerParams(dimension_semantics=sem, vmem_limit_bytes=VMEM_LIMIT)


def _tile(n, pref):
    t = min(n, pref)
    while n % t:
        t -= 8
    return t


def _layer_norm(x, g, b):
    mu = jnp.mean(x, axis=-1, keepdims=True)
    xc = x - mu
    var = jnp.mean(xc * xc, axis=-1, keepdims=True)
    return xc * lax.rsqrt(var + LN_EPS) * g + b


def _split_bf16(x):
    hi = x.astype(BF16)
    lo = (x - hi.astype(F32)).astype(BF16)
    return hi, lo


def _log_sigmoid(z):
    return jnp.minimum(z, 0.0) - jnp.log(1.0 + jnp.exp(-jnp.abs(z)))


def _silu(x):
    return x / (1.0 + jnp.exp(-x))


def _proj_kernel(x_ref, *refs, n_out):
    xb = x_ref[...].astype(BF16)
    for w_ref, o_ref in zip(refs[:n_out], refs[n_out:]):
        o_ref[...] = jnp.dot(xb, w_ref[...], preferred_element_type=F32).astype(o_ref.dtype)


def _proj(x, ws, dtypes, name):
    n, k = x.shape
    tm = _tile(n, 512)
    return pl.pallas_call(
        functools.partial(_proj_kernel, n_out=len(ws)),
        grid=(n // tm,),
        in_specs=[pl.BlockSpec((tm, k), lambda i: (i, 0))]
        + [pl.BlockSpec(w.shape, lambda i: (0, 0)) for w in ws],
        out_specs=[pl.BlockSpec((tm, w.shape[1]), lambda i: (i, 0)) for w in ws],
        out_shape=[jax.ShapeDtypeStruct((n, w.shape[1]), dt) for w, dt in zip(ws, dtypes)],
        compiler_params=_cparams("parallel"),
        name=name,
    )(x, *ws)


def _mm_res_ln_kernel(a_ref, w_ref, res_ref, g_ref, b_ref, o_ref, *, alpha):
    y = jnp.dot(a_ref[...].astype(BF16), w_ref[...], preferred_element_type=F32)
    o_ref[...] = _layer_norm(alpha * res_ref[...] + y, g_ref[...], b_ref[...])


def _mm_res_ln(a, w, res, g, b, alpha, name):
    n, k = a.shape
    d = w.shape[1]
    tm = _tile(n, 512)
    return pl.pallas_call(
        functools.partial(_mm_res_ln_kernel, alpha=alpha),
        grid=(n // tm,),
        in_specs=[pl.BlockSpec((tm, k), lambda i: (i, 0)),
                  pl.BlockSpec((k, d), lambda i: (0, 0)),
                  pl.BlockSpec((tm, d), lambda i: (i, 0)),
                  pl.BlockSpec((1, d), lambda i: (0, 0)),
                  pl.BlockSpec((1, d), lambda i: (0, 0))],
        out_specs=pl.BlockSpec((tm, d), lambda i: (i, 0)),
        out_shape=jax.ShapeDtypeStruct((n, d), F32),
        compiler_params=_cparams("parallel"),
        name=name,
    )(a, w, res, g, b)


def _sb_cumsum_rhs(nk):
    s = lax.broadcasted_iota(jnp.int32, (nk, nk), 0)
    j = lax.broadcasted_iota(jnp.int32, (nk, nk), 1)
    later = jnp.where(s > j, 1.0, 0.0).astype(BF16)
    return jnp.concatenate([later, jnp.ones((nk, nk), BF16)], axis=1)


def _sb_block(qh, kb, vb, valid, acc, carry, ucat, scale):
    nk = kb.shape[0]
    z = lax.dot_general(qh, kb, (((1,), (1,)), ((), ())), preferred_element_type=F32) * scale
    ls = _log_sigmoid(z)
    lstay = ls - z
    if valid is not None:
        lstay = jnp.where(valid, lstay, 0.0)
    hi, lo = _split_bf16(lstay)
    cs = (jnp.dot(hi, ucat, preferred_element_type=F32)
          + jnp.dot(lo, ucat, preferred_element_type=F32))
    later = cs[:, :nk] + carry
    w = jnp.exp(ls + later)
    if valid is not None:
        w = jnp.where(valid, w, 0.0)
    acc = acc + jnp.dot(w.astype(BF16), vb, preferred_element_type=F32)
    return acc, carry + cs[:, nk:]


def _sb_prompt_kernel(q_ref, k_ref, v_ref, o_ref, *, scale, head_dim):
    qi = pl.program_id(2)
    tq = q_ref.shape[0]
    q = q_ref[...]
    lane = lax.broadcasted_iota(jnp.int32, (tq, LANES), 1)
    row = lax.broadcasted_iota(jnp.int32, (tq, SB_QBLOCK), 0)
    col = lax.broadcasted_iota(jnp.int32, (tq, SB_QBLOCK), 1)
    ucat = _sb_cumsum_rhs(SB_QBLOCK)
    heads = LANES // head_dim
    out = jnp.zeros((tq, LANES), F32)
    for h in range(heads):
        in_head = (lane >= h * head_dim) & (lane < (h + 1) * head_dim)
        qh = jnp.where(in_head, q, jnp.zeros_like(q))

        def body(jj, state, qh=qh):
            acc, carry = state
            start = pl.multiple_of((qi - jj) * SB_QBLOCK, SB_QBLOCK)
            kb = k_ref[pl.ds(start, SB_QBLOCK), :].astype(BF16)
            vb = v_ref[pl.ds(start, SB_QBLOCK), :].astype(BF16)
            valid = col < row + jnp.where(jj > 0, SB_QBLOCK, 0)
            return _sb_block(qh, kb, vb, valid, acc, carry, ucat, scale)

        zero = jnp.zeros((tq, LANES), F32)
        acc, _ = lax.fori_loop(0, qi + 1, body, (zero, zero))
        out = jnp.where(in_head, acc, out)
    o_ref[...] = out.astype(o_ref.dtype)


def _sb_prompt(q, k, v, batch, seq, head_dim):
    n, d = q.shape
    nqb = seq // SB_QBLOCK
    return pl.pallas_call(
        functools.partial(_sb_prompt_kernel, scale=head_dim ** -0.5, head_dim=head_dim),
        grid=(batch, d // LANES, nqb),
        in_specs=[pl.BlockSpec((SB_QBLOCK, LANES), lambda b, hp, qi: (b * nqb + qi, hp)),
                  pl.BlockSpec((seq, LANES), lambda b, hp, qi: (b, hp)),
                  pl.BlockSpec((seq, LANES), lambda b, hp, qi: (b, hp))],
        out_specs=pl.BlockSpec((SB_QBLOCK, LANES), lambda b, hp, qi: (b * nqb + qi, hp)),
        out_shape=jax.ShapeDtypeStruct((n, d), BF16),
        compiler_params=_cparams("parallel", "parallel", "arbitrary"),
        name="sb_prompt",
    )(q, k, v)


def _sb_sample_kernel(q_ref, kn_ref, vn_ref, kp_ref, vp_ref, o_ref, *, scale, head_dim):
    tq = q_ref.shape[0]
    past = kp_ref.shape[0]
    q = q_ref[...]
    lane = lax.broadcasted_iota(jnp.int32, (tq, LANES), 1)
    row = lax.broadcasted_iota(jnp.int32, (tq, SB_QBLOCK), 0)
    col = lax.broadcasted_iota(jnp.int32, (tq, SB_QBLOCK), 1)
    ucat = _sb_cumsum_rhs(SB_QBLOCK)
    pad = jnp.zeros((SB_QBLOCK - tq, LANES), BF16)
    kn = jnp.concatenate([kn_ref[...].astype(BF16), pad], axis=0)
    vn = jnp.concatenate([vn_ref[...].astype(BF16), pad], axis=0)
    heads = LANES // head_dim
    out = jnp.zeros((tq, LANES), F32)
    for h in range(heads):
        in_head = (lane >= h * head_dim) & (lane < (h + 1) * head_dim)
        qh = jnp.where(in_head, q, jnp.zeros_like(q))
        zero = jnp.zeros((tq, LANES), F32)
        state = _sb_block(qh, kn, vn, col < row, zero, zero, ucat, scale)

        def body(jj, state, qh=qh):
            start = pl.multiple_of(past - (jj + 1) * SB_QBLOCK, SB_QBLOCK)
            kb = kp_ref[pl.ds(start, SB_QBLOCK), :].astype(BF16)
            vb = vp_ref[pl.ds(start, SB_QBLOCK), :].astype(BF16)
            return _sb_block(qh, kb, vb, None, state[0], state[1], ucat, scale)

        acc, _ = lax.fori_loop(0, past // SB_QBLOCK, body, state)
        out = jnp.where(in_head, acc, out)
    o_ref[...] = out.astype(o_ref.dtype)


def _sb_sample(q, kn, vn, kp, vp, batch, tq, head_dim):
    n, d = q.shape
    past = kp.shape[1]
    return pl.pallas_call(
        functools.partial(_sb_sample_kernel, scale=head_dim ** -0.5, head_dim=head_dim),
        grid=(batch, d // LANES),
        in_specs=[pl.BlockSpec((tq, LANES), lambda b, hp: (b, hp)),
                  pl.BlockSpec((tq, LANES), lambda b, hp: (b, hp)),
                  pl.BlockSpec((tq, LANES), lambda b, hp: (b, hp)),
                  pl.BlockSpec((None, past, LANES), lambda b, hp: (b, 0, hp)),
                  pl.BlockSpec((None, past, LANES), lambda b, hp: (b, 0, hp))],
        out_specs=pl.BlockSpec((tq, LANES), lambda b, hp: (b, hp)),
        out_shape=jax.ShapeDtypeStruct((n, d), BF16),
        compiler_params=_cparams("parallel", "parallel"),
        name="sb_sample",
    )(q, kn, vn, kp, vp)


def _gelu_tanh(x):
    c = math.sqrt(2.0 / math.pi)
    return 0.5 * x * (1.0 + jnp.tanh(c * (x + 0.044715 * (x * x * x))))


def _sgu_kernel(x_ref, win_ref, lng_ref, lnb_ref, ws_ref, bs_ref, wout_ref, g_ref, b_ref,
                o_ref, *v_out, alpha, width, groups):
    x = x_ref[...]
    z = _gelu_tanh(jnp.dot(x.astype(BF16), win_ref[...], preferred_element_type=F32))
    u = z[:, :width]
    v = _layer_norm(z[:, width:], lng_ref[...], lnb_ref[...])
    if v_out:
        v_out[0][...] = v
    tm = x.shape[0]
    gd = width // groups
    vb = v.astype(BF16)
    rows = []
    for c in range(tm // SGU_LEN):
        r0 = c * SGU_LEN
        cols = []
        for g in range(groups):
            mixed = jnp.dot(ws_ref[g], vb[r0:r0 + SGU_LEN, g * gd:(g + 1) * gd],
                            preferred_element_type=F32) + bs_ref[g]
            cols.append(mixed)
        rows.append(jnp.concatenate(cols, axis=1))
    mixed = rows[0] if len(rows) == 1 else jnp.concatenate(rows, axis=0)
    y = jnp.dot((u * mixed).astype(BF16), wout_ref[...], preferred_element_type=F32)
    o_ref[...] = _layer_norm(alpha * x + y, g_ref[...], b_ref[...])


def _sgu(x, w_in, ln_g, ln_b, ws_mix, bs_mix, w_out, g, b, alpha, want_v):
    n, d = x.shape
    width = w_out.shape[0]
    groups = ws_mix.shape[0]
    tm = _tile(n, 256)
    const2 = lambda i: (0, 0)
    const3 = lambda i: (0, 0, 0)
    out_specs = [pl.BlockSpec((tm, d), lambda i: (i, 0))]
    out_shape = [jax.ShapeDtypeStruct((n, d), F32)]
    if want_v:
        out_specs.append(pl.BlockSpec((tm, width), lambda i: (i, 0)))
        out_shape.append(jax.ShapeDtypeStruct((n, width), F32))
    res = pl.pallas_call(
        functools.partial(_sgu_kernel, alpha=alpha, width=width, groups=groups),
        grid=(n // tm,),
        in_specs=[pl.BlockSpec((tm, d), lambda i: (i, 0)),
                  pl.BlockSpec(w_in.shape, const2),
                  pl.BlockSpec((1, width), const2),
                  pl.BlockSpec((1, width), const2),
                  pl.BlockSpec(ws_mix.shape, const3),
                  pl.BlockSpec(bs_mix.shape, const3),
                  pl.BlockSpec(w_out.shape, const2),
                  pl.BlockSpec((1, d), const2),
                  pl.BlockSpec((1, d), const2)],
        out_specs=out_specs,
        out_shape=out_shape,
        compiler_params=_cparams("parallel"),
        name="sgu",
    )(x, w_in, ln_g, ln_b, ws_mix, bs_mix, w_out, g, b)
    return res[0], (res[1] if want_v else None)


def _gla_proj_kernel(x_ref, wq_ref, wk_ref, wv_ref, wr_ref, wg1_ref, wg2_ref, bg_ref,
                     q_ref, k_ref, v_ref, r_ref, la_ref, *, qscale):
    xb = x_ref[...].astype(BF16)
    q_ref[...] = jnp.dot(xb, wq_ref[...], preferred_element_type=F32) * qscale
    k_ref[...] = jnp.dot(xb, wk_ref[...], preferred_element_type=F32)
    v_ref[...] = jnp.dot(xb, wv_ref[...], preferred_element_type=F32).astype(BF16)
    r_ref[...] = jnp.dot(xb, wr_ref[...], preferred_element_type=F32)
    t = jnp.dot(xb, wg1_ref[...], preferred_element_type=F32)
    gate = jnp.dot(t.astype(BF16), wg2_ref[...], preferred_element_type=F32) + bg_ref[...]
    la_ref[...] = _log_sigmoid(gate) / GLA_TAU


def _gla_proj(x, wq, wk, wv, wr, wg1, wg2, bg, qscale):
    n, d = x.shape
    dq, dv = wq.shape[1], wv.shape[1]
    tm = _tile(n, 512)
    const = lambda i: (0, 0)
    row = lambda i: (i, 0)
    return pl.pallas_call(
        functools.partial(_gla_proj_kernel, qscale=qscale),
        grid=(n // tm,),
        in_specs=[pl.BlockSpec((tm, d), row)]
        + [pl.BlockSpec(w.shape, const) for w in (wq, wk, wv, wr, wg1, wg2, bg)],
        out_specs=[pl.BlockSpec((tm, dq), row), pl.BlockSpec((tm, dq), row),
                   pl.BlockSpec((tm, dv), row), pl.BlockSpec((tm, dv), row),
                   pl.BlockSpec((tm, dq), row)],
        out_shape=[jax.ShapeDtypeStruct((n, dq), F32), jax.ShapeDtypeStruct((n, dq), F32),
                   jax.ShapeDtypeStruct((n, dv), BF16), jax.ShapeDtypeStruct((n, dv), F32),
                   jax.ShapeDtypeStruct((n, dq), F32)],
        compiler_params=_cparams("parallel"),
        name="gla_proj",
    )(x, wq, wk, wv, wr, wg1, wg2, bg)


def _gla_scan_kernel(q_ref, k_ref, v_ref, la_ref, s0_ref, o_ref, sfin_ref, st_ref,
                     *, heads, dk, dv):
    ti = pl.program_id(1)
    tt = q_ref.shape[0]
    nsb = tt // GLA_BLOCK

    @pl.when(ti == 0)
    def _():
        for h in range(heads):
            st_ref[h] = s0_ref[0, h].T

    row = lax.broadcasted_iota(jnp.int32, (tt, tt), 0)
    col = lax.broadcasted_iota(jnp.int32, (tt, tt), 1)
    same = (row // GLA_BLOCK) == (col // GLA_BLOCK)
    tril = same & (col <= row)
    bdcat = jnp.concatenate([jnp.where(tril, 1.0, 0.0), jnp.where(same, 1.0, 0.0)],
                            axis=0).astype(BF16)
    rsub = lax.broadcasted_iota(jnp.int32, (tt, dk), 0) // GLA_BLOCK

    for h in range(heads):
        ks = slice(h * dk, (h + 1) * dk)
        vs = slice(h * dv, (h + 1) * dv)
        hi, lo = _split_bf16(la_ref[:, ks])
        ct = (jnp.dot(bdcat, hi, preferred_element_type=F32)
              + jnp.dot(bdcat, lo, preferred_element_type=F32))
        cum, tot = ct[:tt], ct[tt:]
        k = k_ref[:, ks]
        qd = (q_ref[:, ks] * jnp.exp(cum)).astype(BF16)
        kd = (k * jnp.exp(-cum)).astype(BF16)
        kt = k * jnp.exp(tot - cum)
        sc = lax.dot_general(qd, kd, (((1,), (1,)), ((), ())), preferred_element_type=F32)
        sc = jnp.where(tril, sc, 0.0).astype(BF16)
        vh = v_ref[:, vs]
        o_intra = jnp.dot(sc, vh, preferred_element_type=F32)
        ktcat = jnp.concatenate(
            [jnp.where(rsub == i, kt, 0.0).astype(BF16) for i in range(nsb)], axis=1)
        ut = lax.dot_general(vh, ktcat, (((0,), (0,)), ((), ())), preferred_element_type=F32)
        st = st_ref[h]
        parts = []
        for i in range(nsb):
            r0 = i * GLA_BLOCK
            parts.append(lax.dot_general(qd[r0:r0 + GLA_BLOCK], st.astype(BF16),
                                         (((1,), (1,)), ((), ())), preferred_element_type=F32))
            st = st * jnp.exp(tot[r0:r0 + 1, :]) + ut[:, i * dk:(i + 1) * dk]
        st_ref[h] = st
        o_ref[:, vs] = o_intra + jnp.concatenate(parts, axis=0)

    @pl.when(ti == pl.num_programs(1) - 1)
    def _():
        for h in range(heads):
            sfin_ref[0, h] = st_ref[h].T


def _gla_scan(q, k, v, la, s0, batch, seq, heads):
    n, dq = q.shape
    dvt = v.shape[1]
    dk, dv = dq // heads, dvt // heads
    tt = SGU_LEN
    nt = seq // tt
    row = lambda b, t: (b * nt + t, 0)
    st = lambda b, t: (b, 0, 0, 0)
    return pl.pallas_call(
        functools.partial(_gla_scan_kernel, heads=heads, dk=dk, dv=dv),
        grid=(batch, nt),
        in_specs=[pl.BlockSpec((tt, dq), row), pl.BlockSpec((tt, dq), row),
                  pl.BlockSpec((tt, dvt), row), pl.BlockSpec((tt, dq), row),
                  pl.BlockSpec((1, heads, dk, dv), st)],
        out_specs=[pl.BlockSpec((tt, dvt), row), pl.BlockSpec((1, heads, dk, dv), st)],
        out_shape=[jax.ShapeDtypeStruct((n, dvt), F32),
                   jax.ShapeDtypeStruct((batch, heads, dk, dv), F32)],
        scratch_shapes=[pltpu.VMEM((heads, dv, dk), F32)],
        compiler_params=_cparams("parallel", "arbitrary"),
        name="gla_scan",
    )(q, k, v, la, s0)


def _gla_out_kernel(o_ref, r_ref, ng_ref, wo_ref, x_ref, g_ref, b_ref, y_ref, *, alpha, heads):
    o = o_ref[...]
    dv = o.shape[1] // heads
    normed = []
    for h in range(heads):
        oh = o[:, h * dv:(h + 1) * dv]
        ms = jnp.mean(oh * oh, axis=-1, keepdims=True)
        normed.append(oh * lax.rsqrt(ms + LN_EPS) * ng_ref[...])
    gated = jnp.concatenate(normed, axis=1) * _silu(r_ref[...])
    y = jnp.dot(gated.astype(BF16), wo_ref[...], preferred_element_type=F32)
    y_ref[...] = _layer_norm(alpha * x_ref[...] + y, g_ref[...], b_ref[...])


def _gla_out(o, r, norm_g, w_o, x, g, b, alpha, heads):
    n, dvt = o.shape
    d = x.shape[1]
    tm = _tile(n, 512)
    row = lambda i: (i, 0)
    const = lambda i: (0, 0)
    return pl.pallas_call(
        functools.partial(_gla_out_kernel, alpha=alpha, heads=heads),
        grid=(n // tm,),
        in_specs=[pl.BlockSpec((tm, dvt), row), pl.BlockSpec((tm, dvt), row),
                  pl.BlockSpec(norm_g.shape, const), pl.BlockSpec(w_o.shape, const),
                  pl.BlockSpec((tm, d), row), pl.BlockSpec((1, d), const),
                  pl.BlockSpec((1, d), const)],
        out_specs=pl.BlockSpec((tm, d), row),
        out_shape=jax.ShapeDtypeStruct((n, d), F32),
        compiler_params=_cparams("parallel"),
        name="gla_out",
    )(o, r, norm_g, w_o, x, g, b)


def _pool_kernel(x_ref, prev_ref, hist_ref, w_ref, sc_ref, g_ref, b_ref, o_ref, buf_ref,
                 *, alpha, n_valid):
    ti = pl.program_id(1)
    tm, d = x_ref.shape
    x = x_ref[...]

    @pl.when(ti == 0)
    def _():
        buf_ref[0:POOL_PAD, :] = hist_ref[0]

    @pl.when(ti > 0)
    def _():
        buf_ref[0:POOL_PAD, :] = prev_ref[...]

    buf_ref[POOL_PAD:POOL_PAD + tm, :] = x
    gd = d // len(POOL_WINDOWS)
    pos = ti * tm + lax.broadcasted_iota(jnp.int32, (tm, 1), 0)
    ys = []
    for g, win in enumerate(POOL_WINDOWS):
        cs = slice(g * gd, (g + 1) * gd)
        acc = x[:, cs]
        for j in range(1, win):
            acc = acc + buf_ref[POOL_PAD - j:POOL_PAD - j + tm, cs]
        count = jnp.minimum(pos + 1 + n_valid, win).astype(F32)
        dlt = acc / count - x[:, cs]
        ys.append(jnp.dot(dlt.astype(BF16), w_ref[g], preferred_element_type=F32))
    y = jnp.concatenate(ys, axis=1) * sc_ref[...]
    o_ref[...] = _layer_norm(alpha * x + y, g_ref[...], b_ref[...])


def _pool(x, hist, w_pool, scale, g, b, alpha, batch, seq, n_valid):
    n, d = x.shape
    tm = _tile(seq, 256)
    nt = seq // tm
    ratio = tm // POOL_PAD
    const2 = lambda bb, t: (0, 0)
    row = lambda bb, t: (bb * nt + t, 0)
    return pl.pallas_call(
        functools.partial(_pool_kernel, alpha=alpha, n_valid=n_valid),
        grid=(batch, nt),
        in_specs=[pl.BlockSpec((tm, d), row),
                  pl.BlockSpec((POOL_PAD, d),
                               lambda bb, t: (jnp.maximum((bb * nt + t) * ratio - 1, 0), 0)),
                  pl.BlockSpec((1, POOL_PAD, d), lambda bb, t: (bb, 0, 0)),
                  pl.BlockSpec(w_pool.shape, lambda bb, t: (0, 0, 0)),
                  pl.BlockSpec((1, d), const2), pl.BlockSpec((1, d), const2),
                  pl.BlockSpec((1, d), const2)],
        out_specs=pl.BlockSpec((tm, d), row),
        out_shape=jax.ShapeDtypeStruct((n, d), F32),
        scratch_shapes=[pltpu.VMEM((POOL_PAD + tm, d), F32)],
        compiler_params=_cparams("parallel", "arbitrary"),
        name="pool",
    )(x, x, hist, w_pool, scale, g, b)


def _moe_gates(x, wr_hi, wr_lo, br, n_exp, n_grp):
    hi, lo = _split_bf16(x)
    logit = (jnp.dot(hi, wr_hi, preferred_element_type=F32)
             + jnp.dot(lo, wr_hi, preferred_element_type=F32)
             + jnp.dot(hi, wr_lo, preferred_element_type=F32)) + br
    lane_i = lax.broadcasted_iota(jnp.int32, logit.shape, 1)
    lane = lane_i.astype(F32)
    neg = jnp.float32(-jnp.inf)
    big = jnp.float32(LANES)
    is_grp = (lane_i >= n_exp) & (lane_i < n_exp + n_grp)
    gl = jnp.where(is_grp, logit, neg)
    gmax = jnp.max(gl, axis=-1, keepdims=True)
    g_sel = jnp.min(jnp.where(gl == gmax, lane, big), axis=-1, keepdims=True) - n_exp
    g_prob = 1.0 / jnp.sum(jnp.where(is_grp, jnp.exp(gl - gmax), 0.0), axis=-1, keepdims=True)
    lane_grp = (lane_i // MOE_EXPERTS_PER_GROUP).astype(F32)
    in_grp = (lane_i < n_exp) & (lane_grp == g_sel)
    el = jnp.where(in_grp, logit, neg)
    m1 = jnp.max(el, axis=-1, keepdims=True)
    i1 = jnp.min(jnp.where(el == m1, lane, big), axis=-1, keepdims=True)
    el2 = jnp.where(lane == i1, neg, el)
    m2 = jnp.max(el2, axis=-1, keepdims=True)
    i2 = jnp.min(jnp.where(el2 == m2, lane, big), axis=-1, keepdims=True)
    e2 = jnp.exp(m2 - m1)
    w1 = g_prob / (1.0 + e2)
    w2 = g_prob * e2 / (1.0 + e2)
    return jnp.where(lane == i1, w1, 0.0) + jnp.where(lane == i2, w2, 0.0)


def _moe_dense_kernel(x_ref, wrh_ref, wrl_ref, br_ref, wup_ref, wdn_ref, g_ref, b_ref, o_ref,
                      xb_ref, gate_ref, acc_ref, *, alpha, n_exp, n_grp, hidden):
    e = pl.program_id(1)

    @pl.when(e == 0)
    def _():
        x = x_ref[...]
        xb_ref[...] = x.astype(BF16)
        gate_ref[...] = _moe_gates(x, wrh_ref[...], wrl_ref[...], br_ref[...], n_exp, n_grp)
        acc_ref[...] = jnp.zeros_like(acc_ref)

    gate = gate_ref[...]
    lane = lax.broadcasted_iota(jnp.int32, gate.shape, 1)
    ge = jnp.sum(jnp.where(lane == e, gate, 0.0), axis=-1, keepdims=True)
    h = jnp.dot(xb_ref[...], wup_ref[0], preferred_element_type=F32)
    act = _silu(h[:, :hidden]) * h[:, hidden:] * ge
    acc_ref[...] += jnp.dot(act.astype(BF16), wdn_ref[0], preferred_element_type=F32)

    @pl.when(e == n_exp - 1)
    def _():
        o_ref[...] = _layer_norm(alpha * x_ref[...] + acc_ref[...], g_ref[...], b_ref[...])


def _moe(x, wr_hi, wr_lo, br, w_up, w_dn, g, b, alpha, n_grp):
    n, d = x.shape
    n_exp, _, two_f = w_up.shape
    hidden = two_f // 2
    tm = _tile(n, 512)
    const = lambda i, e: (0, 0)
    row = lambda i, e: (i, 0)
    return pl.pallas_call(
        functools.partial(_moe_dense_kernel, alpha=alpha, n_exp=n_exp, n_grp=n_grp, hidden=hidden),
        grid=(n // tm, n_exp),
        in_specs=[pl.BlockSpec((tm, d), row),
                  pl.BlockSpec(wr_hi.shape, const), pl.BlockSpec(wr_lo.shape, const),
                  pl.BlockSpec((1, LANES), const),
                  pl.BlockSpec((1, d, two_f), lambda i, e: (e, 0, 0)),
                  pl.BlockSpec((1, hidden, d), lambda i, e: (e, 0, 0)),
                  pl.BlockSpec((1, d), const), pl.BlockSpec((1, d), const)],
        out_specs=pl.BlockSpec((tm, d), row),
        out_shape=jax.ShapeDtypeStruct((n, d), F32),
        scratch_shapes=[pltpu.VMEM((tm, d), BF16), pltpu.VMEM((tm, LANES), F32),
                        pltpu.VMEM((tm, d), F32)],
        compiler_params=_cparams("parallel", "arbitrary"),
        name="moe",
    )(x, wr_hi, wr_lo, br, w_up, w_dn, g, b)


def _router_params(w_group, b_group, w_router, b_router):
    d, n_grp = w_group.shape
    n_exp = w_router.shape[1]
    w = jnp.zeros((d, LANES), F32).at[:, :n_exp].set(w_router).at[:, n_exp:n_exp + n_grp].set(w_group)
    hi = w.astype(BF16)
    lo = (w - hi.astype(F32)).astype(BF16)
    br = jnp.zeros((1, LANES), F32).at[0, :n_exp].set(b_router).at[0, n_exp:n_exp + n_grp].set(b_group)
    return hi, lo, br


def _sgu_mix_params(w_s, b_s, t_len, gd):
    groups = w_s.shape[0]
    L = min(t_len, SGU_LEN)
    pos = jnp.arange(L)
    allowed = (pos[None, :] // CHUNK) <= (pos[:, None] // CHUNK)
    w = jnp.where(allowed[None], w_s[:, :L, :L], 0.0)
    rep = SGU_LEN // L
    if rep > 1:
        eye = jnp.eye(rep, dtype=F32)
        w = jnp.einsum("ab,gts->gatbs", eye, w).reshape(groups, SGU_LEN, SGU_LEN)
    bias = jnp.tile(b_s[:, :L], (1, rep))
    bias = jnp.broadcast_to(bias[:, :, None], (groups, SGU_LEN, gd))
    return w.astype(BF16), bias.astype(F32)


def kernel(x_prompt, x_sample, cache_sb_k, cache_sb_v, state_gla, state_pool, sb_w_qkv, sb_w_o, sgu_w_in, sgu_ln_g, sgu_ln_b, sgu_w_s, sgu_b_s, sgu_w_out, gla_w_in, gla_w_g1, gla_w_g2, gla_b_g, gla_norm_g, gla_w_o, pool_w, pool_scale, ln1_g, ln1_b, ln2_g, ln2_b, moe_w_group, moe_b_group, moe_w_router, moe_b_router, moe_w_up, moe_w_down):
    depth = ln1_g.shape[0]
    d = x_prompt.shape[-1]
    alpha = (2.0 * depth) ** 0.25
    sb_heads, sb_hd = cache_sb_k.shape[3], cache_sb_k.shape[4]
    gla_heads, gla_dk, gla_dv = state_gla.shape[2], state_gla.shape[3], state_gla.shape[4]
    n_grp = moe_w_group.shape[-1]
    dq, dvt = gla_heads * gla_dk, gla_heads * gla_dv
    n_hist = state_pool.shape[2]
    sgu_groups = sgu_w_s.shape[1]
    sgu_width = sgu_w_out.shape[1]

    bf = lambda a: a.astype(BF16)
    vec = lambda a: a.reshape(1, -1)

    def run(x3, past):
        batch, seq, _ = x3.shape
        n = batch * seq
        x = x3.reshape(n, d)
        sb_k, sb_v, sgu_v, gla_s, pool_h = [], [], [], [], []
        for i in range(depth):
            m, j = i % 4, i // 4
            g1, b1 = vec(ln1_g[i]), vec(ln1_b[i])
            if m == 0:
                wq, wk, wv = (bf(sb_w_qkv[j][:, c * d:(c + 1) * d]) for c in range(3))
                q, k, v = _proj(x, [wq, wk, wv], [BF16, F32, F32], "sb_qkv")
                if past is None:
                    o = _sb_prompt(q, k, v, batch, seq, sb_hd)
                else:
                    kp = past[0][j].reshape(batch, -1, d)
                    vp = past[1][j].reshape(batch, -1, d)
                    o = _sb_sample(q, k, v, kp, vp, batch, seq, sb_hd)
                sb_k.append(k.reshape(batch, seq, sb_heads, sb_hd))
                sb_v.append(v.reshape(batch, seq, sb_heads, sb_hd))
                x = _mm_res_ln(o, bf(sb_w_o[j]), x, g1, b1, alpha, "sb_out")
            elif m == 1:
                ws_mix, bs_mix = _sgu_mix_params(sgu_w_s[j], sgu_b_s[j], seq, sgu_width // sgu_groups)
                x, v = _sgu(x, bf(sgu_w_in[j]), vec(sgu_ln_g[j]), vec(sgu_ln_b[j]), ws_mix, bs_mix,
                            bf(sgu_w_out[j]), g1, b1, alpha, past is not None)
                if v is not None:
                    sgu_v.append(v.reshape(batch, seq, sgu_width))
            elif m == 2:
                w_in = gla_w_in[j]
                wg1 = jnp.zeros((d, LANES), F32).at[:, :gla_w_g1.shape[-1]].set(gla_w_g1[j])
                wg2 = jnp.zeros((LANES, dq), F32).at[:gla_w_g2.shape[1]].set(gla_w_g2[j])
                q, k, v, r, la = _gla_proj(
                    x, bf(w_in[:, :dq]), bf(w_in[:, dq:2 * dq]), bf(w_in[:, 2 * dq:2 * dq + dvt]),
                    bf(w_in[:, 2 * dq + dvt:]), bf(wg1), bf(wg2), vec(gla_b_g[j]), gla_dk ** -0.5)
                if past is None:
                    s0 = jnp.zeros((batch, gla_heads, gla_dk, gla_dv), F32)
                    seq_p = seq
                else:
                    s0 = past[2][j]
                    seq_p = -(-seq // SGU_LEN) * SGU_LEN
                    padrows = lambda a: jnp.pad(a.reshape(batch, seq, -1),
                                                ((0, 0), (0, seq_p - seq), (0, 0))).reshape(batch * seq_p, -1)
                    q, k, v, la = padrows(q), padrows(k), padrows(v), padrows(la)
                o, s = _gla_scan(q, k, v, la, s0, batch, seq_p, gla_heads)
                if seq_p != seq:
                    o = o.reshape(batch, seq_p, dvt)[:, :seq].reshape(n, dvt)
                gla_s.append(s)
                x = _gla_out(o, r, vec(gla_norm_g[j]), bf(gla_w_o[j]), x, g1, b1, alpha, gla_heads)
            else:
                if past is None:
                    hist = jnp.zeros((batch, POOL_PAD, d), F32)
                    n_valid = 0
                else:
                    hist = jnp.pad(past[3][j], ((0, 0), (POOL_PAD - n_hist, 0), (0, 0)))
                    n_valid = n_hist
                x_in = x.reshape(batch, seq, d)
                if past is None:
                    pool_h.append(x_in[:, seq - n_hist:])
                else:
                    pool_h.append(jnp.concatenate([past[3][j], x_in], axis=1)[:, -n_hist:])
                x = _pool(x, hist, bf(pool_w[j]), vec(pool_scale[j]), g1, b1, alpha, batch, seq, n_valid)
            wr_hi, wr_lo, br = _router_params(moe_w_group[i], moe_b_group[i], moe_w_router[i], moe_b_router[i])
            x = _moe(x, wr_hi, wr_lo, br, bf(moe_w_up[i]), bf(moe_w_down[i]),
                     vec(ln2_g[i]), vec(ln2_b[i]), alpha, n_grp)
        return x.reshape(batch, seq, d), sb_k, sb_v, sgu_v, gla_s, pool_h

    y_p, kp, vp, _, sp, hp = run(x_prompt, None)
    y_s, ks, vs, us, ss, hs = run(x_sample, (cache_sb_k, cache_sb_v, state_gla, state_pool))
    return (y_p, y_s, jnp.stack(kp), jnp.stack(vp), jnp.stack(ks), jnp.stack(vs), jnp.stack(us),
            jnp.stack(sp), jnp.stack(ss), jnp.stack(hp), jnp.stack(hs))
```

```python
import functools
import math

import jax
import jax.numpy as jnp
from jax import lax
from jax.experimental import pallas as pl
from jax.experimental.pallas import tpu as pltpu

F32 = jnp.float32
BF16 = jnp.bfloat16

LN_EPS = 1e-5
CHUNK = 64
SGU_LEN = 128
SB_QBLOCK = 128
SB_SPAN_BLOCKS = 3
SB_UNDERFLOW = 104.0
GLA_BLOCK = 16
GLA_TAU = 16.0
POOL_WINDOWS = (2, 4, 8, 16)
POOL_PAD = 16
MOE_EXPERTS_PER_GROUP = 4
LANES = 128
VMEM_LIMIT = 56 * 1024 * 1024


def _cparams(*sem):
    return pltpu.CompilerParams(dimension_semantics=sem, vmem_limit_bytes=VMEM_LIMIT)


def _tile(n, pref):
    t = min(n, pref)
    while n % t:
        t -= 8
    return t


def _layer_norm(x, g, b):
    mu = jnp.mean(x, axis=-1, keepdims=True)
    xc = x - mu
    var = jnp.mean(xc * xc, axis=-1, keepdims=True)
    return xc * lax.rsqrt(var + LN_EPS) * g + b


def _split_bf16(x):
    hi = x.astype(BF16)
    lo = (x - hi.astype(F32)).astype(BF16)
    return hi, lo


def _log_sigmoid(z):
    return jnp.minimum(z, 0.0) - jnp.log(1.0 + jnp.exp(-jnp.abs(z)))


def _silu(x):
    return x / (1.0 + jnp.exp(-x))


def _proj_kernel(x_ref, *refs, outs):
    n_w = len(refs) - len(outs)
    xb = x_ref[...].astype(BF16)
    ys = [jnp.dot(xb, w_ref[...], preferred_element_type=F32) for w_ref in refs[:n_w]]
    for (wi, _, scale), o_ref in zip(outs, refs[n_w:]):
        o_ref[...] = (ys[wi] * scale).astype(o_ref.dtype)


def _proj(x, ws, outs, name):
    n, k = x.shape
    tm = _tile(n, 512)
    return pl.pallas_call(
        functools.partial(_proj_kernel, outs=tuple(outs)),
        grid=(n // tm,),
        in_specs=[pl.BlockSpec((tm, k), lambda i: (i, 0))]
        + [pl.BlockSpec(w.shape, lambda i: (0, 0)) for w in ws],
        out_specs=[pl.BlockSpec((tm, ws[wi].shape[1]), lambda i: (i, 0)) for wi, _, _ in outs],
        out_shape=[jax.ShapeDtypeStruct((n, ws[wi].shape[1]), dt) for wi, dt, _ in outs],
        compiler_params=_cparams("parallel"),
        name=name,
    )(x, *ws)


def _mm_res_ln_kernel(a_ref, w_ref, res_ref, g_ref, b_ref, o_ref, *, alpha):
    y = jnp.dot(a_ref[...].astype(BF16), w_ref[...], preferred_element_type=F32)
    o_ref[...] = _layer_norm(alpha * res_ref[...] + y, g_ref[...], b_ref[...])


def _mm_res_ln(a, w, res, g, b, alpha, name):
    n, k = a.shape
    d = w.shape[1]
    tm = _tile(n, 512)
    return pl.pallas_call(
        functools.partial(_mm_res_ln_kernel, alpha=alpha),
        grid=(n // tm,),
        in_specs=[pl.BlockSpec((tm, k), lambda i: (i, 0)),
                  pl.BlockSpec((k, d), lambda i: (0, 0)),
                  pl.BlockSpec((tm, d), lambda i: (i, 0)),
                  pl.BlockSpec((1, d), lambda i: (0, 0)),
                  pl.BlockSpec((1, d), lambda i: (0, 0))],
        out_specs=pl.BlockSpec((tm, d), lambda i: (i, 0)),
        out_shape=jax.ShapeDtypeStruct((n, d), F32),
        compiler_params=_cparams("parallel"),
        name=name,
    )(a, w, res, g, b)


def _sb_later_selector(span):
    s = jnp.arange(span)
    return (s[:, None] > s[None, :]).astype(BF16)


def _sb_stack_heads(q, head_dim):
    lane = lax.broadcasted_iota(jnp.int32, q.shape, 1)
    parts = [jnp.where((lane >= h * head_dim) & (lane < (h + 1) * head_dim), q, jnp.zeros_like(q))
             for h in range(LANES // head_dim)]
    return jnp.concatenate(parts, axis=0)


def _sb_unstack_heads(acc, tq, head_dim):
    lane = lax.broadcasted_iota(jnp.int32, (tq, LANES), 1)
    out = acc[:tq]
    for h in range(1, LANES // head_dim):
        out = jnp.where(lane >= h * head_dim, acc[h * tq:(h + 1) * tq], out)
    return out


def _sb_later_sums(lstay, u_ref, off):
    n = lstay.shape[1]
    hi, lo = _split_bf16(lstay)
    cols = []
    for c0 in range(0, n, 2 * LANES):
        c1 = min(c0 + 2 * LANES, n)
        u = u_ref[off + c0:off + n, off + c0:off + c1]
        cols.append(jnp.dot(hi[:, c0:], u, preferred_element_type=F32)
                    + jnp.dot(lo[:, c0:], u, preferred_element_type=F32))
    return cols[0] if len(cols) == 1 else jnp.concatenate(cols, axis=1)


def _sb_span(qs, k, v, valid, u_ref, u_off, acc, carry):
    z = lax.dot_general(qs, k, (((1,), (1,)), ((), ())), preferred_element_type=F32)
    ls = _log_sigmoid(z)
    lstay = ls - z
    if valid is not None:
        lstay = jnp.where(valid, lstay, 0.0)
    later = _sb_later_sums(lstay, u_ref, u_off)
    if carry is not None:
        later = later + carry
    w = jnp.exp(ls + later)
    if valid is not None:
        w = jnp.where(valid, w, 0.0)
    pv = jnp.dot(w.astype(BF16), v, preferred_element_type=F32)
    tot = jnp.sum(lstay, axis=-1, keepdims=True)
    if acc is None:
        return pv, tot
    return acc + pv, carry + tot


def _sb_prompt_kernel(q_ref, k_ref, v_ref, u_ref, o_ref, *, head_dim, span):
    qi = pl.program_id(2)
    tq = q_ref.shape[0]
    qs = _sb_stack_heads(q_ref[...], head_dim)
    rows = qs.shape[0]
    first = jnp.maximum(qi - (span // SB_QBLOCK - 1), 0)
    start = pl.multiple_of(first * SB_QBLOCK, SB_QBLOCK)
    row = lax.broadcasted_iota(jnp.int32, (rows, span), 0)
    col = lax.broadcasted_iota(jnp.int32, (rows, span), 1)
    valid = (start + col) < (qi * tq + (row & (tq - 1)))
    acc, carry = _sb_span(qs, k_ref[pl.ds(start, span), :], v_ref[pl.ds(start, span), :],
                          valid, u_ref, 0, None, None)

    def cond(state):
        j, _, carry = state
        return (j >= 0) & (jnp.max(carry) > -SB_UNDERFLOW)

    def body(state):
        j, acc, carry = state
        s = pl.multiple_of(j * SB_QBLOCK, SB_QBLOCK)
        acc, carry = _sb_span(qs, k_ref[pl.ds(s, SB_QBLOCK), :], v_ref[pl.ds(s, SB_QBLOCK), :],
                              None, u_ref, 0, acc, carry)
        return j - 1, acc, carry

    _, acc, _ = lax.while_loop(cond, body, (first - 1, acc, carry))
    o_ref[...] = _sb_unstack_heads(acc, tq, head_dim).astype(o_ref.dtype)


def _sb_prompt(q, k, v, batch, seq, head_dim):
    n, d = q.shape
    nqb = seq // SB_QBLOCK
    span = min(SB_SPAN_BLOCKS, nqb) * SB_QBLOCK
    u = _sb_later_selector(span)
    return pl.pallas_call(
        functools.partial(_sb_prompt_kernel, head_dim=head_dim, span=span),
        grid=(batch, d // LANES, nqb),
        in_specs=[pl.BlockSpec((SB_QBLOCK, LANES), lambda b, hp, qi: (b * nqb + qi, hp)),
                  pl.BlockSpec((seq, LANES), lambda b, hp, qi: (b, hp)),
                  pl.BlockSpec((seq, LANES), lambda b, hp, qi: (b, hp)),
                  pl.BlockSpec((span, span), lambda b, hp, qi: (0, 0))],
        out_specs=pl.BlockSpec((SB_QBLOCK, LANES), lambda b, hp, qi: (b * nqb + qi, hp)),
        out_shape=jax.ShapeDtypeStruct((n, d), BF16),
        compiler_params=_cparams("parallel", "parallel", "arbitrary"),
        name="sb_prompt",
    )(q, k, v, u)


def _sb_sample_kernel(q_ref, kn_ref, vn_ref, kt_ref, vt_ref, kp_hbm, vp_hbm, u_ref, o_ref,
                      kbuf, vbuf, sem, *, head_dim, past):
    b, hp = pl.program_id(0), pl.program_id(1)
    tq = q_ref.shape[0]
    tail = kt_ref.shape[0]
    qs = _sb_stack_heads(q_ref[...], head_dim)
    rows = qs.shape[0]
    pad = jnp.zeros((SB_QBLOCK - tq, LANES), BF16)
    k = jnp.concatenate([kt_ref[...].astype(BF16), kn_ref[...], pad], axis=0)
    v = jnp.concatenate([vt_ref[...].astype(BF16), vn_ref[...], pad], axis=0)
    span = tail + SB_QBLOCK
    row = lax.broadcasted_iota(jnp.int32, (rows, span), 0)
    col = lax.broadcasted_iota(jnp.int32, (rows, span), 1)
    valid = col < tail + (row & (tq - 1))
    acc, carry = _sb_span(qs, k, v, valid, u_ref, 0, None, None)

    def block_copies(j):
        src = (b, pl.ds(j * SB_QBLOCK, SB_QBLOCK), pl.ds(hp * LANES, LANES))
        return (pltpu.make_async_copy(kp_hbm.at[src], kbuf, sem.at[0]),
                pltpu.make_async_copy(vp_hbm.at[src], vbuf, sem.at[1]))

    def cond(state):
        j, _, carry = state
        return (j >= 0) & (jnp.max(carry) > -SB_UNDERFLOW)

    def body(state):
        j, acc, carry = state
        kc, vc = block_copies(j)
        kc.start()
        vc.start()
        kc.wait()
        vc.wait()
        acc, carry = _sb_span(qs, kbuf[...].astype(BF16), vbuf[...].astype(BF16),
                              None, u_ref, 0, acc, carry)
        return j - 1, acc, carry

    _, acc, _ = lax.while_loop(cond, body, ((past - tail) // SB_QBLOCK - 1, acc, carry))
    o_ref[...] = _sb_unstack_heads(acc, tq, head_dim).astype(o_ref.dtype)


def _sb_sample(q, kn, vn, kp, vp, batch, tq, head_dim):
    n, d = q.shape
    past = kp.shape[1]
    tail = min(SB_SPAN_BLOCKS - 1, past // SB_QBLOCK) * SB_QBLOCK
    u = _sb_later_selector(tail + SB_QBLOCK)
    new = pl.BlockSpec((tq, LANES), lambda b, hp: (b, hp))
    tail_spec = pl.BlockSpec((None, tail, LANES), lambda b, hp: (b, past // tail - 1, hp))
    return pl.pallas_call(
        functools.partial(_sb_sample_kernel, head_dim=head_dim, past=past),
        grid=(batch, d // LANES),
        in_specs=[new, new, new, tail_spec, tail_spec,
                  pl.BlockSpec(memory_space=pl.ANY), pl.BlockSpec(memory_space=pl.ANY),
                  pl.BlockSpec(u.shape, lambda b, hp: (0, 0))],
        out_specs=new,
        out_shape=jax.ShapeDtypeStruct((n, d), BF16),
        scratch_shapes=[pltpu.VMEM((SB_QBLOCK, LANES), F32), pltpu.VMEM((SB_QBLOCK, LANES), F32),
                        pltpu.SemaphoreType.DMA((2,))],
        compiler_params=_cparams("parallel", "parallel"),
        name="sb_sample",
    )(q, kn, vn, kp, vp, kp, vp, u)


def _gelu_tanh(x):
    c = math.sqrt(2.0 / math.pi)
    return 0.5 * x * (1.0 + jnp.tanh(c * (x + 0.044715 * (x * x * x))))


def _sgu_kernel(x_ref, win_ref, lng_ref, lnb_ref, ws_ref, bs_ref, wout_ref, g_ref, b_ref,
                o_ref, *v_out, alpha, width, groups):
    x = x_ref[...]
    z = _gelu_tanh(jnp.dot(x.astype(BF16), win_ref[...], preferred_element_type=F32))
    u = z[:, :width]
    v = _layer_norm(z[:, width:], lng_ref[...], lnb_ref[...])
    if v_out:
        v_out[0][...] = v
    tm = x.shape[0]
    gd = width // groups
    vb = v.astype(BF16)
    rows = []
    for c in range(tm // SGU_LEN):
        r0 = c * SGU_LEN
        cols = []
        for g in range(groups):
            mixed = jnp.dot(ws_ref[g], vb[r0:r0 + SGU_LEN, g * gd:(g + 1) * gd],
                            preferred_element_type=F32) + bs_ref[g]
            cols.append(mixed)
        rows.append(jnp.concatenate(cols, axis=1))
    mixed = rows[0] if len(rows) == 1 else jnp.concatenate(rows, axis=0)
    y = jnp.dot((u * mixed).astype(BF16), wout_ref[...], preferred_element_type=F32)
    o_ref[...] = _layer_norm(alpha * x + y, g_ref[...], b_ref[...])


def _sgu(x, w_in, ln_g, ln_b, ws_mix, bs_mix, w_out, g, b, alpha, want_v):
    n, d = x.shape
    width = w_out.shape[0]
    groups = ws_mix.shape[0]
    tm = _tile(n, 256)
    const2 = lambda i: (0, 0)
    const3 = lambda i: (0, 0, 0)
    out_specs = [pl.BlockSpec((tm, d), lambda i: (i, 0))]
    out_shape = [jax.ShapeDtypeStruct((n, d), F32)]
    if want_v:
        out_specs.append(pl.BlockSpec((tm, width), lambda i: (i, 0)))
        out_shape.append(jax.ShapeDtypeStruct((n, width), F32))
    res = pl.pallas_call(
        functools.partial(_sgu_kernel, alpha=alpha, width=width, groups=groups),
        grid=(n // tm,),
        in_specs=[pl.BlockSpec((tm, d), lambda i: (i, 0)),
                  pl.BlockSpec(w_in.shape, const2),
                  pl.BlockSpec((1, width), const2),
                  pl.BlockSpec((1, width), const2),
                  pl.BlockSpec(ws_mix.shape, const3),
                  pl.BlockSpec(bs_mix.shape, const3),
                  pl.BlockSpec(w_out.shape, const2),
                  pl.BlockSpec((1, d), const2),
                  pl.BlockSpec((1, d), const2)],
        out_specs=out_specs,
        out_shape=out_shape,
        compiler_params=_cparams("parallel"),
        name="sgu",
    )(x, w_in, ln_g, ln_b, ws_mix, bs_mix, w_out, g, b)
    return res[0], (res[1] if want_v else None)


def _gla_proj_kernel(x_ref, wq_ref, wk_ref, wv_ref, wr_ref, wg1_ref, wg2_ref, bg_ref,
                     q_ref, k_ref, v_ref, r_ref, la_ref, *, qscale):
    xb = x_ref[...].astype(BF16)
    q_ref[...] = jnp.dot(xb, wq_ref[...], preferred_element_type=F32) * qscale
    k_ref[...] = jnp.dot(xb, wk_ref[...], preferred_element_type=F32)
    v_ref[...] = jnp.dot(xb, wv_ref[...], preferred_element_type=F32).astype(BF16)
    r_ref[...] = jnp.dot(xb, wr_ref[...], preferred_element_type=F32)
    t = jnp.dot(xb, wg1_ref[...], preferred_element_type=F32)
    gate = jnp.dot(t.astype(BF16), wg2_ref[...], preferred_element_type=F32) + bg_ref[...]
    la_ref[...] = _log_sigmoid(gate) / GLA_TAU


def _gla_proj(x, wq, wk, wv, wr, wg1, wg2, bg, qscale):
    n, d = x.shape
    dq, dv = wq.shape[1], wv.shape[1]
    tm = _tile(n, 512)
    const = lambda i: (0, 0)
    row = lambda i: (i, 0)
    return pl.pallas_call(
        functools.partial(_gla_proj_kernel, qscale=qscale),
        grid=(n // tm,),
        in_specs=[pl.BlockSpec((tm, d), row)]
        + [pl.BlockSpec(w.shape, const) for w in (wq, wk, wv, wr, wg1, wg2, bg)],
        out_specs=[pl.BlockSpec((tm, dq), row), pl.BlockSpec((tm, dq), row),
                   pl.BlockSpec((tm, dv), row), pl.BlockSpec((tm, dv), row),
                   pl.BlockSpec((tm, dq), row)],
        out_shape=[jax.ShapeDtypeStruct((n, dq), F32), jax.ShapeDtypeStruct((n, dq), F32),
                   jax.ShapeDtypeStruct((n, dv), BF16), jax.ShapeDtypeStruct((n, dv), F32),
                   jax.ShapeDtypeStruct((n, dq), F32)],
        compiler_params=_cparams("parallel"),
        name="gla_proj",
    )(x, wq, wk, wv, wr, wg1, wg2, bg)


def _gla_scan_kernel(q_ref, k_ref, v_ref, la_ref, s0_ref, o_ref, sfin_ref, st_ref,
                     *, heads, dk, dv):
    ti = pl.program_id(1)
    tt = q_ref.shape[0]
    nsb = tt // GLA_BLOCK

    @pl.when(ti == 0)
    def _():
        for h in range(heads):
            st_ref[h] = s0_ref[0, h].T

    row = lax.broadcasted_iota(jnp.int32, (tt, tt), 0)
    col = lax.broadcasted_iota(jnp.int32, (tt, tt), 1)
    same = (row // GLA_BLOCK) == (col // GLA_BLOCK)
    tril = same & (col <= row)
    bdcat = jnp.concatenate([jnp.where(tril, 1.0, 0.0), jnp.where(same, 1.0, 0.0)],
                            axis=0).astype(BF16)
    rsub = lax.broadcasted_iota(jnp.int32, (tt, dk), 0) // GLA_BLOCK

    for h in range(heads):
        ks = slice(h * dk, (h + 1) * dk)
        vs = slice(h * dv, (h + 1) * dv)
        hi, lo = _split_bf16(la_ref[:, ks])
        ct = (jnp.dot(bdcat, hi, preferred_element_type=F32)
              + jnp.dot(bdcat, lo, preferred_element_type=F32))
        cum, tot = ct[:tt], ct[tt:]
        k = k_ref[:, ks]
        qd = (q_ref[:, ks] * jnp.exp(cum)).astype(BF16)
        kd = (k * jnp.exp(-cum)).astype(BF16)
        kt = k * jnp.exp(tot - cum)
        sc = lax.dot_general(qd, kd, (((1,), (1,)), ((), ())), preferred_element_type=F32)
        sc = jnp.where(tril, sc, 0.0).astype(BF16)
        vh = v_ref[:, vs]
        o_intra = jnp.dot(sc, vh, preferred_element_type=F32)
        ktcat = jnp.concatenate(
            [jnp.where(rsub == i, kt, 0.0).astype(BF16) for i in range(nsb)], axis=1)
        ut = lax.dot_general(vh, ktcat, (((0,), (0,)), ((), ())), preferred_element_type=F32)
        st = st_ref[h]
        parts = []
        for i in range(nsb):
            r0 = i * GLA_BLOCK
            parts.append(lax.dot_general(qd[r0:r0 + GLA_BLOCK], st.astype(BF16),
                                         (((1,), (1,)), ((), ())), preferred_element_type=F32))
            st = st * jnp.exp(tot[r0:r0 + 1, :]) + ut[:, i * dk:(i + 1) * dk]
        st_ref[h] = st
        o_ref[:, vs] = o_intra + jnp.concatenate(parts, axis=0)

    @pl.when(ti == pl.num_programs(1) - 1)
    def _():
        for h in range(heads):
            sfin_ref[0, h] = st_ref[h].T


def _gla_scan(q, k, v, la, s0, batch, seq, heads):
    n, dq = q.shape
    dvt = v.shape[1]
    dk, dv = dq // heads, dvt // heads
    tt = SGU_LEN
    nt = seq // tt
    row = lambda b, t: (b * nt + t, 0)
    st = lambda b, t: (b, 0, 0, 0)
    return pl.pallas_call(
        functools.partial(_gla_scan_kernel, heads=heads, dk=dk, dv=dv),
        grid=(batch, nt),
        in_specs=[pl.BlockSpec((tt, dq), row), pl.BlockSpec((tt, dq), row),
                  pl.BlockSpec((tt, dvt), row), pl.BlockSpec((tt, dq), row),
                  pl.BlockSpec((1, heads, dk, dv), st)],
        out_specs=[pl.BlockSpec((tt, dvt), row), pl.BlockSpec((1, heads, dk, dv), st)],
        out_shape=[jax.ShapeDtypeStruct((n, dvt), F32),
                   jax.ShapeDtypeStruct((batch, heads, dk, dv), F32)],
        scratch_shapes=[pltpu.VMEM((heads, dv, dk), F32)],
        compiler_params=_cparams("parallel", "arbitrary"),
        name="gla_scan",
    )(q, k, v, la, s0)


def _gla_out_kernel(o_ref, r_ref, ng_ref, wo_ref, x_ref, g_ref, b_ref, y_ref, *, alpha, heads):
    o = o_ref[...]
    dv = o.shape[1] // heads
    normed = []
    for h in range(heads):
        oh = o[:, h * dv:(h + 1) * dv]
        ms = jnp.mean(oh * oh, axis=-1, keepdims=True)
        normed.append(oh * lax.rsqrt(ms + LN_EPS) * ng_ref[...])
    gated = jnp.concatenate(normed, axis=1) * _silu(r_ref[...])
    y = jnp.dot(gated.astype(BF16), wo_ref[...], preferred_element_type=F32)
    y_ref[...] = _layer_norm(alpha * x_ref[...] + y, g_ref[...], b_ref[...])


def _gla_out(o, r, norm_g, w_o, x, g, b, alpha, heads):
    n, dvt = o.shape
    d = x.shape[1]
    tm = _tile(n, 512)
    row = lambda i: (i, 0)
    const = lambda i: (0, 0)
    return pl.pallas_call(
        functools.partial(_gla_out_kernel, alpha=alpha, heads=heads),
        grid=(n // tm,),
        in_specs=[pl.BlockSpec((tm, dvt), row), pl.BlockSpec((tm, dvt), row),
                  pl.BlockSpec(norm_g.shape, const), pl.BlockSpec(w_o.shape, const),
                  pl.BlockSpec((tm, d), row), pl.BlockSpec((1, d), const),
                  pl.BlockSpec((1, d), const)],
        out_specs=pl.BlockSpec((tm, d), row),
        out_shape=jax.ShapeDtypeStruct((n, d), F32),
        compiler_params=_cparams("parallel"),
        name="gla_out",
    )(o, r, norm_g, w_o, x, g, b)


def _pool_kernel(x_ref, prev_ref, hist_ref, w_ref, sc_ref, g_ref, b_ref, o_ref, buf_ref,
                 *, alpha, n_valid):
    ti = pl.program_id(1)
    tm, d = x_ref.shape
    x = x_ref[...]

    @pl.when(ti == 0)
    def _():
        buf_ref[0:POOL_PAD, :] = hist_ref[0]

    @pl.when(ti > 0)
    def _():
        buf_ref[0:POOL_PAD, :] = prev_ref[...]

    buf_ref[POOL_PAD:POOL_PAD + tm, :] = x
    gd = d // len(POOL_WINDOWS)
    pos = ti * tm + lax.broadcasted_iota(jnp.int32, (tm, 1), 0)
    ys = []
    for g, win in enumerate(POOL_WINDOWS):
        cs = slice(g * gd, (g + 1) * gd)
        acc = x[:, cs]
        for j in range(1, win):
            acc = acc + buf_ref[POOL_PAD - j:POOL_PAD - j + tm, cs]
        count = jnp.minimum(pos + 1 + n_valid, win).astype(F32)
        dlt = acc / count - x[:, cs]
        ys.append(jnp.dot(dlt.astype(BF16), w_ref[g], preferred_element_type=F32))
    y = jnp.concatenate(ys, axis=1) * sc_ref[...]
    o_ref[...] = _layer_norm(alpha * x + y, g_ref[...], b_ref[...])


def _pool(x, hist, w_pool, scale, g, b, alpha, batch, seq, n_valid):
    n, d = x.shape
    tm = _tile(seq, 256)
    nt = seq // tm
    ratio = tm // POOL_PAD
    const2 = lambda bb, t: (0, 0)
    row = lambda bb, t: (bb * nt + t, 0)
    return pl.pallas_call(
        functools.partial(_pool_kernel, alpha=alpha, n_valid=n_valid),
        grid=(batch, nt),
        in_specs=[pl.BlockSpec((tm, d), row),
                  pl.BlockSpec((POOL_PAD, d),
                               lambda bb, t: (jnp.maximum((bb * nt + t) * ratio - 1, 0), 0)),
                  pl.BlockSpec((1, POOL_PAD, d), lambda bb, t: (bb, 0, 0)),
                  pl.BlockSpec(w_pool.shape, lambda bb, t: (0, 0, 0)),
                  pl.BlockSpec((1, d), const2), pl.BlockSpec((1, d), const2),
                  pl.BlockSpec((1, d), const2)],
        out_specs=pl.BlockSpec((tm, d), row),
        out_shape=jax.ShapeDtypeStruct((n, d), F32),
        scratch_shapes=[pltpu.VMEM((POOL_PAD + tm, d), F32)],
        compiler_params=_cparams("parallel", "arbitrary"),
        name="pool",
    )(x, x, hist, w_pool, scale, g, b)


def _moe_gates(x, wr_hi, wr_lo, br, n_exp, n_grp):
    hi, lo = _split_bf16(x)
    logit = (jnp.dot(hi, wr_hi, preferred_element_type=F32)
             + jnp.dot(lo, wr_hi, preferred_element_type=F32)
             + jnp.dot(hi, wr_lo, preferred_element_type=F32)) + br
    lane_i = lax.broadcasted_iota(jnp.int32, logit.shape, 1)
    lane = lane_i.astype(F32)
    neg = jnp.float32(-jnp.inf)
    big = jnp.float32(LANES)
    is_grp = (lane_i >= n_exp) & (lane_i < n_exp + n_grp)
    gl = jnp.where(is_grp, logit, neg)
    gmax = jnp.max(gl, axis=-1, keepdims=True)
    g_sel = jnp.min(jnp.where(gl == gmax, lane, big), axis=-1, keepdims=True) - n_exp
    g_prob = 1.0 / jnp.sum(jnp.where(is_grp, jnp.exp(gl - gmax), 0.0), axis=-1, keepdims=True)
    lane_grp = (lane_i // MOE_EXPERTS_PER_GROUP).astype(F32)
    in_grp = (lane_i < n_exp) & (lane_grp == g_sel)
    el = jnp.where(in_grp, logit, neg)
    m1 = jnp.max(el, axis=-1, keepdims=True)
    i1 = jnp.min(jnp.where(el == m1, lane, big), axis=-1, keepdims=True)
    el2 = jnp.where(lane == i1, neg, el)
    m2 = jnp.max(el2, axis=-1, keepdims=True)
    i2 = jnp.min(jnp.where(el2 == m2, lane, big), axis=-1, keepdims=True)
    e2 = jnp.exp(m2 - m1)
    w1 = g_prob / (1.0 + e2)
    w2 = g_prob * e2 / (1.0 + e2)
    return jnp.where(lane == i1, w1, 0.0) + jnp.where(lane == i2, w2, 0.0)


def _moe_dense_kernel(x_ref, wrh_ref, wrl_ref, br_ref, wup_ref, wdn_ref, g_ref, b_ref, o_ref,
                      xb_ref, gate_ref, acc_ref, *, alpha, n_exp, n_grp, hidden):
    e = pl.program_id(1)

    @pl.when(e == 0)
    def _():
        x = x_ref[...]
        xb_ref[...] = x.astype(BF16)
        gate_ref[...] = _moe_gates(x, wrh_ref[...], wrl_ref[...], br_ref[...], n_exp, n_grp)
        acc_ref[...] = jnp.zeros_like(acc_ref)

    gate = gate_ref[...]
    lane = lax.broadcasted_iota(jnp.int32, gate.shape, 1)
    ge = jnp.sum(jnp.where(lane == e, gate, 0.0), axis=-1, keepdims=True)
    h = jnp.dot(xb_ref[...], wup_ref[0], preferred_element_type=F32)
    act = _silu(h[:, :hidden]) * h[:, hidden:] * ge
    acc_ref[...] += jnp.dot(act.astype(BF16), wdn_ref[0], preferred_element_type=F32)

    @pl.when(e == n_exp - 1)
    def _():
        o_ref[...] = _layer_norm(alpha * x_ref[...] + acc_ref[...], g_ref[...], b_ref[...])


def _moe(x, wr_hi, wr_lo, br, w_up, w_dn, g, b, alpha, n_grp):
    n, d = x.shape
    n_exp, _, two_f = w_up.shape
    hidden = two_f // 2
    tm = _tile(n, 512)
    const = lambda i, e: (0, 0)
    row = lambda i, e: (i, 0)
    return pl.pallas_call(
        functools.partial(_moe_dense_kernel, alpha=alpha, n_exp=n_exp, n_grp=n_grp, hidden=hidden),
        grid=(n // tm, n_exp),
        in_specs=[pl.BlockSpec((tm, d), row),
                  pl.BlockSpec(wr_hi.shape, const), pl.BlockSpec(wr_lo.shape, const),
                  pl.BlockSpec((1, LANES), const),
                  pl.BlockSpec((1, d, two_f), lambda i, e: (e, 0, 0)),
                  pl.BlockSpec((1, hidden, d), lambda i, e: (e, 0, 0)),
                  pl.BlockSpec((1, d), const), pl.BlockSpec((1, d), const)],
        out_specs=pl.BlockSpec((tm, d), row),
        out_shape=jax.ShapeDtypeStruct((n, d), F32),
        scratch_shapes=[pltpu.VMEM((tm, d), BF16), pltpu.VMEM((tm, LANES), F32),
                        pltpu.VMEM((tm, d), F32)],
        compiler_params=_cparams("parallel", "arbitrary"),
        name="moe",
    )(x, wr_hi, wr_lo, br, w_up, w_dn, g, b)


def _router_params(w_group, b_group, w_router, b_router):
    d, n_grp = w_group.shape
    n_exp = w_router.shape[1]
    w = jnp.zeros((d, LANES), F32).at[:, :n_exp].set(w_router).at[:, n_exp:n_exp + n_grp].set(w_group)
    hi = w.astype(BF16)
    lo = (w - hi.astype(F32)).astype(BF16)
    br = jnp.zeros((1, LANES), F32).at[0, :n_exp].set(b_router).at[0, n_exp:n_exp + n_grp].set(b_group)
    return hi, lo, br


def _sgu_mix_params(w_s, b_s, t_len, gd):
    groups = w_s.shape[0]
    L = min(t_len, SGU_LEN)
    pos = jnp.arange(L)
    allowed = (pos[None, :] // CHUNK) <= (pos[:, None] // CHUNK)
    w = jnp.where(allowed[None], w_s[:, :L, :L], 0.0)
    rep = SGU_LEN // L
    if rep > 1:
        eye = jnp.eye(rep, dtype=F32)
        w = jnp.einsum("ab,gts->gatbs", eye, w).reshape(groups, SGU_LEN, SGU_LEN)
    bias = jnp.tile(b_s[:, :L], (1, rep))
    bias = jnp.broadcast_to(bias[:, :, None], (groups, SGU_LEN, gd))
    return w.astype(BF16), bias.astype(F32)


def kernel(x_prompt, x_sample, cache_sb_k, cache_sb_v, state_gla, state_pool, sb_w_qkv, sb_w_o, sgu_w_in, sgu_ln_g, sgu_ln_b, sgu_w_s, sgu_b_s, sgu_w_out, gla_w_in, gla_w_g1, gla_w_g2, gla_b_g, gla_norm_g, gla_w_o, pool_w, pool_scale, ln1_g, ln1_b, ln2_g, ln2_b, moe_w_group, moe_b_group, moe_w_router, moe_b_router, moe_w_up, moe_w_down):
    depth = ln1_g.shape[0]
    d = x_prompt.shape[-1]
    alpha = (2.0 * depth) ** 0.25
    sb_heads, sb_hd = cache_sb_k.shape[3], cache_sb_k.shape[4]
    gla_heads, gla_dk, gla_dv = state_gla.shape[2], state_gla.shape[3], state_gla.shape[4]
    n_grp = moe_w_group.shape[-1]
    dq, dvt = gla_heads * gla_dk, gla_heads * gla_dv
    n_hist = state_pool.shape[2]
    sgu_groups = sgu_w_s.shape[1]
    sgu_width = sgu_w_out.shape[1]

    bf = lambda a: a.astype(BF16)
    vec = lambda a: a.reshape(1, -1)

    def run(x3, past):
        batch, seq, _ = x3.shape
        n = batch * seq
        x = x3.reshape(n, d)
        sb_k, sb_v, sgu_v, gla_s, pool_h = [], [], [], [], []
        for i in range(depth):
            m, j = i % 4, i // 4
            g1, b1 = vec(ln1_g[i]), vec(ln1_b[i])
            if m == 0:
                wq, wk, wv = (bf(sb_w_qkv[j][:, c * d:(c + 1) * d]) for c in range(3))
                q, k, v, kb, vb = _proj(
                    x, [wq, wk, wv],
                    [(0, BF16, sb_hd ** -0.5), (1, F32, 1.0), (2, F32, 1.0), (1, BF16, 1.0), (2, BF16, 1.0)],
                    "sb_qkv")
                if past is None:
                    o = _sb_prompt(q, kb, vb, batch, seq, sb_hd)
                else:
                    kp = past[0][j].reshape(batch, -1, d)
                    vp = past[1][j].reshape(batch, -1, d)
                    o = _sb_sample(q, kb, vb, kp, vp, batch, seq, sb_hd)
                sb_k.append(k.reshape(batch, seq, sb_heads, sb_hd))
                sb_v.append(v.reshape(batch, seq, sb_heads, sb_hd))
                x = _mm_res_ln(o, bf(sb_w_o[j]), x, g1, b1, alpha, "sb_out")
            elif m == 1:
                ws_mix, bs_mix = _sgu_mix_params(sgu_w_s[j], sgu_b_s[j], seq, sgu_width // sgu_groups)
                x, v = _sgu(x, bf(sgu_w_in[j]), vec(sgu_ln_g[j]), vec(sgu_ln_b[j]), ws_mix, bs_mix,
                            bf(sgu_w_out[j]), g1, b1, alpha, past is not None)
                if v is not None:
                    sgu_v.append(v.reshape(batch, seq, sgu_width))
            elif m == 2:
                w_in = gla_w_in[j]
                wg1 = jnp.zeros((d, LANES), F32).at[:, :gla_w_g1.shape[-1]].set(gla_w_g1[j])
                wg2 = jnp.zeros((LANES, dq), F32).at[:gla_w_g2.shape[1]].set(gla_w_g2[j])
                q, k, v, r, la = _gla_proj(
                    x, bf(w_in[:, :dq]), bf(w_in[:, dq:2 * dq]), bf(w_in[:, 2 * dq:2 * dq + dvt]),
                    bf(w_in[:, 2 * dq + dvt:]), bf(wg1), bf(wg2), vec(gla_b_g[j]), gla_dk ** -0.5)
                if past is None:
                    s0 = jnp.zeros((batch, gla_heads, gla_dk, gla_dv), F32)
                    seq_p = seq
                else:
                    s0 = past[2][j]
                    seq_p = -(-seq // SGU_LEN) * SGU_LEN
                    padrows = lambda a: jnp.pad(a.reshape(batch, seq, -1),
                                                ((0, 0), (0, seq_p - seq), (0, 0))).reshape(batch * seq_p, -1)
                    q, k, v, la = padrows(q), padrows(k), padrows(v), padrows(la)
                o, s = _gla_scan(q, k, v, la, s0, batch, seq_p, gla_heads)
                if seq_p != seq:
                    o = o.reshape(batch, seq_p, dvt)[:, :seq].reshape(n, dvt)
                gla_s.append(s)
                x = _gla_out(o, r, vec(gla_norm_g[j]), bf(gla_w_o[j]), x, g1, b1, alpha, gla_heads)
            else:
                if past is None:
                    hist = jnp.zeros((batch, POOL_PAD, d), F32)
                    n_valid = 0
                else:
                    hist = jnp.pad(past[3][j], ((0, 0), (POOL_PAD - n_hist, 0), (0, 0)))
                    n_valid = n_hist
                x_in = x.reshape(batch, seq, d)
                if past is None:
                    pool_h.append(x_in[:, seq - n_hist:])
                else:
                    pool_h.append(jnp.concatenate([past[3][j], x_in], axis=1)[:, -n_hist:])
                x = _pool(x, hist, bf(pool_w[j]), vec(pool_scale[j]), g1, b1, alpha, batch, seq, n_valid)
            wr_hi, wr_lo, br = _router_params(moe_w_group[i], moe_b_group[i], moe_w_router[i], moe_b_router[i])
            x = _moe(x, wr_hi, wr_lo, br, bf(moe_w_up[i]), bf(moe_w_down[i]),
                     vec(ln2_g[i]), vec(ln2_b[i]), alpha, n_grp)
        return x.reshape(batch, seq, d), sb_k, sb_v, sgu_v, gla_s, pool_h

    y_p, kp, vp, _, sp, hp = run(x_prompt, None)
    y_s, ks, vs, us, ss, hs = run(x_sample, (cache_sb_k, cache_sb_v, state_gla, state_pool))
    return (y_p, y_s, jnp.stack(kp), jnp.stack(vp), jnp.stack(ks), jnp.stack(vs), jnp.stack(us),
            jnp.stack(sp), jnp.stack(ss), jnp.stack(hp), jnp.stack(hs))
```

```python
import functools
import math

import jax
import jax.numpy as jnp
from jax import lax
from jax.experimental import pallas as pl
from jax.experimental.pallas import tpu as pltpu

F32 = jnp.float32
BF16 = jnp.bfloat16

LN_EPS = 1e-5
CHUNK = 64
SGU_LEN = 128
SB_QBLOCK = 128
SB_SPAN_BLOCKS = 3
SB_UNDERFLOW = 104.0
GLA_BLOCK = 16
GLA_TAU = 16.0
POOL_WINDOWS = (2, 4, 8, 16)
POOL_PAD = 16
MOE_EXPERTS_PER_GROUP = 4
MOE_TILE = 256
MOE_UNIT = 2048
MOE_ALIGN = 16
MOE_FFN_ROWS = 512
LANES = 128
VMEM_LIMIT = 56 * 1024 * 1024


def _cparams(*sem):
    return pltpu.CompilerParams(dimension_semantics=sem, vmem_limit_bytes=VMEM_LIMIT)


def _tile(n, pref):
    t = min(n, pref)
    while n % t:
        t -= 8
    return t


def _layer_norm(x, g, b):
    mu = jnp.mean(x, axis=-1, keepdims=True)
    xc = x - mu
    var = jnp.mean(xc * xc, axis=-1, keepdims=True)
    return xc * lax.rsqrt(var + LN_EPS) * g + b


def _split_bf16(x):
    hi = x.astype(BF16)
    lo = (x - hi.astype(F32)).astype(BF16)
    return hi, lo


def _log_sigmoid(z):
    return jnp.minimum(z, 0.0) - jnp.log(1.0 + jnp.exp(-jnp.abs(z)))


def _silu(x):
    return x / (1.0 + jnp.exp(-x))


def _proj_kernel(x_ref, *refs, outs):
    n_w = len(refs) - len(outs)
    xb = x_ref[...].astype(BF16)
    ys = [jnp.dot(xb, w_ref[...], preferred_element_type=F32) for w_ref in refs[:n_w]]
    for (wi, _, scale), o_ref in zip(outs, refs[n_w:]):
        o_ref[...] = (ys[wi] * scale).astype(o_ref.dtype)


def _proj(x, ws, outs, name):
    n, k = x.shape
    tm = _tile(n, 512)
    return pl.pallas_call(
        functools.partial(_proj_kernel, outs=tuple(outs)),
        grid=(n // tm,),
        in_specs=[pl.BlockSpec((tm, k), lambda i: (i, 0))]
        + [pl.BlockSpec(w.shape, lambda i: (0, 0)) for w in ws],
        out_specs=[pl.BlockSpec((tm, ws[wi].shape[1]), lambda i: (i, 0)) for wi, _, _ in outs],
        out_shape=[jax.ShapeDtypeStruct((n, ws[wi].shape[1]), dt) for wi, dt, _ in outs],
        compiler_params=_cparams("parallel"),
        name=name,
    )(x, *ws)


def _mm_res_ln_kernel(a_ref, w_ref, res_ref, g_ref, b_ref, o_ref, *, alpha):
    y = jnp.dot(a_ref[...].astype(BF16), w_ref[...], preferred_element_type=F32)
    o_ref[...] = _layer_norm(alpha * res_ref[...] + y, g_ref[...], b_ref[...])


def _mm_res_ln(a, w, res, g, b, alpha, name):
    n, k = a.shape
    d = w.shape[1]
    tm = _tile(n, 512)
    return pl.pallas_call(
        functools.partial(_mm_res_ln_kernel, alpha=alpha),
        grid=(n // tm,),
        in_specs=[pl.BlockSpec((tm, k), lambda i: (i, 0)),
                  pl.BlockSpec((k, d), lambda i: (0, 0)),
                  pl.BlockSpec((tm, d), lambda i: (i, 0)),
                  pl.BlockSpec((1, d), lambda i: (0, 0)),
                  pl.BlockSpec((1, d), lambda i: (0, 0))],
        out_specs=pl.BlockSpec((tm, d), lambda i: (i, 0)),
        out_shape=jax.ShapeDtypeStruct((n, d), F32),
        compiler_params=_cparams("parallel"),
        name=name,
    )(a, w, res, g, b)


def _sb_later_selector(span):
    s = jnp.arange(span)
    return (s[:, None] > s[None, :]).astype(BF16)


def _sb_stack_heads(q, head_dim):
    lane = lax.broadcasted_iota(jnp.int32, q.shape, 1)
    parts = [jnp.where((lane >= h * head_dim) & (lane < (h + 1) * head_dim), q, jnp.zeros_like(q))
             for h in range(LANES // head_dim)]
    return jnp.concatenate(parts, axis=0)


def _sb_unstack_heads(acc, tq, head_dim):
    lane = lax.broadcasted_iota(jnp.int32, (tq, LANES), 1)
    out = acc[:tq]
    for h in range(1, LANES // head_dim):
        out = jnp.where(lane >= h * head_dim, acc[h * tq:(h + 1) * tq], out)
    return out


def _sb_later_sums(lstay, u_ref, off):
    n = lstay.shape[1]
    hi, lo = _split_bf16(lstay)
    cols = []
    for c0 in range(0, n, 2 * LANES):
        c1 = min(c0 + 2 * LANES, n)
        u = u_ref[off + c0:off + n, off + c0:off + c1]
        cols.append(jnp.dot(hi[:, c0:], u, preferred_element_type=F32)
                    + jnp.dot(lo[:, c0:], u, preferred_element_type=F32))
    return cols[0] if len(cols) == 1 else jnp.concatenate(cols, axis=1)


def _sb_span(qs, k, v, valid, u_ref, u_off, acc, carry):
    z = lax.dot_general(qs, k, (((1,), (1,)), ((), ())), preferred_element_type=F32)
    ls = _log_sigmoid(z)
    lstay = ls - z
    if valid is not None:
        lstay = jnp.where(valid, lstay, 0.0)
    later = _sb_later_sums(lstay, u_ref, u_off)
    if carry is not None:
        later = later + carry
    w = jnp.exp(ls + later)
    if valid is not None:
        w = jnp.where(valid, w, 0.0)
    pv = jnp.dot(w.astype(BF16), v, preferred_element_type=F32)
    tot = jnp.sum(lstay, axis=-1, keepdims=True)
    if acc is None:
        return pv, tot
    return acc + pv, carry + tot


def _sb_prompt_kernel(q_ref, k_ref, v_ref, u_ref, o_ref, *, head_dim, span):
    qi = pl.program_id(2)
    tq = q_ref.shape[0]
    qs = _sb_stack_heads(q_ref[...], head_dim)
    rows = qs.shape[0]
    first = jnp.maximum(qi - (span // SB_QBLOCK - 1), 0)
    start = pl.multiple_of(first * SB_QBLOCK, SB_QBLOCK)
    row = lax.broadcasted_iota(jnp.int32, (rows, span), 0)
    col = lax.broadcasted_iota(jnp.int32, (rows, span), 1)
    valid = (start + col) < (qi * tq + (row & (tq - 1)))
    acc, carry = _sb_span(qs, k_ref[pl.ds(start, span), :], v_ref[pl.ds(start, span), :],
                          valid, u_ref, 0, None, None)

    def cond(state):
        j, _, carry = state
        return (j >= 0) & (jnp.max(carry) > -SB_UNDERFLOW)

    def body(state):
        j, acc, carry = state
        s = pl.multiple_of(j * SB_QBLOCK, SB_QBLOCK)
        acc, carry = _sb_span(qs, k_ref[pl.ds(s, SB_QBLOCK), :], v_ref[pl.ds(s, SB_QBLOCK), :],
                              None, u_ref, 0, acc, carry)
        return j - 1, acc, carry

    _, acc, _ = lax.while_loop(cond, body, (first - 1, acc, carry))
    o_ref[...] = _sb_unstack_heads(acc, tq, head_dim).astype(o_ref.dtype)


def _sb_prompt(q, k, v, batch, seq, head_dim):
    n, d = q.shape
    nqb = seq // SB_QBLOCK
    span = min(SB_SPAN_BLOCKS, nqb) * SB_QBLOCK
    u = _sb_later_selector(span)
    return pl.pallas_call(
        functools.partial(_sb_prompt_kernel, head_dim=head_dim, span=span),
        grid=(batch, d // LANES, nqb),
        in_specs=[pl.BlockSpec((SB_QBLOCK, LANES), lambda b, hp, qi: (b * nqb + qi, hp)),
                  pl.BlockSpec((seq, LANES), lambda b, hp, qi: (b, hp)),
                  pl.BlockSpec((seq, LANES), lambda b, hp, qi: (b, hp)),
                  pl.BlockSpec((span, span), lambda b, hp, qi: (0, 0))],
        out_specs=pl.BlockSpec((SB_QBLOCK, LANES), lambda b, hp, qi: (b * nqb + qi, hp)),
        out_shape=jax.ShapeDtypeStruct((n, d), BF16),
        compiler_params=_cparams("parallel", "parallel", "arbitrary"),
        name="sb_prompt",
    )(q, k, v, u)


def _sb_sample_kernel(q_ref, kn_ref, vn_ref, kt_ref, vt_ref, kp_hbm, vp_hbm, u_ref, o_ref,
                      kbuf, vbuf, sem, *, head_dim, past):
    b, hp = pl.program_id(0), pl.program_id(1)
    tq = q_ref.shape[0]
    tail = kt_ref.shape[0]
    qs = _sb_stack_heads(q_ref[...], head_dim)
    rows = qs.shape[0]
    pad = jnp.zeros((SB_QBLOCK - tq, LANES), BF16)
    k = jnp.concatenate([kt_ref[...].astype(BF16), kn_ref[...], pad], axis=0)
    v = jnp.concatenate([vt_ref[...].astype(BF16), vn_ref[...], pad], axis=0)
    span = tail + SB_QBLOCK
    row = lax.broadcasted_iota(jnp.int32, (rows, span), 0)
    col = lax.broadcasted_iota(jnp.int32, (rows, span), 1)
    valid = col < tail + (row & (tq - 1))
    acc, carry = _sb_span(qs, k, v, valid, u_ref, 0, None, None)

    def block_copies(j):
        src = (b, pl.ds(j * SB_QBLOCK, SB_QBLOCK), pl.ds(hp * LANES, LANES))
        return (pltpu.make_async_copy(kp_hbm.at[src], kbuf, sem.at[0]),
                pltpu.make_async_copy(vp_hbm.at[src], vbuf, sem.at[1]))

    def cond(state):
        j, _, carry = state
        return (j >= 0) & (jnp.max(carry) > -SB_UNDERFLOW)

    def body(state):
        j, acc, carry = state
        kc, vc = block_copies(j)
        kc.start()
        vc.start()
        kc.wait()
        vc.wait()
        acc, carry = _sb_span(qs, kbuf[...].astype(BF16), vbuf[...].astype(BF16),
                              None, u_ref, 0, acc, carry)
        return j - 1, acc, carry

    _, acc, _ = lax.while_loop(cond, body, ((past - tail) // SB_QBLOCK - 1, acc, carry))
    o_ref[...] = _sb_unstack_heads(acc, tq, head_dim).astype(o_ref.dtype)


def _sb_sample(q, kn, vn, kp, vp, batch, tq, head_dim):
    n, d = q.shape
    past = kp.shape[1]
    tail = min(SB_SPAN_BLOCKS - 1, past // SB_QBLOCK) * SB_QBLOCK
    u = _sb_later_selector(tail + SB_QBLOCK)
    new = pl.BlockSpec((tq, LANES), lambda b, hp: (b, hp))
    tail_spec = pl.BlockSpec((None, tail, LANES), lambda b, hp: (b, past // tail - 1, hp))
    return pl.pallas_call(
        functools.partial(_sb_sample_kernel, head_dim=head_dim, past=past),
        grid=(batch, d // LANES),
        in_specs=[new, new, new, tail_spec, tail_spec,
                  pl.BlockSpec(memory_space=pl.ANY), pl.BlockSpec(memory_space=pl.ANY),
                  pl.BlockSpec(u.shape, lambda b, hp: (0, 0))],
        out_specs=new,
        out_shape=jax.ShapeDtypeStruct((n, d), BF16),
        scratch_shapes=[pltpu.VMEM((SB_QBLOCK, LANES), F32), pltpu.VMEM((SB_QBLOCK, LANES), F32),
                        pltpu.SemaphoreType.DMA((2,))],
        compiler_params=_cparams("parallel", "parallel"),
        name="sb_sample",
    )(q, kn, vn, kp, vp, kp, vp, u)


def _gelu_tanh(x):
    c = math.sqrt(2.0 / math.pi)
    return 0.5 * x * (1.0 + jnp.tanh(c * (x + 0.044715 * (x * x * x))))


def _sgu_kernel(x_ref, win_ref, lng_ref, lnb_ref, ws_ref, bs_ref, wout_ref, g_ref, b_ref,
                o_ref, *v_out, alpha, width, groups):
    x = x_ref[...]
    z = _gelu_tanh(jnp.dot(x.astype(BF16), win_ref[...], preferred_element_type=F32))
    u = z[:, :width]
    v = _layer_norm(z[:, width:], lng_ref[...], lnb_ref[...])
    if v_out:
        v_out[0][...] = v
    tm = x.shape[0]
    gd = width // groups
    vb = v.astype(BF16)
    rows = []
    for c in range(tm // SGU_LEN):
        r0 = c * SGU_LEN
        cols = []
        for g in range(groups):
            mixed = jnp.dot(ws_ref[g], vb[r0:r0 + SGU_LEN, g * gd:(g + 1) * gd],
                            preferred_element_type=F32) + bs_ref[g]
            cols.append(mixed)
        rows.append(jnp.concatenate(cols, axis=1))
    mixed = rows[0] if len(rows) == 1 else jnp.concatenate(rows, axis=0)
    y = jnp.dot((u * mixed).astype(BF16), wout_ref[...], preferred_element_type=F32)
    o_ref[...] = _layer_norm(alpha * x + y, g_ref[...], b_ref[...])


def _sgu(x, w_in, ln_g, ln_b, ws_mix, bs_mix, w_out, g, b, alpha, want_v):
    n, d = x.shape
    width = w_out.shape[0]
    groups = ws_mix.shape[0]
    tm = _tile(n, 256)
    const2 = lambda i: (0, 0)
    const3 = lambda i: (0, 0, 0)
    out_specs = [pl.BlockSpec((tm, d), lambda i: (i, 0))]
    out_shape = [jax.ShapeDtypeStruct((n, d), F32)]
    if want_v:
        out_specs.append(pl.BlockSpec((tm, width), lambda i: (i, 0)))
        out_shape.append(jax.ShapeDtypeStruct((n, width), F32))
    res = pl.pallas_call(
        functools.partial(_sgu_kernel, alpha=alpha, width=width, groups=groups),
        grid=(n // tm,),
        in_specs=[pl.BlockSpec((tm, d), lambda i: (i, 0)),
                  pl.BlockSpec(w_in.shape, const2),
                  pl.BlockSpec((1, width), const2),
                  pl.BlockSpec((1, width), const2),
                  pl.BlockSpec(ws_mix.shape, const3),
                  pl.BlockSpec(bs_mix.shape, const3),
                  pl.BlockSpec(w_out.shape, const2),
                  pl.BlockSpec((1, d), const2),
                  pl.BlockSpec((1, d), const2)],
        out_specs=out_specs,
        out_shape=out_shape,
        compiler_params=_cparams("parallel"),
        name="sgu",
    )(x, w_in, ln_g, ln_b, ws_mix, bs_mix, w_out, g, b)
    return res[0], (res[1] if want_v else None)


def _gla_proj_kernel(x_ref, wq_ref, wk_ref, wv_ref, wr_ref, wg1_ref, wg2_ref, bg_ref,
                     q_ref, k_ref, v_ref, r_ref, la_ref, *, qscale):
    xb = x_ref[...].astype(BF16)
    q_ref[...] = jnp.dot(xb, wq_ref[...], preferred_element_type=F32) * qscale
    k_ref[...] = jnp.dot(xb, wk_ref[...], preferred_element_type=F32)
    v_ref[...] = jnp.dot(xb, wv_ref[...], preferred_element_type=F32).astype(BF16)
    r_ref[...] = jnp.dot(xb, wr_ref[...], preferred_element_type=F32)
    t = jnp.dot(xb, wg1_ref[...], preferred_element_type=F32)
    gate = jnp.dot(t.astype(BF16), wg2_ref[...], preferred_element_type=F32) + bg_ref[...]
    la_ref[...] = _log_sigmoid(gate) / GLA_TAU


def _gla_proj(x, wq, wk, wv, wr, wg1, wg2, bg, qscale):
    n, d = x.shape
    dq, dv = wq.shape[1], wv.shape[1]
    tm = _tile(n, 512)
    const = lambda i: (0, 0)
    row = lambda i: (i, 0)
    return pl.pallas_call(
        functools.partial(_gla_proj_kernel, qscale=qscale),
        grid=(n // tm,),
        in_specs=[pl.BlockSpec((tm, d), row)]
        + [pl.BlockSpec(w.shape, const) for w in (wq, wk, wv, wr, wg1, wg2, bg)],
        out_specs=[pl.BlockSpec((tm, dq), row), pl.BlockSpec((tm, dq), row),
                   pl.BlockSpec((tm, dv), row), pl.BlockSpec((tm, dv), row),
                   pl.BlockSpec((tm, dq), row)],
        out_shape=[jax.ShapeDtypeStruct((n, dq), F32), jax.ShapeDtypeStruct((n, dq), F32),
                   jax.ShapeDtypeStruct((n, dv), BF16), jax.ShapeDtypeStruct((n, dv), F32),
                   jax.ShapeDtypeStruct((n, dq), F32)],
        compiler_params=_cparams("parallel"),
        name="gla_proj",
    )(x, wq, wk, wv, wr, wg1, wg2, bg)


def _gla_scan_kernel(q_ref, k_ref, v_ref, la_ref, s0_ref, o_ref, sfin_ref, st_ref,
                     *, heads, dk, dv):
    ti = pl.program_id(1)
    tt = q_ref.shape[0]
    nsb = tt // GLA_BLOCK

    @pl.when(ti == 0)
    def _():
        for h in range(heads):
            st_ref[h] = s0_ref[0, h].T

    row = lax.broadcasted_iota(jnp.int32, (tt, tt), 0)
    col = lax.broadcasted_iota(jnp.int32, (tt, tt), 1)
    same = (row // GLA_BLOCK) == (col // GLA_BLOCK)
    tril = same & (col <= row)
    bdcat = jnp.concatenate([jnp.where(tril, 1.0, 0.0), jnp.where(same, 1.0, 0.0)],
                            axis=0).astype(BF16)
    rsub = lax.broadcasted_iota(jnp.int32, (tt, dk), 0) // GLA_BLOCK

    for h in range(heads):
        ks = slice(h * dk, (h + 1) * dk)
        vs = slice(h * dv, (h + 1) * dv)
        hi, lo = _split_bf16(la_ref[:, ks])
        ct = (jnp.dot(bdcat, hi, preferred_element_type=F32)
              + jnp.dot(bdcat, lo, preferred_element_type=F32))
        cum, tot = ct[:tt], ct[tt:]
        k = k_ref[:, ks]
        qd = (q_ref[:, ks] * jnp.exp(cum)).astype(BF16)
        kd = (k * jnp.exp(-cum)).astype(BF16)
        kt = k * jnp.exp(tot - cum)
        sc = lax.dot_general(qd, kd, (((1,), (1,)), ((), ())), preferred_element_type=F32)
        sc = jnp.where(tril, sc, 0.0).astype(BF16)
        vh = v_ref[:, vs]
        o_intra = jnp.dot(sc, vh, preferred_element_type=F32)
        ktcat = jnp.concatenate(
            [jnp.where(rsub == i, kt, 0.0).astype(BF16) for i in range(nsb)], axis=1)
        ut = lax.dot_general(vh, ktcat, (((0,), (0,)), ((), ())), preferred_element_type=F32)
        st = st_ref[h]
        parts = []
        for i in range(nsb):
            r0 = i * GLA_BLOCK
            parts.append(lax.dot_general(qd[r0:r0 + GLA_BLOCK], st.astype(BF16),
                                         (((1,), (1,)), ((), ())), preferred_element_type=F32))
            st = st * jnp.exp(tot[r0:r0 + 1, :]) + ut[:, i * dk:(i + 1) * dk]
        st_ref[h] = st
        o_ref[:, vs] = o_intra + jnp.concatenate(parts, axis=0)

    @pl.when(ti == pl.num_programs(1) - 1)
    def _():
        for h in range(heads):
            sfin_ref[0, h] = st_ref[h].T


def _gla_scan(q, k, v, la, s0, batch, seq, heads):
    n, dq = q.shape
    dvt = v.shape[1]
    dk, dv = dq // heads, dvt // heads
    tt = SGU_LEN
    nt = seq // tt
    row = lambda b, t: (b * nt + t, 0)
    st = lambda b, t: (b, 0, 0, 0)
    return pl.pallas_call(
        functools.partial(_gla_scan_kernel, heads=heads, dk=dk, dv=dv),
        grid=(batch, nt),
        in_specs=[pl.BlockSpec((tt, dq), row), pl.BlockSpec((tt, dq), row),
                  pl.BlockSpec((tt, dvt), row), pl.BlockSpec((tt, dq), row),
                  pl.BlockSpec((1, heads, dk, dv), st)],
        out_specs=[pl.BlockSpec((tt, dvt), row), pl.BlockSpec((1, heads, dk, dv), st)],
        out_shape=[jax.ShapeDtypeStruct((n, dvt), F32),
                   jax.ShapeDtypeStruct((batch, heads, dk, dv), F32)],
        scratch_shapes=[pltpu.VMEM((heads, dv, dk), F32)],
        compiler_params=_cparams("parallel", "arbitrary"),
        name="gla_scan",
    )(q, k, v, la, s0)


def _gla_out_kernel(o_ref, r_ref, ng_ref, wo_ref, x_ref, g_ref, b_ref, y_ref, *, alpha, heads):
    o = o_ref[...]
    dv = o.shape[1] // heads
    normed = []
    for h in range(heads):
        oh = o[:, h * dv:(h + 1) * dv]
        ms = jnp.mean(oh * oh, axis=-1, keepdims=True)
        normed.append(oh * lax.rsqrt(ms + LN_EPS) * ng_ref[...])
    gated = jnp.concatenate(normed, axis=1) * _silu(r_ref[...])
    y = jnp.dot(gated.astype(BF16), wo_ref[...], preferred_element_type=F32)
    y_ref[...] = _layer_norm(alpha * x_ref[...] + y, g_ref[...], b_ref[...])


def _gla_out(o, r, norm_g, w_o, x, g, b, alpha, heads):
    n, dvt = o.shape
    d = x.shape[1]
    tm = _tile(n, 512)
    row = lambda i: (i, 0)
    const = lambda i: (0, 0)
    return pl.pallas_call(
        functools.partial(_gla_out_kernel, alpha=alpha, heads=heads),
        grid=(n // tm,),
        in_specs=[pl.BlockSpec((tm, dvt), row), pl.BlockSpec((tm, dvt), row),
                  pl.BlockSpec(norm_g.shape, const), pl.BlockSpec(w_o.shape, const),
                  pl.BlockSpec((tm, d), row), pl.BlockSpec((1, d), const),
                  pl.BlockSpec((1, d), const)],
        out_specs=pl.BlockSpec((tm, d), row),
        out_shape=jax.ShapeDtypeStruct((n, d), F32),
        compiler_params=_cparams("parallel"),
        name="gla_out",
    )(o, r, norm_g, w_o, x, g, b)


def _pool_kernel(x_ref, prev_ref, hist_ref, w_ref, sc_ref, g_ref, b_ref, o_ref, buf_ref,
                 *, alpha, n_valid):
    ti = pl.program_id(1)
    tm, d = x_ref.shape
    x = x_ref[...]

    @pl.when(ti == 0)
    def _():
        buf_ref[0:POOL_PAD, :] = hist_ref[0]

    @pl.when(ti > 0)
    def _():
        buf_ref[0:POOL_PAD, :] = prev_ref[...]

    buf_ref[POOL_PAD:POOL_PAD + tm, :] = x
    gd = d // len(POOL_WINDOWS)
    pos = ti * tm + lax.broadcasted_iota(jnp.int32, (tm, 1), 0)
    ys = []
    for g, win in enumerate(POOL_WINDOWS):
        cs = slice(g * gd, (g + 1) * gd)
        acc = x[:, cs]
        for j in range(1, win):
            acc = acc + buf_ref[POOL_PAD - j:POOL_PAD - j + tm, cs]
        count = jnp.minimum(pos + 1 + n_valid, win).astype(F32)
        dlt = acc / count - x[:, cs]
        ys.append(jnp.dot(dlt.astype(BF16), w_ref[g], preferred_element_type=F32))
    y = jnp.concatenate(ys, axis=1) * sc_ref[...]
    o_ref[...] = _layer_norm(alpha * x + y, g_ref[...], b_ref[...])


def _pool(x, hist, w_pool, scale, g, b, alpha, batch, seq, n_valid):
    n, d = x.shape
    tm = _tile(seq, 256)
    nt = seq // tm
    ratio = tm // POOL_PAD
    const2 = lambda bb, t: (0, 0)
    row = lambda bb, t: (bb * nt + t, 0)
    return pl.pallas_call(
        functools.partial(_pool_kernel, alpha=alpha, n_valid=n_valid),
        grid=(batch, nt),
        in_specs=[pl.BlockSpec((tm, d), row),
                  pl.BlockSpec((POOL_PAD, d),
                               lambda bb, t: (jnp.maximum((bb * nt + t) * ratio - 1, 0), 0)),
                  pl.BlockSpec((1, POOL_PAD, d), lambda bb, t: (bb, 0, 0)),
                  pl.BlockSpec(w_pool.shape, lambda bb, t: (0, 0, 0)),
                  pl.BlockSpec((1, d), const2), pl.BlockSpec((1, d), const2),
                  pl.BlockSpec((1, d), const2)],
        out_specs=pl.BlockSpec((tm, d), row),
        out_shape=jax.ShapeDtypeStruct((n, d), F32),
        scratch_shapes=[pltpu.VMEM((POOL_PAD + tm, d), F32)],
        compiler_params=_cparams("parallel", "arbitrary"),
        name="pool",
    )(x, x, hist, w_pool, scale, g, b)


def _moe_route_math(x, wr_hi, wr_lo, br, n_exp, n_grp):
    hi, lo = _split_bf16(x)
    logit = (jnp.dot(hi, wr_hi, preferred_element_type=F32)
             + jnp.dot(lo, wr_hi, preferred_element_type=F32)
             + jnp.dot(hi, wr_lo, preferred_element_type=F32)) + br
    lane_i = lax.broadcasted_iota(jnp.int32, logit.shape, 1)
    lane = lane_i.astype(F32)
    neg = jnp.float32(-jnp.inf)
    big = jnp.float32(LANES)
    is_grp = (lane_i >= n_exp) & (lane_i < n_exp + n_grp)
    gl = jnp.where(is_grp, logit, neg)
    gmax = jnp.max(gl, axis=-1, keepdims=True)
    g_sel = jnp.min(jnp.where(gl == gmax, lane, big), axis=-1, keepdims=True) - n_exp
    g_prob = 1.0 / jnp.sum(jnp.where(is_grp, jnp.exp(gl - gmax), 0.0), axis=-1, keepdims=True)
    lane_grp = (lane_i // MOE_EXPERTS_PER_GROUP).astype(F32)
    in_grp = (lane_i < n_exp) & (lane_grp == g_sel)
    el = jnp.where(in_grp, logit, neg)
    m1 = jnp.max(el, axis=-1, keepdims=True)
    i1 = jnp.min(jnp.where(el == m1, lane, big), axis=-1, keepdims=True)
    el2 = jnp.where(lane == i1, neg, el)
    m2 = jnp.max(el2, axis=-1, keepdims=True)
    i2 = jnp.min(jnp.where(el2 == m2, lane, big), axis=-1, keepdims=True)
    e2 = jnp.exp(m2 - m1)
    w1 = g_prob / (1.0 + e2)
    w2 = g_prob * e2 / (1.0 + e2)
    return i1, i2, w1, w2


_META_E1, _META_E2, _META_R1, _META_R2, _META_W1, _META_W2 = range(6)


def _moe_route_kernel(x_ref, wrh_ref, wrl_ref, br_ref, meta_ref, cnt_ref, *, n_exp, n_grp):
    i1, i2, w1, w2 = _moe_route_math(x_ref[...], wrh_ref[...], wrl_ref[...], br_ref[...], n_exp, n_grp)
    tm = x_ref.shape[0]
    lane = lax.broadcasted_iota(jnp.int32, (tm, LANES), 1).astype(F32)
    m1 = jnp.where(lane == i1, 1.0, 0.0)
    m2 = jnp.where(lane == i2, 1.0, 0.0)
    both = m1 + m2
    r = lax.broadcasted_iota(jnp.int32, (tm, tm), 0)
    c = lax.broadcasted_iota(jnp.int32, (tm, tm), 1)
    before = jnp.where(c < r, 1.0, 0.0).astype(BF16)
    rank = jnp.dot(before, both.astype(BF16), preferred_element_type=F32)
    r1 = jnp.sum(m1 * rank, axis=-1, keepdims=True)
    r2 = jnp.sum(m2 * rank, axis=-1, keepdims=True)
    meta = jnp.zeros((tm, LANES), F32)
    for idx, col in ((_META_E1, i1), (_META_E2, i2), (_META_R1, r1), (_META_R2, r2),
                     (_META_W1, w1), (_META_W2, w2)):
        meta = jnp.where(lane == idx, col, meta)
    meta_ref[...] = meta
    cnt_ref[0] = jnp.broadcast_to(jnp.sum(both, axis=0, keepdims=True), cnt_ref.shape[1:])


def _moe_route(x, wr_hi, wr_lo, br, n_exp, n_grp):
    n, d = x.shape
    tm = MOE_TILE
    const = lambda i: (0, 0)
    return pl.pallas_call(
        functools.partial(_moe_route_kernel, n_exp=n_exp, n_grp=n_grp),
        grid=(n // tm,),
        in_specs=[pl.BlockSpec((tm, d), lambda i: (i, 0)),
                  pl.BlockSpec(wr_hi.shape, const), pl.BlockSpec(wr_lo.shape, const),
                  pl.BlockSpec((1, LANES), const)],
        out_specs=[pl.BlockSpec((tm, LANES), lambda i: (i, 0)),
                   pl.BlockSpec((1, 8, LANES), lambda i: (i, 0, 0))],
        out_shape=[jax.ShapeDtypeStruct((n, LANES), F32),
                   jax.ShapeDtypeStruct((n // tm, 8, LANES), F32)],
        compiler_params=_cparams("parallel"),
        name="moe_route",
    )(x, wr_hi, wr_lo, br)


def _moe_placement(meta, base_row, n_stage, weighted):
    tm = meta.shape[0]
    lane = lax.broadcasted_iota(jnp.int32, (tm, LANES), 1).astype(F32)
    col = lax.broadcasted_iota(jnp.int32, (tm, n_stage), 1).astype(F32)
    out = None
    for e_idx, r_idx, w_idx in ((_META_E1, _META_R1, _META_W1), (_META_E2, _META_R2, _META_W2)):
        e = meta[:, e_idx:e_idx + 1]
        pos = jnp.sum(jnp.where(lane == e, base_row, 0.0), axis=-1, keepdims=True) + meta[:, r_idx:r_idx + 1]
        val = meta[:, w_idx:w_idx + 1] if weighted else 1.0
        term = jnp.where(col == pos, val, 0.0)
        out = term if out is None else out + term
    return out.astype(BF16)


def _moe_segment_copies(tile, n_exp, seg_start, seg_len, stage_base, copy_rows):
    for e in range(n_exp):
        idx = tile * n_exp + e
        src, dst, n16 = stage_base[idx], seg_start[idx], seg_len[idx] // MOE_ALIGN

        def piece(c, carry, src=src, dst=dst):
            copy_rows(pl.multiple_of(dst + c * MOE_ALIGN, MOE_ALIGN),
                      pl.multiple_of(src + c * MOE_ALIGN, MOE_ALIGN))
            return carry

        lax.fori_loop(0, n16, piece, 0)


def _moe_sort_kernel(seg_start, seg_len, stage_base, x_ref, meta_ref, base_ref, xs_ref, stg_ref,
                     *, n_exp, tiles_per_unit):
    u, t = pl.program_id(0), pl.program_id(1)

    @pl.when(t == 0)
    def _():
        xs_ref[...] = jnp.zeros_like(xs_ref)

    n_stage = stg_ref.shape[0]
    place = _moe_placement(meta_ref[...], base_ref[0, 0:1, :], n_stage, weighted=False)
    xb = x_ref[...].astype(BF16)
    for c0 in range(0, n_stage, 2 * LANES):
        rows = lax.dot_general(place[:, c0:c0 + 2 * LANES], xb, (((0,), (0,)), ((), ())),
                               preferred_element_type=F32)
        stg_ref[c0:c0 + 2 * LANES, :] = rows.astype(BF16)

    def copy_rows(unit_row, stage_row):
        xs_ref[pl.ds(unit_row, MOE_ALIGN), :] = stg_ref[pl.ds(stage_row, MOE_ALIGN), :]

    _moe_segment_copies(u * tiles_per_unit + t, n_exp, seg_start, seg_len, stage_base, copy_rows)


def _moe_ffn_kernel(off, xs_ref, wup_ref, wdn_ref, ys_ref, *, n_exp, hidden):
    u, e = pl.program_id(0), pl.program_id(1)
    start = off[u * (n_exp + 1) + e]
    n = off[u * (n_exp + 1) + e + 1] - start

    def ffn_rows(s, size):
        s = pl.multiple_of(s, MOE_ALIGN)
        h = jnp.dot(xs_ref[pl.ds(s, size), :], wup_ref[0], preferred_element_type=F32)
        act = _silu(h[:, :hidden]) * h[:, hidden:]
        y = jnp.dot(act.astype(BF16), wdn_ref[0], preferred_element_type=F32)
        ys_ref[pl.ds(s, size), :] = y.astype(ys_ref.dtype)

    n_full = n // MOE_FFN_ROWS

    def full(c, carry):
        ffn_rows(start + c * MOE_FFN_ROWS, MOE_FFN_ROWS)
        return carry

    lax.fori_loop(0, n_full, full, 0)
    rem = n - n_full * MOE_FFN_ROWS
    tail = start + n_full * MOE_FFN_ROWS
    for size in range(LANES, MOE_FFN_ROWS + 1, LANES):
        @pl.when((rem > size - LANES) & (rem <= size))
        def _(size=size):
            ffn_rows(tail, size)


def _moe_combine_kernel(seg_start, seg_len, stage_base, ys_ref, x_ref, meta_ref, base_ref, g_ref, b_ref,
                        o_ref, stg_ref, *, alpha, n_exp, tiles_per_unit):
    u, t = pl.program_id(0), pl.program_id(1)
    stg_ref[...] = jnp.zeros_like(stg_ref)

    def copy_rows(unit_row, stage_row):
        stg_ref[pl.ds(stage_row, MOE_ALIGN), :] = ys_ref[pl.ds(unit_row, MOE_ALIGN), :]

    _moe_segment_copies(u * tiles_per_unit + t, n_exp, seg_start, seg_len, stage_base, copy_rows)
    place = _moe_placement(meta_ref[...], base_ref[0, 0:1, :], stg_ref.shape[0], weighted=True)
    f = jnp.dot(place, stg_ref[...], preferred_element_type=F32)
    o_ref[...] = _layer_norm(alpha * x_ref[...] + f, g_ref[...], b_ref[...])


def _moe(x, wr_hi, wr_lo, br, w_up, w_dn, g, b, alpha, n_grp):
    n, d = x.shape
    n_exp, _, two_f = w_up.shape
    hidden = two_f // 2
    tm = MOE_TILE
    unit = min(MOE_UNIT, n)
    tpu = unit // tm
    n_tiles, n_units = n // tm, n // unit
    worst_pad = MOE_ALIGN - 1
    n_stage = -(-(2 * tm + n_exp * worst_pad) // (2 * LANES)) * (2 * LANES)
    cap = -(-(2 * unit + n_exp * tpu * worst_pad) // MOE_FFN_ROWS) * MOE_FFN_ROWS + MOE_FFN_ROWS

    meta, cnt = _moe_route(x, wr_hi, wr_lo, br, n_exp, n_grp)
    cnt = cnt[:, 0, :n_exp].astype(jnp.int32)
    seg_len = (cnt + worst_pad) // MOE_ALIGN * MOE_ALIGN
    stage_base = jnp.cumsum(seg_len, axis=1) - seg_len
    sl = seg_len.reshape(n_units, tpu, n_exp)
    unit_len = jnp.sum(sl, axis=1)
    off = jnp.concatenate([jnp.zeros((n_units, 1), jnp.int32), jnp.cumsum(unit_len, axis=1)], axis=1)
    seg_start = off[:, None, :n_exp] + jnp.cumsum(sl, axis=1) - sl
    base_rows = jnp.zeros((n_tiles, 8, LANES), F32).at[:, 0, :n_exp].set(stage_base.astype(F32))
    tables = (seg_start.reshape(-1), seg_len.reshape(-1), stage_base.reshape(-1))

    tile_row = lambda u, t, *_: (u * tpu + t, 0)
    tile_row3 = lambda u, t, *_: (u * tpu + t, 0, 0)
    unit_row = lambda u, t, *_: (u, 0)
    const = lambda u, t, *_: (0, 0)

    xs = pl.pallas_call(
        functools.partial(_moe_sort_kernel, n_exp=n_exp, tiles_per_unit=tpu),
        grid_spec=pltpu.PrefetchScalarGridSpec(
            num_scalar_prefetch=3, grid=(n_units, tpu),
            in_specs=[pl.BlockSpec((tm, d), tile_row), pl.BlockSpec((tm, LANES), tile_row),
                      pl.BlockSpec((1, 8, LANES), tile_row3)],
            out_specs=pl.BlockSpec((cap, d), unit_row),
            scratch_shapes=[pltpu.VMEM((n_stage, d), BF16)]),
        out_shape=jax.ShapeDtypeStruct((n_units * cap, d), BF16),
        compiler_params=_cparams("parallel", "arbitrary"),
        name="moe_sort",
    )(*tables, x, meta, base_rows)

    ys = pl.pallas_call(
        functools.partial(_moe_ffn_kernel, n_exp=n_exp, hidden=hidden),
        grid_spec=pltpu.PrefetchScalarGridSpec(
            num_scalar_prefetch=1, grid=(n_units, n_exp),
            in_specs=[pl.BlockSpec((cap, d), unit_row, pipeline_mode=pl.Buffered(1)),
                      pl.BlockSpec((1, d, two_f), lambda u, e, *_: (e, 0, 0)),
                      pl.BlockSpec((1, hidden, d), lambda u, e, *_: (e, 0, 0))],
            out_specs=pl.BlockSpec((cap, d), unit_row)),
        out_shape=jax.ShapeDtypeStruct((n_units * cap, d), BF16),
        compiler_params=_cparams("parallel", "arbitrary"),
        name="moe_ffn",
    )(off.reshape(-1), xs, w_up, w_dn)

    return pl.pallas_call(
        functools.partial(_moe_combine_kernel, alpha=alpha, n_exp=n_exp, tiles_per_unit=tpu),
        grid_spec=pltpu.PrefetchScalarGridSpec(
            num_scalar_prefetch=3, grid=(n_units, tpu),
            in_specs=[pl.BlockSpec((cap, d), unit_row), pl.BlockSpec((tm, d), tile_row),
                      pl.BlockSpec((tm, LANES), tile_row), pl.BlockSpec((1, 8, LANES), tile_row3),
                      pl.BlockSpec((1, d), const), pl.BlockSpec((1, d), const)],
            out_specs=pl.BlockSpec((tm, d), tile_row),
            scratch_shapes=[pltpu.VMEM((n_stage, d), BF16)]),
        out_shape=jax.ShapeDtypeStruct((n, d), F32),
        compiler_params=_cparams("parallel", "arbitrary"),
        name="moe_combine",
    )(*tables, ys, x, meta, base_rows, g, b)


def _router_params(w_group, b_group, w_router, b_router):
    d, n_grp = w_group.shape
    n_exp = w_router.shape[1]
    w = jnp.zeros((d, LANES), F32).at[:, :n_exp].set(w_router).at[:, n_exp:n_exp + n_grp].set(w_group)
    hi = w.astype(BF16)
    lo = (w - hi.astype(F32)).astype(BF16)
    br = jnp.zeros((1, LANES), F32).at[0, :n_exp].set(b_router).at[0, n_exp:n_exp + n_grp].set(b_group)
    return hi, lo, br


def _sgu_mix_params(w_s, b_s, t_len, gd):
    groups = w_s.shape[0]
    L = min(t_len, SGU_LEN)
    pos = jnp.arange(L)
    allowed = (pos[None, :] // CHUNK) <= (pos[:, None] // CHUNK)
    w = jnp.where(allowed[None], w_s[:, :L, :L], 0.0)
    rep = SGU_LEN // L
    if rep > 1:
        eye = jnp.eye(rep, dtype=F32)
        w = jnp.einsum("ab,gts->gatbs", eye, w).reshape(groups, SGU_LEN, SGU_LEN)
    bias = jnp.tile(b_s[:, :L], (1, rep))
    bias = jnp.broadcast_to(bias[:, :, None], (groups, SGU_LEN, gd))
    return w.astype(BF16), bias.astype(F32)


def kernel(x_prompt, x_sample, cache_sb_k, cache_sb_v, state_gla, state_pool, sb_w_qkv, sb_w_o, sgu_w_in, sgu_ln_g, sgu_ln_b, sgu_w_s, sgu_b_s, sgu_w_out, gla_w_in, gla_w_g1, gla_w_g2, gla_b_g, gla_norm_g, gla_w_o, pool_w, pool_scale, ln1_g, ln1_b, ln2_g, ln2_b, moe_w_group, moe_b_group, moe_w_router, moe_b_router, moe_w_up, moe_w_down):
    depth = ln1_g.shape[0]
    d = x_prompt.shape[-1]
    alpha = (2.0 * depth) ** 0.25
    sb_heads, sb_hd = cache_sb_k.shape[3], cache_sb_k.shape[4]
    gla_heads, gla_dk, gla_dv = state_gla.shape[2], state_gla.shape[3], state_gla.shape[4]
    n_grp = moe_w_group.shape[-1]
    dq, dvt = gla_heads * gla_dk, gla_heads * gla_dv
    n_hist = state_pool.shape[2]
    sgu_groups = sgu_w_s.shape[1]
    sgu_width = sgu_w_out.shape[1]

    bf = lambda a: a.astype(BF16)
    vec = lambda a: a.reshape(1, -1)

    def run(x3, past):
        batch, seq, _ = x3.shape
        n = batch * seq
        x = x3.reshape(n, d)
        sb_k, sb_v, sgu_v, gla_s, pool_h = [], [], [], [], []
        for i in range(depth):
            m, j = i % 4, i // 4
            g1, b1 = vec(ln1_g[i]), vec(ln1_b[i])
            if m == 0:
                wq, wk, wv = (bf(sb_w_qkv[j][:, c * d:(c + 1) * d]) for c in range(3))
                q, k, v, kb, vb = _proj(
                    x, [wq, wk, wv],
                    [(0, BF16, sb_hd ** -0.5), (1, F32, 1.0), (2, F32, 1.0), (1, BF16, 1.0), (2, BF16, 1.0)],
                    "sb_qkv")
                if past is None:
                    o = _sb_prompt(q, kb, vb, batch, seq, sb_hd)
                else:
                    kp = past[0][j].reshape(batch, -1, d)
                    vp = past[1][j].reshape(batch, -1, d)
                    o = _sb_sample(q, kb, vb, kp, vp, batch, seq, sb_hd)
                sb_k.append(k.reshape(batch, seq, sb_heads, sb_hd))
                sb_v.append(v.reshape(batch, seq, sb_heads, sb_hd))
                x = _mm_res_ln(o, bf(sb_w_o[j]), x, g1, b1, alpha, "sb_out")
            elif m == 1:
                ws_mix, bs_mix = _sgu_mix_params(sgu_w_s[j], sgu_b_s[j], seq, sgu_width // sgu_groups)
                x, v = _sgu(x, bf(sgu_w_in[j]), vec(sgu_ln_g[j]), vec(sgu_ln_b[j]), ws_mix, bs_mix,
                            bf(sgu_w_out[j]), g1, b1, alpha, past is not None)
                if v is not None:
                    sgu_v.append(v.reshape(batch, seq, sgu_width))
            elif m == 2:
                w_in = gla_w_in[j]
                wg1 = jnp.zeros((d, LANES), F32).at[:, :gla_w_g1.shape[-1]].set(gla_w_g1[j])
                wg2 = jnp.zeros((LANES, dq), F32).at[:gla_w_g2.shape[1]].set(gla_w_g2[j])
                q, k, v, r, la = _gla_proj(
                    x, bf(w_in[:, :dq]), bf(w_in[:, dq:2 * dq]), bf(w_in[:, 2 * dq:2 * dq + dvt]),
                    bf(w_in[:, 2 * dq + dvt:]), bf(wg1), bf(wg2), vec(gla_b_g[j]), gla_dk ** -0.5)
                if past is None:
                    s0 = jnp.zeros((batch, gla_heads, gla_dk, gla_dv), F32)
                    seq_p = seq
                else:
                    s0 = past[2][j]
                    seq_p = -(-seq // SGU_LEN) * SGU_LEN
                    padrows = lambda a: jnp.pad(a.reshape(batch, seq, -1),
                                                ((0, 0), (0, seq_p - seq), (0, 0))).reshape(batch * seq_p, -1)
                    q, k, v, la = padrows(q), padrows(k), padrows(v), padrows(la)
                o, s = _gla_scan(q, k, v, la, s0, batch, seq_p, gla_heads)
                if seq_p != seq:
                    o = o.reshape(batch, seq_p, dvt)[:, :seq].reshape(n, dvt)
                gla_s.append(s)
                x = _gla_out(o, r, vec(gla_norm_g[j]), bf(gla_w_o[j]), x, g1, b1, alpha, gla_heads)
            else:
                if past is None:
                    hist = jnp.zeros((batch, POOL_PAD, d), F32)
                    n_valid = 0
                else:
                    hist = jnp.pad(past[3][j], ((0, 0), (POOL_PAD - n_hist, 0), (0, 0)))
                    n_valid = n_hist
                x_in = x.reshape(batch, seq, d)
                if past is None:
                    pool_h.append(x_in[:, seq - n_hist:])
                else:
                    pool_h.append(jnp.concatenate([past[3][j], x_in], axis=1)[:, -n_hist:])
                x = _pool(x, hist, bf(pool_w[j]), vec(pool_scale[j]), g1, b1, alpha, batch, seq, n_valid)
            wr_hi, wr_lo, br = _router_params(moe_w_group[i], moe_b_group[i], moe_w_router[i], moe_b_router[i])
            x = _moe(x, wr_hi, wr_lo, br, bf(moe_w_up[i]), bf(moe_w_down[i]),
                     vec(ln2_g[i]), vec(ln2_b[i]), alpha, n_grp)
        return x.reshape(batch, seq, d), sb_k, sb_v, sgu_v, gla_s, pool_h

    y_p, kp, vp, _, sp, hp = run(x_prompt, None)
    y_s, ks, vs, us, ss, hs = run(x_sample, (cache_sb_k, cache_sb_v, state_gla, state_pool))
    return (y_p, y_s, jnp.stack(kp), jnp.stack(vp), jnp.stack(ks), jnp.stack(vs), jnp.stack(us),
            jnp.stack(sp), jnp.stack(ss), jnp.stack(hp), jnp.stack(hs))
```

```python
import functools
import math

import jax
import jax.numpy as jnp
from jax import lax
from jax.experimental import pallas as pl
from jax.experimental.pallas import tpu as pltpu

F32 = jnp.float32
BF16 = jnp.bfloat16

LN_EPS = 1e-5
CHUNK = 64
SGU_LEN = 128
SB_QBLOCK = 128
SB_SPAN_BLOCKS = 3
SB_LANE_GROUPS = 2
SB_UNDERFLOW = 104.0
GLA_BLOCK = 16
GLA_TAU = 16.0
POOL_WINDOWS = (2, 4, 8, 16)
POOL_PAD = 16
MOE_EXPERTS_PER_GROUP = 4
MOE_TILE = 256
MOE_UNIT = 4096
MOE_ALIGN = 16
MOE_FFN_ROWS = 512
LANES = 128
VMEM_LIMIT = 56 * 1024 * 1024


def _cparams(*sem):
    return pltpu.CompilerParams(dimension_semantics=sem, vmem_limit_bytes=VMEM_LIMIT)


def _tile(n, pref):
    t = min(n, pref)
    while n % t:
        t -= 8
    return t


def _layer_norm(x, g, b):
    mu = jnp.mean(x, axis=-1, keepdims=True)
    xc = x - mu
    var = jnp.mean(xc * xc, axis=-1, keepdims=True)
    return xc * lax.rsqrt(var + LN_EPS) * g + b


def _split_bf16(x):
    hi = x.astype(BF16)
    lo = (x - hi.astype(F32)).astype(BF16)
    return hi, lo


def _log_sigmoid(z):
    return jnp.minimum(z, 0.0) - jnp.log(1.0 + jnp.exp(-jnp.abs(z)))


def _silu(x):
    return x / (1.0 + jnp.exp(-x))


def _proj_kernel(x_ref, *refs, outs):
    n_w = len(refs) - len(outs)
    xb = x_ref[...].astype(BF16)
    ys = [jnp.dot(xb, w_ref[...], preferred_element_type=F32) for w_ref in refs[:n_w]]
    for (wi, _, scale), o_ref in zip(outs, refs[n_w:]):
        o_ref[...] = (ys[wi] * scale).astype(o_ref.dtype)


def _proj(x, ws, outs, name):
    n, k = x.shape
    tm = _tile(n, 512)
    return pl.pallas_call(
        functools.partial(_proj_kernel, outs=tuple(outs)),
        grid=(n // tm,),
        in_specs=[pl.BlockSpec((tm, k), lambda i: (i, 0))]
        + [pl.BlockSpec(w.shape, lambda i: (0, 0)) for w in ws],
        out_specs=[pl.BlockSpec((tm, ws[wi].shape[1]), lambda i: (i, 0)) for wi, _, _ in outs],
        out_shape=[jax.ShapeDtypeStruct((n, ws[wi].shape[1]), dt) for wi, dt, _ in outs],
        compiler_params=_cparams("parallel"),
        name=name,
    )(x, *ws)


def _mm_res_ln_kernel(a_ref, w_ref, res_ref, g_ref, b_ref, o_ref, *, alpha):
    y = jnp.dot(a_ref[...].astype(BF16), w_ref[...], preferred_element_type=F32)
    o_ref[...] = _layer_norm(alpha * res_ref[...] + y, g_ref[...], b_ref[...])


def _mm_res_ln(a, w, res, g, b, alpha, name):
    n, k = a.shape
    d = w.shape[1]
    tm = _tile(n, 512)
    return pl.pallas_call(
        functools.partial(_mm_res_ln_kernel, alpha=alpha),
        grid=(n // tm,),
        in_specs=[pl.BlockSpec((tm, k), lambda i: (i, 0)),
                  pl.BlockSpec((k, d), lambda i: (0, 0)),
                  pl.BlockSpec((tm, d), lambda i: (i, 0)),
                  pl.BlockSpec((1, d), lambda i: (0, 0)),
                  pl.BlockSpec((1, d), lambda i: (0, 0))],
        out_specs=pl.BlockSpec((tm, d), lambda i: (i, 0)),
        out_shape=jax.ShapeDtypeStruct((n, d), F32),
        compiler_params=_cparams("parallel"),
        name=name,
    )(a, w, res, g, b)


def _sb_later_selector(span):
    s = jnp.arange(span)
    return (s[:, None] > s[None, :]).astype(BF16)


def _sb_stack_heads(q, head_dim):
    lane = lax.broadcasted_iota(jnp.int32, q.shape, 1)
    parts = [jnp.where((lane >= h * head_dim) & (lane < (h + 1) * head_dim), q, jnp.zeros_like(q))
             for h in range(LANES // head_dim)]
    return jnp.concatenate(parts, axis=0)


def _sb_unstack_heads(acc, tq, head_dim):
    lane = lax.broadcasted_iota(jnp.int32, (tq, LANES), 1)
    out = acc[:tq]
    for h in range(1, LANES // head_dim):
        out = jnp.where(lane >= h * head_dim, acc[h * tq:(h + 1) * tq], out)
    return out


def _sb_later_sums(lstay, u_ref, off):
    n = lstay.shape[1]
    hi, lo = _split_bf16(lstay)
    cols = []
    for c0 in range(0, n, 2 * LANES):
        c1 = min(c0 + 2 * LANES, n)
        u = u_ref[off + c0:off + n, off + c0:off + c1]
        cols.append(jnp.dot(hi[:, c0:], u, preferred_element_type=F32)
                    + jnp.dot(lo[:, c0:], u, preferred_element_type=F32))
    return cols[0] if len(cols) == 1 else jnp.concatenate(cols, axis=1)


def _sb_span(qs, k, v, valid, u_ref, u_off, acc, carry):
    z = lax.dot_general(qs, k, (((1,), (1,)), ((), ())), preferred_element_type=F32)
    ls = _log_sigmoid(z)
    lstay = ls - z
    if valid is not None:
        lstay = jnp.where(valid, lstay, 0.0)
    later = _sb_later_sums(lstay, u_ref, u_off)
    if carry is not None:
        later = later + carry
    w = jnp.exp(ls + later)
    if valid is not None:
        w = jnp.where(valid, w, 0.0)
    pv = jnp.dot(w.astype(BF16), v, preferred_element_type=F32)
    tot = jnp.sum(lstay, axis=-1, keepdims=True)
    if acc is None:
        return pv, tot
    return acc + pv, carry + tot


def _sb_prompt_kernel(q_ref, k_ref, v_ref, u_ref, o_ref, *, head_dim, span):
    qi = pl.program_id(2)
    tq = q_ref.shape[0]
    rows = tq * (LANES // head_dim)
    first = jnp.maximum(qi - (span // SB_QBLOCK - 1), 0)
    start = pl.multiple_of(first * SB_QBLOCK, SB_QBLOCK)
    row = lax.broadcasted_iota(jnp.int32, (rows, span), 0)
    col = lax.broadcasted_iota(jnp.int32, (rows, span), 1)
    valid = (start + col) < (qi * tq + (row & (tq - 1)))
    groups = []
    for g in range(q_ref.shape[1] // LANES):
        ls = slice(g * LANES, (g + 1) * LANES)
        qs = _sb_stack_heads(q_ref[:, ls], head_dim)
        acc, carry = _sb_span(qs, k_ref[pl.ds(start, span), ls], v_ref[pl.ds(start, span), ls],
                              valid, u_ref, 0, None, None)
        groups.append((ls, qs, acc, carry))

    def cond(state):
        j, _, carry = state
        return (j >= 0) & (jnp.max(carry) > -SB_UNDERFLOW)

    for ls, qs, acc, carry in groups:
        def body(state, ls=ls, qs=qs):
            j, acc, carry = state
            s = pl.multiple_of(j * SB_QBLOCK, SB_QBLOCK)
            acc, carry = _sb_span(qs, k_ref[pl.ds(s, SB_QBLOCK), ls], v_ref[pl.ds(s, SB_QBLOCK), ls],
                                  None, u_ref, 0, acc, carry)
            return j - 1, acc, carry

        _, acc, _ = lax.while_loop(cond, body, (first - 1, acc, carry))
        o_ref[:, ls] = _sb_unstack_heads(acc, tq, head_dim).astype(o_ref.dtype)


def _sb_prompt(q, k, v, batch, seq, head_dim):
    n, d = q.shape
    nqb = seq // SB_QBLOCK
    span = min(SB_SPAN_BLOCKS, nqb) * SB_QBLOCK
    u = _sb_later_selector(span)
    width = SB_LANE_GROUPS * LANES
    return pl.pallas_call(
        functools.partial(_sb_prompt_kernel, head_dim=head_dim, span=span),
        grid=(batch, d // width, nqb),
        in_specs=[pl.BlockSpec((SB_QBLOCK, width), lambda b, hp, qi: (b * nqb + qi, hp)),
                  pl.BlockSpec((seq, width), lambda b, hp, qi: (b, hp)),
                  pl.BlockSpec((seq, width), lambda b, hp, qi: (b, hp)),
                  pl.BlockSpec((span, span), lambda b, hp, qi: (0, 0))],
        out_specs=pl.BlockSpec((SB_QBLOCK, width), lambda b, hp, qi: (b * nqb + qi, hp)),
        out_shape=jax.ShapeDtypeStruct((n, d), BF16),
        compiler_params=_cparams("parallel", "parallel", "arbitrary"),
        name="sb_prompt",
    )(q, k, v, u)


def _sb_sample_kernel(q_ref, kn_ref, vn_ref, kt_ref, vt_ref, kp_hbm, vp_hbm, u_ref, o_ref,
                      kbuf, vbuf, sem, *, head_dim, past):
    b, hp = pl.program_id(0), pl.program_id(1)
    tq = q_ref.shape[0]
    tail = kt_ref.shape[0]
    qs = _sb_stack_heads(q_ref[...], head_dim)
    rows = qs.shape[0]
    pad = jnp.zeros((SB_QBLOCK - tq, LANES), BF16)
    k = jnp.concatenate([kt_ref[...].astype(BF16), kn_ref[...], pad], axis=0)
    v = jnp.concatenate([vt_ref[...].astype(BF16), vn_ref[...], pad], axis=0)
    span = tail + SB_QBLOCK
    row = lax.broadcasted_iota(jnp.int32, (rows, span), 0)
    col = lax.broadcasted_iota(jnp.int32, (rows, span), 1)
    valid = col < tail + (row & (tq - 1))
    acc, carry = _sb_span(qs, k, v, valid, u_ref, 0, None, None)

    heads = LANES // head_dim

    def block_copies(j):
        copies = []
        for h in range(heads):
            src = (b, pl.ds(j * SB_QBLOCK, SB_QBLOCK), hp * heads + h)
            copies.append(pltpu.make_async_copy(kp_hbm.at[src], kbuf.at[h], sem.at[0, h]))
            copies.append(pltpu.make_async_copy(vp_hbm.at[src], vbuf.at[h], sem.at[1, h]))
        return copies

    def cond(state):
        j, _, carry = state
        return (j >= 0) & (jnp.max(carry) > -SB_UNDERFLOW)

    def body(state):
        j, acc, carry = state
        copies = block_copies(j)
        for c in copies:
            c.start()
        for c in copies:
            c.wait()
        kb = jnp.concatenate([kbuf[h] for h in range(heads)], axis=1).astype(BF16)
        vb = jnp.concatenate([vbuf[h] for h in range(heads)], axis=1).astype(BF16)
        acc, carry = _sb_span(qs, kb, vb, None, u_ref, 0, acc, carry)
        return j - 1, acc, carry

    _, acc, _ = lax.while_loop(cond, body, ((past - tail) // SB_QBLOCK - 1, acc, carry))
    o_ref[...] = _sb_unstack_heads(acc, tq, head_dim).astype(o_ref.dtype)


def _sb_sample(q, kn, vn, kp, vp, batch, tq, head_dim):
    n, d = q.shape
    past = kp.shape[1]
    tail = min(SB_SPAN_BLOCKS - 1, past // SB_QBLOCK) * SB_QBLOCK
    u = _sb_later_selector(tail + SB_QBLOCK)
    kt = kp[:, past - tail:].reshape(batch, tail, d)
    vt = vp[:, past - tail:].reshape(batch, tail, d)
    heads = LANES // head_dim
    new = pl.BlockSpec((tq, LANES), lambda b, hp: (b, hp))
    tail_spec = pl.BlockSpec((None, tail, LANES), lambda b, hp: (b, 0, hp))
    return pl.pallas_call(
        functools.partial(_sb_sample_kernel, head_dim=head_dim, past=past),
        grid=(batch, d // LANES),
        in_specs=[new, new, new, tail_spec, tail_spec,
                  pl.BlockSpec(memory_space=pl.ANY), pl.BlockSpec(memory_space=pl.ANY),
                  pl.BlockSpec(u.shape, lambda b, hp: (0, 0))],
        out_specs=new,
        out_shape=jax.ShapeDtypeStruct((n, d), BF16),
        scratch_shapes=[pltpu.VMEM((heads, SB_QBLOCK, head_dim), F32),
                        pltpu.VMEM((heads, SB_QBLOCK, head_dim), F32),
                        pltpu.SemaphoreType.DMA((2, heads))],
        compiler_params=_cparams("parallel", "parallel"),
        name="sb_sample",
    )(q, kn, vn, kt, vt, kp, vp, u)


def _gelu_tanh(x):
    c = math.sqrt(2.0 / math.pi)
    return 0.5 * x * (1.0 + jnp.tanh(c * (x + 0.044715 * (x * x * x))))


def _sgu_kernel(x_ref, win_ref, lng_ref, lnb_ref, ws_ref, bs_ref, wout_ref, g_ref, b_ref,
                o_ref, *v_out, alpha, width, groups):
    x = x_ref[...]
    z = _gelu_tanh(jnp.dot(x.astype(BF16), win_ref[...], preferred_element_type=F32))
    u = z[:, :width]
    v = _layer_norm(z[:, width:], lng_ref[...], lnb_ref[...])
    if v_out:
        v_out[0][...] = v
    tm = x.shape[0]
    gd = width // groups
    vb = v.astype(BF16)
    rows = []
    for c in range(tm // SGU_LEN):
        r0 = c * SGU_LEN
        cols = []
        for g in range(groups):
            mixed = jnp.dot(ws_ref[g], vb[r0:r0 + SGU_LEN, g * gd:(g + 1) * gd],
                            preferred_element_type=F32) + bs_ref[g]
            cols.append(mixed)
        rows.append(jnp.concatenate(cols, axis=1))
    mixed = rows[0] if len(rows) == 1 else jnp.concatenate(rows, axis=0)
    y = jnp.dot((u * mixed).astype(BF16), wout_ref[...], preferred_element_type=F32)
    o_ref[...] = _layer_norm(alpha * x + y, g_ref[...], b_ref[...])


def _sgu(x, w_in, ln_g, ln_b, ws_mix, bs_mix, w_out, g, b, alpha, want_v):
    n, d = x.shape
    width = w_out.shape[0]
    groups = ws_mix.shape[0]
    tm = _tile(n, 256)
    const2 = lambda i: (0, 0)
    const3 = lambda i: (0, 0, 0)
    out_specs = [pl.BlockSpec((tm, d), lambda i: (i, 0))]
    out_shape = [jax.ShapeDtypeStruct((n, d), F32)]
    if want_v:
        out_specs.append(pl.BlockSpec((tm, width), lambda i: (i, 0)))
        out_shape.append(jax.ShapeDtypeStruct((n, width), F32))
    res = pl.pallas_call(
        functools.partial(_sgu_kernel, alpha=alpha, width=width, groups=groups),
        grid=(n // tm,),
        in_specs=[pl.BlockSpec((tm, d), lambda i: (i, 0)),
                  pl.BlockSpec(w_in.shape, const2),
                  pl.BlockSpec((1, width), const2),
                  pl.BlockSpec((1, width), const2),
                  pl.BlockSpec(ws_mix.shape, const3),
                  pl.BlockSpec(bs_mix.shape, const3),
                  pl.BlockSpec(w_out.shape, const2),
                  pl.BlockSpec((1, d), const2),
                  pl.BlockSpec((1, d), const2)],
        out_specs=out_specs,
        out_shape=out_shape,
        compiler_params=_cparams("parallel"),
        name="sgu",
    )(x, w_in, ln_g, ln_b, ws_mix, bs_mix, w_out, g, b)
    return res[0], (res[1] if want_v else None)


def _gla_proj_kernel(x_ref, wq_ref, wk_ref, wv_ref, wr_ref, wg1_ref, wg2_ref, bg_ref,
                     q_ref, k_ref, v_ref, r_ref, la_ref, *, qscale):
    xb = x_ref[...].astype(BF16)
    q_ref[...] = jnp.dot(xb, wq_ref[...], preferred_element_type=F32) * qscale
    k_ref[...] = jnp.dot(xb, wk_ref[...], preferred_element_type=F32)
    v_ref[...] = jnp.dot(xb, wv_ref[...], preferred_element_type=F32).astype(BF16)
    r_ref[...] = jnp.dot(xb, wr_ref[...], preferred_element_type=F32)
    t = jnp.dot(xb, wg1_ref[...], preferred_element_type=F32)
    gate = jnp.dot(t.astype(BF16), wg2_ref[...], preferred_element_type=F32) + bg_ref[...]
    la_ref[...] = _log_sigmoid(gate) / GLA_TAU


def _gla_proj(x, wq, wk, wv, wr, wg1, wg2, bg, qscale):
    n, d = x.shape
    dq, dv = wq.shape[1], wv.shape[1]
    tm = _tile(n, 512)
    const = lambda i: (0, 0)
    row = lambda i: (i, 0)
    return pl.pallas_call(
        functools.partial(_gla_proj_kernel, qscale=qscale),
        grid=(n // tm,),
        in_specs=[pl.BlockSpec((tm, d), row)]
        + [pl.BlockSpec(w.shape, const) for w in (wq, wk, wv, wr, wg1, wg2, bg)],
        out_specs=[pl.BlockSpec((tm, dq), row), pl.BlockSpec((tm, dq), row),
                   pl.BlockSpec((tm, dv), row), pl.BlockSpec((tm, dv), row),
                   pl.BlockSpec((tm, dq), row)],
        out_shape=[jax.ShapeDtypeStruct((n, dq), F32), jax.ShapeDtypeStruct((n, dq), F32),
                   jax.ShapeDtypeStruct((n, dv), BF16), jax.ShapeDtypeStruct((n, dv), F32),
                   jax.ShapeDtypeStruct((n, dq), F32)],
        compiler_params=_cparams("parallel"),
        name="gla_proj",
    )(x, wq, wk, wv, wr, wg1, wg2, bg)


def _gla_scan_kernel(q_ref, k_ref, v_ref, la_ref, s0_ref, o_ref, sfin_ref, st_ref,
                     *, heads, dk, dv):
    ti = pl.program_id(1)
    tt = q_ref.shape[0]
    nsb = tt // GLA_BLOCK

    @pl.when(ti == 0)
    def _():
        for h in range(heads):
            st_ref[h] = s0_ref[0, h].T

    row = lax.broadcasted_iota(jnp.int32, (tt, tt), 0)
    col = lax.broadcasted_iota(jnp.int32, (tt, tt), 1)
    same = (row // GLA_BLOCK) == (col // GLA_BLOCK)
    tril = same & (col <= row)
    bdcat = jnp.concatenate([jnp.where(tril, 1.0, 0.0), jnp.where(same, 1.0, 0.0)],
                            axis=0).astype(BF16)
    rsub = lax.broadcasted_iota(jnp.int32, (tt, dk), 0) // GLA_BLOCK

    for h in range(heads):
        ks = slice(h * dk, (h + 1) * dk)
        vs = slice(h * dv, (h + 1) * dv)
        hi, lo = _split_bf16(la_ref[:, ks])
        ct = (jnp.dot(bdcat, hi, preferred_element_type=F32)
              + jnp.dot(bdcat, lo, preferred_element_type=F32))
        cum, tot = ct[:tt], ct[tt:]
        k = k_ref[:, ks]
        qd = (q_ref[:, ks] * jnp.exp(cum)).astype(BF16)
        kd = (k * jnp.exp(-cum)).astype(BF16)
        kt = k * jnp.exp(tot - cum)
        sc = lax.dot_general(qd, kd, (((1,), (1,)), ((), ())), preferred_element_type=F32)
        sc = jnp.where(tril, sc, 0.0).astype(BF16)
        vh = v_ref[:, vs]
        o_intra = jnp.dot(sc, vh, preferred_element_type=F32)
        ktcat = jnp.concatenate(
            [jnp.where(rsub == i, kt, 0.0).astype(BF16) for i in range(nsb)], axis=1)
        ut = lax.dot_general(vh, ktcat, (((0,), (0,)), ((), ())), preferred_element_type=F32)
        st = st_ref[h]
        parts = []
        for i in range(nsb):
            r0 = i * GLA_BLOCK
            parts.append(lax.dot_general(qd[r0:r0 + GLA_BLOCK], st.astype(BF16),
                                         (((1,), (1,)), ((), ())), preferred_element_type=F32))
            st = st * jnp.exp(tot[r0:r0 + 1, :]) + ut[:, i * dk:(i + 1) * dk]
        st_ref[h] = st
        o_ref[:, vs] = o_intra + jnp.concatenate(parts, axis=0)

    @pl.when(ti == pl.num_programs(1) - 1)
    def _():
        for h in range(heads):
            sfin_ref[0, h] = st_ref[h].T


def _gla_scan(q, k, v, la, s0, batch, seq, heads):
    n, dq = q.shape
    dvt = v.shape[1]
    dk, dv = dq // heads, dvt // heads
    tt = SGU_LEN
    nt = seq // tt
    row = lambda b, t: (b * nt + t, 0)
    st = lambda b, t: (b, 0, 0, 0)
    return pl.pallas_call(
        functools.partial(_gla_scan_kernel, heads=heads, dk=dk, dv=dv),
        grid=(batch, nt),
        in_specs=[pl.BlockSpec((tt, dq), row), pl.BlockSpec((tt, dq), row),
                  pl.BlockSpec((tt, dvt), row), pl.BlockSpec((tt, dq), row),
                  pl.BlockSpec((1, heads, dk, dv), st)],
        out_specs=[pl.BlockSpec((tt, dvt), row), pl.BlockSpec((1, heads, dk, dv), st)],
        out_shape=[jax.ShapeDtypeStruct((n, dvt), F32),
                   jax.ShapeDtypeStruct((batch, heads, dk, dv), F32)],
        scratch_shapes=[pltpu.VMEM((heads, dv, dk), F32)],
        compiler_params=_cparams("parallel", "arbitrary"),
        name="gla_scan",
    )(q, k, v, la, s0)


def _gla_out_kernel(o_ref, r_ref, ng_ref, wo_ref, x_ref, g_ref, b_ref, y_ref, *, alpha, heads):
    o = o_ref[...]
    dv = o.shape[1] // heads
    normed = []
    for h in range(heads):
        oh = o[:, h * dv:(h + 1) * dv]
        ms = jnp.mean(oh * oh, axis=-1, keepdims=True)
        normed.append(oh * lax.rsqrt(ms + LN_EPS) * ng_ref[...])
    gated = jnp.concatenate(normed, axis=1) * _silu(r_ref[...])
    y = jnp.dot(gated.astype(BF16), wo_ref[...], preferred_element_type=F32)
    y_ref[...] = _layer_norm(alpha * x_ref[...] + y, g_ref[...], b_ref[...])


def _gla_out(o, r, norm_g, w_o, x, g, b, alpha, heads):
    n, dvt = o.shape
    d = x.shape[1]
    tm = _tile(n, 512)
    row = lambda i: (i, 0)
    const = lambda i: (0, 0)
    return pl.pallas_call(
        functools.partial(_gla_out_kernel, alpha=alpha, heads=heads),
        grid=(n // tm,),
        in_specs=[pl.BlockSpec((tm, dvt), row), pl.BlockSpec((tm, dvt), row),
                  pl.BlockSpec(norm_g.shape, const), pl.BlockSpec(w_o.shape, const),
                  pl.BlockSpec((tm, d), row), pl.BlockSpec((1, d), const),
                  pl.BlockSpec((1, d), const)],
        out_specs=pl.BlockSpec((tm, d), row),
        out_shape=jax.ShapeDtypeStruct((n, d), F32),
        compiler_params=_cparams("parallel"),
        name="gla_out",
    )(o, r, norm_g, w_o, x, g, b)


def _pool_kernel(x_ref, prev_ref, hist_ref, w_ref, sc_ref, g_ref, b_ref, o_ref, buf_ref,
                 *, alpha, n_valid):
    ti = pl.program_id(1)
    tm, d = x_ref.shape
    x = x_ref[...]

    @pl.when(ti == 0)
    def _():
        buf_ref[0:POOL_PAD, :] = hist_ref[0]

    @pl.when(ti > 0)
    def _():
        buf_ref[0:POOL_PAD, :] = prev_ref[...]

    buf_ref[POOL_PAD:POOL_PAD + tm, :] = x
    gd = d // len(POOL_WINDOWS)
    pos = ti * tm + lax.broadcasted_iota(jnp.int32, (tm, 1), 0)
    ys = []
    for g, win in enumerate(POOL_WINDOWS):
        cs = slice(g * gd, (g + 1) * gd)
        acc = x[:, cs]
        for j in range(1, win):
            acc = acc + buf_ref[POOL_PAD - j:POOL_PAD - j + tm, cs]
        count = jnp.minimum(pos + 1 + n_valid, win).astype(F32)
        dlt = acc / count - x[:, cs]
        ys.append(jnp.dot(dlt.astype(BF16), w_ref[g], preferred_element_type=F32))
    y = jnp.concatenate(ys, axis=1) * sc_ref[...]
    o_ref[...] = _layer_norm(alpha * x + y, g_ref[...], b_ref[...])


def _pool(x, hist, w_pool, scale, g, b, alpha, batch, seq, n_valid):
    n, d = x.shape
    tm = _tile(seq, 256)
    nt = seq // tm
    ratio = tm // POOL_PAD
    const2 = lambda bb, t: (0, 0)
    row = lambda bb, t: (bb * nt + t, 0)
    return pl.pallas_call(
        functools.partial(_pool_kernel, alpha=alpha, n_valid=n_valid),
        grid=(batch, nt),
        in_specs=[pl.BlockSpec((tm, d), row),
                  pl.BlockSpec((POOL_PAD, d),
                               lambda bb, t: (jnp.maximum((bb * nt + t) * ratio - 1, 0), 0)),
                  pl.BlockSpec((1, POOL_PAD, d), lambda bb, t: (bb, 0, 0)),
                  pl.BlockSpec(w_pool.shape, lambda bb, t: (0, 0, 0)),
                  pl.BlockSpec((1, d), const2), pl.BlockSpec((1, d), const2),
                  pl.BlockSpec((1, d), const2)],
        out_specs=pl.BlockSpec((tm, d), row),
        out_shape=jax.ShapeDtypeStruct((n, d), F32),
        scratch_shapes=[pltpu.VMEM((POOL_PAD + tm, d), F32)],
        compiler_params=_cparams("parallel", "arbitrary"),
        name="pool",
    )(x, x, hist, w_pool, scale, g, b)


def _moe_route_math(x, wr_hi, wr_lo, br, n_exp, n_grp):
    hi, lo = _split_bf16(x)
    logit = (jnp.dot(hi, wr_hi, preferred_element_type=F32)
             + jnp.dot(lo, wr_hi, preferred_element_type=F32)
             + jnp.dot(hi, wr_lo, preferred_element_type=F32)) + br
    lane_i = lax.broadcasted_iota(jnp.int32, logit.shape, 1)
    lane = lane_i.astype(F32)
    neg = jnp.float32(-jnp.inf)
    big = jnp.float32(LANES)
    is_grp = (lane_i >= n_exp) & (lane_i < n_exp + n_grp)
    gl = jnp.where(is_grp, logit, neg)
    gmax = jnp.max(gl, axis=-1, keepdims=True)
    g_sel = jnp.min(jnp.where(gl == gmax, lane, big), axis=-1, keepdims=True) - n_exp
    g_prob = 1.0 / jnp.sum(jnp.where(is_grp, jnp.exp(gl - gmax), 0.0), axis=-1, keepdims=True)
    lane_grp = (lane_i // MOE_EXPERTS_PER_GROUP).astype(F32)
    in_grp = (lane_i < n_exp) & (lane_grp == g_sel)
    el = jnp.where(in_grp, logit, neg)
    m1 = jnp.max(el, axis=-1, keepdims=True)
    i1 = jnp.min(jnp.where(el == m1, lane, big), axis=-1, keepdims=True)
    el2 = jnp.where(lane == i1, neg, el)
    m2 = jnp.max(el2, axis=-1, keepdims=True)
    i2 = jnp.min(jnp.where(el2 == m2, lane, big), axis=-1, keepdims=True)
    e2 = jnp.exp(m2 - m1)
    w1 = g_prob / (1.0 + e2)
    w2 = g_prob * e2 / (1.0 + e2)
    return i1, i2, w1, w2


_META_E1, _META_E2, _META_R1, _META_R2, _META_W1, _META_W2 = range(6)


def _moe_route_kernel(x_ref, wrh_ref, wrl_ref, br_ref, meta_ref, cnt_ref, *, n_exp, n_grp):
    i1, i2, w1, w2 = _moe_route_math(x_ref[...], wrh_ref[...], wrl_ref[...], br_ref[...], n_exp, n_grp)
    tm = x_ref.shape[0]
    lane = lax.broadcasted_iota(jnp.int32, (tm, LANES), 1).astype(F32)
    m1 = jnp.where(lane == i1, 1.0, 0.0)
    m2 = jnp.where(lane == i2, 1.0, 0.0)
    both = m1 + m2
    r = lax.broadcasted_iota(jnp.int32, (tm, tm), 0)
    c = lax.broadcasted_iota(jnp.int32, (tm, tm), 1)
    before = jnp.where(c < r, 1.0, 0.0).astype(BF16)
    rank = jnp.dot(before, both.astype(BF16), preferred_element_type=F32)
    r1 = jnp.sum(m1 * rank, axis=-1, keepdims=True)
    r2 = jnp.sum(m2 * rank, axis=-1, keepdims=True)
    meta = jnp.zeros((tm, LANES), F32)
    for idx, col in ((_META_E1, i1), (_META_E2, i2), (_META_R1, r1), (_META_R2, r2),
                     (_META_W1, w1), (_META_W2, w2)):
        meta = jnp.where(lane == idx, col, meta)
    meta_ref[...] = meta
    cnt_ref[0] = jnp.broadcast_to(jnp.sum(both, axis=0, keepdims=True), cnt_ref.shape[1:])


def _moe_route(x, wr_hi, wr_lo, br, n_exp, n_grp):
    n, d = x.shape
    tm = MOE_TILE
    const = lambda i: (0, 0)
    return pl.pallas_call(
        functools.partial(_moe_route_kernel, n_exp=n_exp, n_grp=n_grp),
        grid=(n // tm,),
        in_specs=[pl.BlockSpec((tm, d), lambda i: (i, 0)),
                  pl.BlockSpec(wr_hi.shape, const), pl.BlockSpec(wr_lo.shape, const),
                  pl.BlockSpec((1, LANES), const)],
        out_specs=[pl.BlockSpec((tm, LANES), lambda i: (i, 0)),
                   pl.BlockSpec((1, 8, LANES), lambda i: (i, 0, 0))],
        out_shape=[jax.ShapeDtypeStruct((n, LANES), F32),
                   jax.ShapeDtypeStruct((n // tm, 8, LANES), F32)],
        compiler_params=_cparams("parallel"),
        name="moe_route",
    )(x, wr_hi, wr_lo, br)


def _moe_placement(meta, base_row, n_stage, weighted):
    tm = meta.shape[0]
    lane = lax.broadcasted_iota(jnp.int32, (tm, LANES), 1).astype(F32)
    col = lax.broadcasted_iota(jnp.int32, (tm, n_stage), 1).astype(F32)
    out = None
    for e_idx, r_idx, w_idx in ((_META_E1, _META_R1, _META_W1), (_META_E2, _META_R2, _META_W2)):
        e = meta[:, e_idx:e_idx + 1]
        pos = jnp.sum(jnp.where(lane == e, base_row, 0.0), axis=-1, keepdims=True) + meta[:, r_idx:r_idx + 1]
        val = meta[:, w_idx:w_idx + 1] if weighted else 1.0
        term = jnp.where(col == pos, val, 0.0)
        out = term if out is None else out + term
    return out.astype(BF16)


def _moe_segment_copies(tile, n_exp, seg_start, seg_len, stage_base, copy_rows):
    for e in range(n_exp):
        idx = tile * n_exp + e
        src, dst, n16 = stage_base[idx], seg_start[idx], seg_len[idx] // MOE_ALIGN

        def piece(c, carry, src=src, dst=dst):
            copy_rows(pl.multiple_of(dst + c * MOE_ALIGN, MOE_ALIGN),
                      pl.multiple_of(src + c * MOE_ALIGN, MOE_ALIGN))
            return carry

        lax.fori_loop(0, n16, piece, 0)


def _moe_experts_kernel(seg_start, seg_len, stage_base, off,
                        x_ref, meta_ref, base_ref, wup_ref, wdn_ref, g_ref, b_ref, o_ref,
                        buf_ref, stg_ref, *, alpha, n_exp, tiles_per_unit, hidden):
    u, s = pl.program_id(0), pl.program_id(1)
    n_stage = stg_ref.shape[0]

    @pl.when((u == 0) & (s == 0))
    def _():
        buf_ref[...] = jnp.zeros_like(buf_ref)

    @pl.when(s < tiles_per_unit)
    def _():
        place = _moe_placement(meta_ref[...], base_ref[0, 0:1, :], n_stage, weighted=False)
        xb = x_ref[...].astype(BF16)
        for c0 in range(0, n_stage, 2 * LANES):
            rows = lax.dot_general(place[:, c0:c0 + 2 * LANES], xb, (((0,), (0,)), ((), ())),
                                   preferred_element_type=F32)
            stg_ref[c0:c0 + 2 * LANES, :] = rows.astype(BF16)

        def copy_rows(unit_row, stage_row):
            buf_ref[pl.ds(unit_row, MOE_ALIGN), :] = stg_ref[pl.ds(stage_row, MOE_ALIGN), :]

        _moe_segment_copies(u * tiles_per_unit + s, n_exp, seg_start, seg_len, stage_base, copy_rows)

    @pl.when((s >= tiles_per_unit) & (s < tiles_per_unit + n_exp))
    def _():
        e = s - tiles_per_unit
        start = off[u * (n_exp + 1) + e]
        n = off[u * (n_exp + 1) + e + 1] - start

        def ffn_rows(r0, size, n_valid):
            r0 = pl.multiple_of(r0, MOE_ALIGN)
            xs = buf_ref[pl.ds(r0, size), :]
            h = jnp.dot(xs, wup_ref[0], preferred_element_type=F32)
            act = _silu(h[:, :hidden]) * h[:, hidden:]
            y = jnp.dot(act.astype(BF16), wdn_ref[0], preferred_element_type=F32)
            if n_valid is not None:
                row = lax.broadcasted_iota(jnp.int32, y.shape, 0)
                y = jnp.where(row < n_valid, y, xs.astype(F32))
            buf_ref[pl.ds(r0, size), :] = y.astype(BF16)

        n_full = n // MOE_FFN_ROWS

        def full(c, carry):
            ffn_rows(start + c * MOE_FFN_ROWS, MOE_FFN_ROWS, None)
            return carry

        lax.fori_loop(0, n_full, full, 0)
        rem = n - n_full * MOE_FFN_ROWS
        tail = start + n_full * MOE_FFN_ROWS
        for size in range(LANES, MOE_FFN_ROWS + 1, LANES):
            @pl.when((rem > size - LANES) & (rem <= size))
            def _(size=size):
                ffn_rows(tail, size, rem)

    @pl.when(s >= tiles_per_unit + n_exp)
    def _():
        t = s - tiles_per_unit - n_exp

        def copy_rows(unit_row, stage_row):
            stg_ref[pl.ds(stage_row, MOE_ALIGN), :] = buf_ref[pl.ds(unit_row, MOE_ALIGN), :]

        _moe_segment_copies(u * tiles_per_unit + t, n_exp, seg_start, seg_len, stage_base, copy_rows)
        place = _moe_placement(meta_ref[...], base_ref[0, 0:1, :], n_stage, weighted=True)
        f = jnp.dot(place, stg_ref[...], preferred_element_type=F32)
        o_ref[...] = _layer_norm(alpha * x_ref[...] + f, g_ref[...], b_ref[...])


def _moe(x, wr_hi, wr_lo, br, w_up, w_dn, g, b, alpha, n_grp):
    n, d = x.shape
    n_exp, _, two_f = w_up.shape
    hidden = two_f // 2
    tm = MOE_TILE
    unit = min(MOE_UNIT, n)
    tpu = unit // tm
    n_tiles, n_units = n // tm, n // unit
    worst_pad = MOE_ALIGN - 1
    n_stage = -(-(2 * tm + n_exp * worst_pad) // (2 * LANES)) * (2 * LANES)
    cap = -(-(2 * unit + n_exp * tpu * worst_pad) // MOE_FFN_ROWS) * MOE_FFN_ROWS + MOE_FFN_ROWS

    meta, cnt = _moe_route(x, wr_hi, wr_lo, br, n_exp, n_grp)
    cnt = cnt[:, 0, :n_exp].astype(jnp.int32)
    seg_len = (cnt + worst_pad) // MOE_ALIGN * MOE_ALIGN
    stage_base = jnp.cumsum(seg_len, axis=1) - seg_len
    sl = seg_len.reshape(n_units, tpu, n_exp)
    unit_len = jnp.sum(sl, axis=1)
    off = jnp.concatenate([jnp.zeros((n_units, 1), jnp.int32), jnp.cumsum(unit_len, axis=1)], axis=1)
    seg_start = off[:, None, :n_exp] + jnp.cumsum(sl, axis=1) - sl
    base_rows = jnp.zeros((n_tiles, 8, LANES), F32).at[:, 0, :n_exp].set(stage_base.astype(F32))
    tables = (seg_start.reshape(-1), seg_len.reshape(-1), stage_base.reshape(-1), off.reshape(-1))

    def tile_of(u, s):
        t = jnp.where(s < tpu, s, jnp.where(s >= tpu + n_exp, s - tpu - n_exp, tpu - 1))
        return u * tpu + t

    tile_row = lambda u, s, *_: (tile_of(u, s), 0)
    tile_row3 = lambda u, s, *_: (tile_of(u, s), 0, 0)
    expert = lambda u, s, *_: (jnp.clip(s - tpu, 0, n_exp - 1), 0, 0)
    out_row = lambda u, s, *_: (u * tpu + jnp.clip(s - tpu - n_exp, 0, tpu - 1), 0)
    const = lambda u, s, *_: (0, 0)

    return pl.pallas_call(
        functools.partial(_moe_experts_kernel, alpha=alpha, n_exp=n_exp, tiles_per_unit=tpu,
                          hidden=hidden),
        grid_spec=pltpu.PrefetchScalarGridSpec(
            num_scalar_prefetch=4, grid=(n_units, 2 * tpu + n_exp),
            in_specs=[pl.BlockSpec((tm, d), tile_row), pl.BlockSpec((tm, LANES), tile_row),
                      pl.BlockSpec((1, 8, LANES), tile_row3),
                      pl.BlockSpec((1, d, two_f), expert), pl.BlockSpec((1, hidden, d), expert),
                      pl.BlockSpec((1, d), const), pl.BlockSpec((1, d), const)],
            out_specs=pl.BlockSpec((tm, d), out_row),
            scratch_shapes=[pltpu.VMEM((cap, d), BF16), pltpu.VMEM((n_stage, d), BF16)]),
        out_shape=jax.ShapeDtypeStruct((n, d), F32),
        compiler_params=_cparams("arbitrary", "arbitrary"),
        name="moe_experts",
    )(*tables, x, meta, base_rows, w_up, w_dn, g, b)


def _router_params(w_group, b_group, w_router, b_router):
    d, n_grp = w_group.shape
    n_exp = w_router.shape[1]
    w = jnp.zeros((d, LANES), F32).at[:, :n_exp].set(w_router).at[:, n_exp:n_exp + n_grp].set(w_group)
    hi = w.astype(BF16)
    lo = (w - hi.astype(F32)).astype(BF16)
    br = jnp.zeros((1, LANES), F32).at[0, :n_exp].set(b_router).at[0, n_exp:n_exp + n_grp].set(b_group)
    return hi, lo, br


def _sgu_mix_params(w_s, b_s, t_len, gd):
    groups = w_s.shape[0]
    L = min(t_len, SGU_LEN)
    pos = jnp.arange(L)
    allowed = (pos[None, :] // CHUNK) <= (pos[:, None] // CHUNK)
    w = jnp.where(allowed[None], w_s[:, :L, :L], 0.0)
    rep = SGU_LEN // L
    if rep > 1:
        eye = jnp.eye(rep, dtype=F32)
        w = jnp.einsum("ab,gts->gatbs", eye, w).reshape(groups, SGU_LEN, SGU_LEN)
    bias = jnp.tile(b_s[:, :L], (1, rep))
    bias = jnp.broadcast_to(bias[:, :, None], (groups, SGU_LEN, gd))
    return w.astype(BF16), bias.astype(F32)


def kernel(x_prompt, x_sample, cache_sb_k, cache_sb_v, state_gla, state_pool, sb_w_qkv, sb_w_o, sgu_w_in, sgu_ln_g, sgu_ln_b, sgu_w_s, sgu_b_s, sgu_w_out, gla_w_in, gla_w_g1, gla_w_g2, gla_b_g, gla_norm_g, gla_w_o, pool_w, pool_scale, ln1_g, ln1_b, ln2_g, ln2_b, moe_w_group, moe_b_group, moe_w_router, moe_b_router, moe_w_up, moe_w_down):
    depth = ln1_g.shape[0]
    d = x_prompt.shape[-1]
    alpha = (2.0 * depth) ** 0.25
    sb_heads, sb_hd = cache_sb_k.shape[3], cache_sb_k.shape[4]
    gla_heads, gla_dk, gla_dv = state_gla.shape[2], state_gla.shape[3], state_gla.shape[4]
    n_grp = moe_w_group.shape[-1]
    dq, dvt = gla_heads * gla_dk, gla_heads * gla_dv
    n_hist = state_pool.shape[2]
    sgu_groups = sgu_w_s.shape[1]
    sgu_width = sgu_w_out.shape[1]

    bf = lambda a: a.astype(BF16)
    vec = lambda a: a.reshape(1, -1)

    def run(x3, past):
        batch, seq, _ = x3.shape
        n = batch * seq
        x = x3.reshape(n, d)
        sb_k, sb_v, sgu_v, gla_s, pool_h = [], [], [], [], []
        for i in range(depth):
            m, j = i % 4, i // 4
            g1, b1 = vec(ln1_g[i]), vec(ln1_b[i])
            if m == 0:
                wq, wk, wv = (bf(sb_w_qkv[j][:, c * d:(c + 1) * d]) for c in range(3))
                q, k, v, kb, vb = _proj(
                    x, [wq, wk, wv],
                    [(0, BF16, sb_hd ** -0.5), (1, F32, 1.0), (2, F32, 1.0), (1, BF16, 1.0), (2, BF16, 1.0)],
                    "sb_qkv")
                if past is None:
                    o = _sb_prompt(q, kb, vb, batch, seq, sb_hd)
                else:
                    o = _sb_sample(q, kb, vb, past[0][j], past[1][j], batch, seq, sb_hd)
                sb_k.append(k.reshape(batch, seq, sb_heads, sb_hd))
                sb_v.append(v.reshape(batch, seq, sb_heads, sb_hd))
                x = _mm_res_ln(o, bf(sb_w_o[j]), x, g1, b1, alpha, "sb_out")
            elif m == 1:
                ws_mix, bs_mix = _sgu_mix_params(sgu_w_s[j], sgu_b_s[j], seq, sgu_width // sgu_groups)
                x, v = _sgu(x, bf(sgu_w_in[j]), vec(sgu_ln_g[j]), vec(sgu_ln_b[j]), ws_mix, bs_mix,
                            bf(sgu_w_out[j]), g1, b1, alpha, past is not None)
                if v is not None:
                    sgu_v.append(v.reshape(batch, seq, sgu_width))
            elif m == 2:
                w_in = gla_w_in[j]
                wg1 = jnp.zeros((d, LANES), F32).at[:, :gla_w_g1.shape[-1]].set(gla_w_g1[j])
                wg2 = jnp.zeros((LANES, dq), F32).at[:gla_w_g2.shape[1]].set(gla_w_g2[j])
                q, k, v, r, la = _gla_proj(
                    x, bf(w_in[:, :dq]), bf(w_in[:, dq:2 * dq]), bf(w_in[:, 2 * dq:2 * dq + dvt]),
                    bf(w_in[:, 2 * dq + dvt:]), bf(wg1), bf(wg2), vec(gla_b_g[j]), gla_dk ** -0.5)
                if past is None:
                    s0 = jnp.zeros((batch, gla_heads, gla_dk, gla_dv), F32)
                    seq_p = seq
                else:
                    s0 = past[2][j]
                    seq_p = -(-seq // SGU_LEN) * SGU_LEN
                    padrows = lambda a: jnp.pad(a.reshape(batch, seq, -1),
                                                ((0, 0), (0, seq_p - seq), (0, 0))).reshape(batch * seq_p, -1)
                    q, k, v, la = padrows(q), padrows(k), padrows(v), padrows(la)
                o, s = _gla_scan(q, k, v, la, s0, batch, seq_p, gla_heads)
                if seq_p != seq:
                    o = o.reshape(batch, seq_p, dvt)[:, :seq].reshape(n, dvt)
                gla_s.append(s)
                x = _gla_out(o, r, vec(gla_norm_g[j]), bf(gla_w_o[j]), x, g1, b1, alpha, gla_heads)
            else:
                if past is None:
                    hist = jnp.zeros((batch, POOL_PAD, d), F32)
                    n_valid = 0
                else:
                    hist = jnp.pad(past[3][j], ((0, 0), (POOL_PAD - n_hist, 0), (0, 0)))
                    n_valid = n_hist
                x_in = x.reshape(batch, seq, d)
                if past is None:
                    pool_h.append(x_in[:, seq - n_hist:])
                else:
                    pool_h.append(jnp.concatenate([past[3][j], x_in], axis=1)[:, -n_hist:])
                x = _pool(x, hist, bf(pool_w[j]), vec(pool_scale[j]), g1, b1, alpha, batch, seq, n_valid)
            wr_hi, wr_lo, br = _router_params(moe_w_group[i], moe_b_group[i], moe_w_router[i], moe_b_router[i])
            x = _moe(x, wr_hi, wr_lo, br, bf(moe_w_up[i]), bf(moe_w_down[i]),
                     vec(ln2_g[i]), vec(ln2_b[i]), alpha, n_grp)
        return x.reshape(batch, seq, d), sb_k, sb_v, sgu_v, gla_s, pool_h

    y_p, kp, vp, _, sp, hp = run(x_prompt, None)
    y_s, ks, vs, us, ss, hs = run(x_sample, (cache_sb_k, cache_sb_v, state_gla, state_pool))
    return (y_p, y_s, jnp.stack(kp), jnp.stack(vp), jnp.stack(ks), jnp.stack(vs), jnp.stack(us),
            jnp.stack(sp), jnp.stack(ss), jnp.stack(hp), jnp.stack(hs))
```

```python
import functools
import math

import jax
import jax.numpy as jnp
from jax import lax
from jax.experimental import pallas as pl
from jax.experimental.pallas import tpu as pltpu

F32 = jnp.float32
BF16 = jnp.bfloat16

LN_EPS = 1e-5
CHUNK = 64
SGU_LEN = 128
SB_QBLOCK = 128
SB_SPAN_BLOCKS = 3
SB_LANE_GROUPS = 2
SB_UNDERFLOW = 104.0
GLA_BLOCK = 16
GLA_TAU = 16.0
POOL_WINDOWS = (2, 4, 8, 16)
POOL_PAD = 16
MOE_EXPERTS_PER_GROUP = 4
MOE_TILE = 256
MOE_UNIT = 4096
MOE_ALIGN = 16
MOE_FFN_ROWS = 512
LANES = 128
VMEM_LIMIT = 56 * 1024 * 1024


def _cparams(*sem):
    return pltpu.CompilerParams(dimension_semantics=sem, vmem_limit_bytes=VMEM_LIMIT)


def _tile(n, pref):
    t = min(n, pref)
    while n % t:
        t -= 8
    return t


def _layer_norm(x, g, b):
    mu = jnp.mean(x, axis=-1, keepdims=True)
    xc = x - mu
    var = jnp.mean(xc * xc, axis=-1, keepdims=True)
    return xc * lax.rsqrt(var + LN_EPS) * g + b


def _split_bf16(x):
    hi = x.astype(BF16)
    lo = (x - hi.astype(F32)).astype(BF16)
    return hi, lo


def _log_sigmoid(z):
    return jnp.minimum(z, 0.0) - jnp.log(1.0 + jnp.exp(-jnp.abs(z)))


def _silu(x):
    return x / (1.0 + jnp.exp(-x))


def _proj_kernel(x_ref, *refs, outs):
    n_w = len(refs) - len(outs)
    xb = x_ref[...].astype(BF16)
    ys = [jnp.dot(xb, w_ref[...], preferred_element_type=F32) for w_ref in refs[:n_w]]
    for (wi, _, scale), o_ref in zip(outs, refs[n_w:]):
        o_ref[...] = (ys[wi] * scale).astype(o_ref.dtype)


def _proj(x, ws, outs, name):
    n, k = x.shape
    tm = _tile(n, 512)
    return pl.pallas_call(
        functools.partial(_proj_kernel, outs=tuple(outs)),
        grid=(n // tm,),
        in_specs=[pl.BlockSpec((tm, k), lambda i: (i, 0))]
        + [pl.BlockSpec(w.shape, lambda i: (0, 0)) for w in ws],
        out_specs=[pl.BlockSpec((tm, ws[wi].shape[1]), lambda i: (i, 0)) for wi, _, _ in outs],
        out_shape=[jax.ShapeDtypeStruct((n, ws[wi].shape[1]), dt) for wi, dt, _ in outs],
        compiler_params=_cparams("parallel"),
        name=name,
    )(x, *ws)


def _mm_res_ln_kernel(a_ref, w_ref, res_ref, g_ref, b_ref, o_ref, *, alpha):
    y = jnp.dot(a_ref[...].astype(BF16), w_ref[...], preferred_element_type=F32)
    o_ref[...] = _layer_norm(alpha * res_ref[...] + y, g_ref[...], b_ref[...])


def _mm_res_ln(a, w, res, g, b, alpha, name):
    n, k = a.shape
    d = w.shape[1]
    tm = _tile(n, 512)
    return pl.pallas_call(
        functools.partial(_mm_res_ln_kernel, alpha=alpha),
        grid=(n // tm,),
        in_specs=[pl.BlockSpec((tm, k), lambda i: (i, 0)),
                  pl.BlockSpec((k, d), lambda i: (0, 0)),
                  pl.BlockSpec((tm, d), lambda i: (i, 0)),
                  pl.BlockSpec((1, d), lambda i: (0, 0)),
                  pl.BlockSpec((1, d), lambda i: (0, 0))],
        out_specs=pl.BlockSpec((tm, d), lambda i: (i, 0)),
        out_shape=jax.ShapeDtypeStruct((n, d), F32),
        compiler_params=_cparams("parallel"),
        name=name,
    )(a, w, res, g, b)


def _sb_later_selector(span):
    s = jnp.arange(span)
    return (s[:, None] > s[None, :]).astype(BF16)


def _sb_stack_heads(q, head_dim):
    lane = lax.broadcasted_iota(jnp.int32, q.shape, 1)
    parts = [jnp.where((lane >= h * head_dim) & (lane < (h + 1) * head_dim), q, jnp.zeros_like(q))
             for h in range(LANES // head_dim)]
    return jnp.concatenate(parts, axis=0)


def _sb_unstack_heads(acc, tq, head_dim):
    lane = lax.broadcasted_iota(jnp.int32, (tq, LANES), 1)
    out = acc[:tq]
    for h in range(1, LANES // head_dim):
        out = jnp.where(lane >= h * head_dim, acc[h * tq:(h + 1) * tq], out)
    return out


def _sb_later_sums(lstay, u_ref, off):
    n = lstay.shape[1]
    hi, lo = _split_bf16(lstay)
    cols = []
    for c0 in range(0, n, 2 * LANES):
        c1 = min(c0 + 2 * LANES, n)
        u = u_ref[off + c0:off + n, off + c0:off + c1]
        cols.append(jnp.dot(hi[:, c0:], u, preferred_element_type=F32)
                    + jnp.dot(lo[:, c0:], u, preferred_element_type=F32))
    return cols[0] if len(cols) == 1 else jnp.concatenate(cols, axis=1)


def _sb_weights(z, valid, u_ref, carry):
    ls = _log_sigmoid(z)
    lstay = ls - z
    if valid is not None:
        lstay = jnp.where(valid, lstay, 0.0)
    later = _sb_later_sums(lstay, u_ref, 0)
    if carry is not None:
        later = later + carry
    w = jnp.exp(ls + later)
    if valid is not None:
        w = jnp.where(valid, w, 0.0)
    return w.astype(BF16), jnp.sum(lstay, axis=-1, keepdims=True)


def _dot_t(a, bt):
    return lax.dot_general(a, bt, (((1,), (1,)), ((), ())), preferred_element_type=F32)


def _sb_qkv_kernel(x_ref, wq_ref, wkt_ref, wvt_ref, q_ref, kt_ref, vt_ref, ktb_ref, vtb_ref, *, qscale):
    xb = x_ref[...].astype(BF16)
    q_ref[...] = (jnp.dot(xb, wq_ref[...], preferred_element_type=F32) * qscale).astype(BF16)
    kt = _dot_t(wkt_ref[...], xb)
    kt_ref[...] = kt
    ktb_ref[...] = kt.astype(BF16)
    vt = _dot_t(wvt_ref[...], xb)
    vt_ref[...] = vt
    vtb_ref[...] = vt.astype(BF16)


def _sb_qkv(x, wq, wkt, wvt, batch, seq, qscale):
    n, d = x.shape
    tm = _tile(seq, 512)
    nt = seq // tm
    row = lambda b, t: (b * nt + t, 0)
    const = lambda b, t: (0, 0)
    col = pl.BlockSpec((None, d, tm), lambda b, t: (b, 0, t))
    return pl.pallas_call(
        functools.partial(_sb_qkv_kernel, qscale=qscale),
        grid=(batch, nt),
        in_specs=[pl.BlockSpec((tm, d), row), pl.BlockSpec((d, d), const),
                  pl.BlockSpec((d, d), const), pl.BlockSpec((d, d), const)],
        out_specs=[pl.BlockSpec((tm, d), row), col, col, col, col],
        out_shape=[jax.ShapeDtypeStruct((n, d), BF16),
                   jax.ShapeDtypeStruct((batch, d, seq), F32), jax.ShapeDtypeStruct((batch, d, seq), F32),
                   jax.ShapeDtypeStruct((batch, d, seq), BF16), jax.ShapeDtypeStruct((batch, d, seq), BF16)],
        compiler_params=_cparams("parallel", "parallel"),
        name="sb_qkv",
    )(x, wq, wkt, wvt)


def _sb_prompt_kernel(q_ref, *refs, head_dim, n_span):
    kt_blocks, vt_blocks = refs[:n_span], refs[n_span:2 * n_span]
    kt_hbm, vt_hbm, u_ref, o_ref, kbuf, vbuf, sem = refs[2 * n_span:]
    b, hp, qi = pl.program_id(0), pl.program_id(1), pl.program_id(2)
    tq = q_ref.shape[0]
    width = q_ref.shape[1]
    rows = tq * (LANES // head_dim)
    span = n_span * SB_QBLOCK
    first = jnp.maximum(qi - (n_span - 1), 0)
    row = lax.broadcasted_iota(jnp.int32, (rows, span), 0)
    col = lax.broadcasted_iota(jnp.int32, (rows, span), 1)
    valid = (first * SB_QBLOCK + col) < (qi * tq + (row & (tq - 1)))
    groups = []
    for g in range(width // LANES):
        ls = slice(g * LANES, (g + 1) * LANES)
        qs = _sb_stack_heads(q_ref[:, ls], head_dim)
        kt = jnp.concatenate([r[ls, :] for r in kt_blocks], axis=1)
        vt = jnp.concatenate([r[ls, :] for r in vt_blocks], axis=1)
        w, tot = _sb_weights(jnp.dot(qs, kt, preferred_element_type=F32), valid, u_ref, None)
        groups.append((g, qs, _dot_t(w, vt), tot))

    def cond(state):
        j, _, carry = state
        return (j >= 0) & (jnp.max(carry) > -SB_UNDERFLOW)

    for g, qs, acc, carry in groups:
        def body(state, g=g, qs=qs):
            j, acc, carry = state
            src = (b, pl.ds(hp * width + g * LANES, LANES), pl.ds(j * SB_QBLOCK, SB_QBLOCK))
            kc = pltpu.make_async_copy(kt_hbm.at[src], kbuf, sem.at[0])
            vc = pltpu.make_async_copy(vt_hbm.at[src], vbuf, sem.at[1])
            kc.start()
            vc.start()
            kc.wait()
            vc.wait()
            w, tot = _sb_weights(jnp.dot(qs, kbuf[...], preferred_element_type=F32), None, u_ref, carry)
            return j - 1, acc + _dot_t(w, vbuf[...]), carry + tot

        _, acc, _ = lax.while_loop(cond, body, (first - 1, acc, carry))
        o_ref[:, g * LANES:(g + 1) * LANES] = _sb_unstack_heads(acc, tq, head_dim).astype(o_ref.dtype)


def _sb_prompt(q, kt, vt, batch, seq, head_dim):
    n, d = q.shape
    nqb = seq // SB_QBLOCK
    n_span = min(SB_SPAN_BLOCKS, nqb)
    u = _sb_later_selector(n_span * SB_QBLOCK)
    width = SB_LANE_GROUPS * LANES

    def span_block(i):
        return pl.BlockSpec((None, width, SB_QBLOCK),
                            lambda b, hp, qi: (b, hp, jnp.maximum(qi - (n_span - 1), 0) + i))

    blocks = [span_block(i) for i in range(n_span)]
    q_spec = pl.BlockSpec((SB_QBLOCK, width), lambda b, hp, qi: (b * nqb + qi, hp))
    return pl.pallas_call(
        functools.partial(_sb_prompt_kernel, head_dim=head_dim, n_span=n_span),
        grid=(batch, d // width, nqb),
        in_specs=[q_spec] + blocks + blocks
        + [pl.BlockSpec(memory_space=pl.ANY), pl.BlockSpec(memory_space=pl.ANY),
           pl.BlockSpec(u.shape, lambda b, hp, qi: (0, 0))],
        out_specs=q_spec,
        out_shape=jax.ShapeDtypeStruct((n, d), BF16),
        scratch_shapes=[pltpu.VMEM((LANES, SB_QBLOCK), BF16), pltpu.VMEM((LANES, SB_QBLOCK), BF16),
                        pltpu.SemaphoreType.DMA((2,))],
        compiler_params=_cparams("parallel", "parallel", "arbitrary"),
        name="sb_prompt",
    )(q, *([kt] * n_span), *([vt] * n_span), kt, vt, u)


def _sb_sample_kernel(q_ref, kn_ref, vn_ref, kt_ref, vt_ref, kp_hbm, vp_hbm, u_ref, o_ref,
                      kbuf, vbuf, sem, *, head_dim, past):
    b, hp = pl.program_id(0), pl.program_id(1)
    tq = q_ref.shape[0]
    tail = kt_ref.shape[1]
    qs = _sb_stack_heads(q_ref[...], head_dim)
    rows = qs.shape[0]
    pad = jnp.zeros((SB_QBLOCK - tq, LANES), BF16)
    kn = jnp.concatenate([kn_ref[...], pad], axis=0)
    vn = jnp.concatenate([vn_ref[...], pad], axis=0)
    z = jnp.concatenate([jnp.dot(qs, kt_ref[...].astype(BF16), preferred_element_type=F32),
                         _dot_t(qs, kn)], axis=1)
    span = tail + SB_QBLOCK
    row = lax.broadcasted_iota(jnp.int32, (rows, span), 0)
    col = lax.broadcasted_iota(jnp.int32, (rows, span), 1)
    valid = col < tail + (row & (tq - 1))
    w, carry = _sb_weights(z, valid, u_ref, None)
    acc = (_dot_t(w[:, :tail], vt_ref[...].astype(BF16))
           + jnp.dot(w[:, tail:], vn, preferred_element_type=F32))

    def cond(state):
        j, _, carry = state
        return (j >= 0) & (jnp.max(carry) > -SB_UNDERFLOW)

    def body(state):
        j, acc, carry = state
        src = (b, pl.ds(hp * LANES, LANES), pl.ds(j * SB_QBLOCK, SB_QBLOCK))
        kc = pltpu.make_async_copy(kp_hbm.at[src], kbuf, sem.at[0])
        vc = pltpu.make_async_copy(vp_hbm.at[src], vbuf, sem.at[1])
        kc.start()
        vc.start()
        kc.wait()
        vc.wait()
        w, tot = _sb_weights(jnp.dot(qs, kbuf[...].astype(BF16), preferred_element_type=F32),
                             None, u_ref, carry)
        return j - 1, acc + _dot_t(w, vbuf[...].astype(BF16)), carry + tot

    _, acc, _ = lax.while_loop(cond, body, ((past - tail) // SB_QBLOCK - 1, acc, carry))
    o_ref[...] = _sb_unstack_heads(acc, tq, head_dim).astype(o_ref.dtype)


def _sb_sample(q, kn, vn, kpt, vpt, batch, tq, head_dim):
    n, d = q.shape
    past = kpt.shape[2]
    tail = min(SB_SPAN_BLOCKS - 1, past // SB_QBLOCK) * SB_QBLOCK
    u = _sb_later_selector(tail + SB_QBLOCK)
    new = pl.BlockSpec((tq, LANES), lambda b, hp: (b, hp))
    tail_spec = pl.BlockSpec((None, LANES, tail), lambda b, hp: (b, hp, past // tail - 1))
    return pl.pallas_call(
        functools.partial(_sb_sample_kernel, head_dim=head_dim, past=past),
        grid=(batch, d // LANES),
        in_specs=[new, new, new, tail_spec, tail_spec,
                  pl.BlockSpec(memory_space=pl.ANY), pl.BlockSpec(memory_space=pl.ANY),
                  pl.BlockSpec(u.shape, lambda b, hp: (0, 0))],
        out_specs=new,
        out_shape=jax.ShapeDtypeStruct((n, d), BF16),
        scratch_shapes=[pltpu.VMEM((LANES, SB_QBLOCK), F32), pltpu.VMEM((LANES, SB_QBLOCK), F32),
                        pltpu.SemaphoreType.DMA((2,))],
        compiler_params=_cparams("parallel", "parallel"),
        name="sb_sample",
    )(q, kn, vn, kpt, vpt, kpt, vpt, u)


def _gelu_tanh(x):
    c = math.sqrt(2.0 / math.pi)
    return 0.5 * x * (1.0 + jnp.tanh(c * (x + 0.044715 * (x * x * x))))


def _sgu_kernel(x_ref, win_ref, lng_ref, lnb_ref, ws_ref, bs_ref, wout_ref, g_ref, b_ref,
                o_ref, *v_out, alpha, width, groups):
    x = x_ref[...]
    z = _gelu_tanh(jnp.dot(x.astype(BF16), win_ref[...], preferred_element_type=F32))
    u = z[:, :width]
    v = _layer_norm(z[:, width:], lng_ref[...], lnb_ref[...])
    if v_out:
        v_out[0][...] = v
    tm = x.shape[0]
    gd = width // groups
    vb = v.astype(BF16)
    rows = []
    for c in range(tm // SGU_LEN):
        r0 = c * SGU_LEN
        cols = []
        for g in range(groups):
            mixed = jnp.dot(ws_ref[g], vb[r0:r0 + SGU_LEN, g * gd:(g + 1) * gd],
                            preferred_element_type=F32) + bs_ref[g]
            cols.append(mixed)
        rows.append(jnp.concatenate(cols, axis=1))
    mixed = rows[0] if len(rows) == 1 else jnp.concatenate(rows, axis=0)
    y = jnp.dot((u * mixed).astype(BF16), wout_ref[...], preferred_element_type=F32)
    o_ref[...] = _layer_norm(alpha * x + y, g_ref[...], b_ref[...])


def _sgu(x, w_in, ln_g, ln_b, ws_mix, bs_mix, w_out, g, b, alpha, want_v):
    n, d = x.shape
    width = w_out.shape[0]
    groups = ws_mix.shape[0]
    tm = _tile(n, 256)
    const2 = lambda i: (0, 0)
    const3 = lambda i: (0, 0, 0)
    out_specs = [pl.BlockSpec((tm, d), lambda i: (i, 0))]
    out_shape = [jax.ShapeDtypeStruct((n, d), F32)]
    if want_v:
        out_specs.append(pl.BlockSpec((tm, width), lambda i: (i, 0)))
        out_shape.append(jax.ShapeDtypeStruct((n, width), F32))
    res = pl.pallas_call(
        functools.partial(_sgu_kernel, alpha=alpha, width=width, groups=groups),
        grid=(n // tm,),
        in_specs=[pl.BlockSpec((tm, d), lambda i: (i, 0)),
                  pl.BlockSpec(w_in.shape, const2),
                  pl.BlockSpec((1, width), const2),
                  pl.BlockSpec((1, width), const2),
                  pl.BlockSpec(ws_mix.shape, const3),
                  pl.BlockSpec(bs_mix.shape, const3),
                  pl.BlockSpec(w_out.shape, const2),
                  pl.BlockSpec((1, d), const2),
                  pl.BlockSpec((1, d), const2)],
        out_specs=out_specs,
        out_shape=out_shape,
        compiler_params=_cparams("parallel"),
        name="sgu",
    )(x, w_in, ln_g, ln_b, ws_mix, bs_mix, w_out, g, b)
    return res[0], (res[1] if want_v else None)


def _gla_proj_kernel(x_ref, wq_ref, wk_ref, wv_ref, wr_ref, wg1_ref, wg2_ref, bg_ref,
                     q_ref, k_ref, v_ref, r_ref, la_ref, *, qscale):
    xb = x_ref[...].astype(BF16)
    q_ref[...] = jnp.dot(xb, wq_ref[...], preferred_element_type=F32) * qscale
    k_ref[...] = jnp.dot(xb, wk_ref[...], preferred_element_type=F32)
    v_ref[...] = jnp.dot(xb, wv_ref[...], preferred_element_type=F32).astype(BF16)
    r_ref[...] = jnp.dot(xb, wr_ref[...], preferred_element_type=F32)
    t = jnp.dot(xb, wg1_ref[...], preferred_element_type=F32)
    gate = jnp.dot(t.astype(BF16), wg2_ref[...], preferred_element_type=F32) + bg_ref[...]
    la_ref[...] = _log_sigmoid(gate) / GLA_TAU


def _gla_proj(x, wq, wk, wv, wr, wg1, wg2, bg, qscale):
    n, d = x.shape
    dq, dv = wq.shape[1], wv.shape[1]
    tm = _tile(n, 512)
    const = lambda i: (0, 0)
    row = lambda i: (i, 0)
    return pl.pallas_call(
        functools.partial(_gla_proj_kernel, qscale=qscale),
        grid=(n // tm,),
        in_specs=[pl.BlockSpec((tm, d), row)]
        + [pl.BlockSpec(w.shape, const) for w in (wq, wk, wv, wr, wg1, wg2, bg)],
        out_specs=[pl.BlockSpec((tm, dq), row), pl.BlockSpec((tm, dq), row),
                   pl.BlockSpec((tm, dv), row), pl.BlockSpec((tm, dv), row),
                   pl.BlockSpec((tm, dq), row)],
        out_shape=[jax.ShapeDtypeStruct((n, dq), F32), jax.ShapeDtypeStruct((n, dq), F32),
                   jax.ShapeDtypeStruct((n, dv), BF16), jax.ShapeDtypeStruct((n, dv), F32),
                   jax.ShapeDtypeStruct((n, dq), F32)],
        compiler_params=_cparams("parallel"),
        name="gla_proj",
    )(x, wq, wk, wv, wr, wg1, wg2, bg)


def _gla_scan_kernel(q_ref, k_ref, v_ref, la_ref, s0_ref, o_ref, sfin_ref, st_ref,
                     *, heads, dk, dv):
    ti = pl.program_id(1)
    tt = q_ref.shape[0]
    nsb = tt // GLA_BLOCK

    @pl.when(ti == 0)
    def _():
        for h in range(heads):
            st_ref[h] = s0_ref[0, h].T

    row = lax.broadcasted_iota(jnp.int32, (tt, tt), 0)
    col = lax.broadcasted_iota(jnp.int32, (tt, tt), 1)
    same = (row // GLA_BLOCK) == (col // GLA_BLOCK)
    tril = same & (col <= row)
    bdcat = jnp.concatenate([jnp.where(tril, 1.0, 0.0), jnp.where(same, 1.0, 0.0)],
                            axis=0).astype(BF16)
    rsub = lax.broadcasted_iota(jnp.int32, (tt, dk), 0) // GLA_BLOCK

    for h in range(heads):
        ks = slice(h * dk, (h + 1) * dk)
        vs = slice(h * dv, (h + 1) * dv)
        hi, lo = _split_bf16(la_ref[:, ks])
        ct = (jnp.dot(bdcat, hi, preferred_element_type=F32)
              + jnp.dot(bdcat, lo, preferred_element_type=F32))
        cum, tot = ct[:tt], ct[tt:]
        k = k_ref[:, ks]
        qd = (q_ref[:, ks] * jnp.exp(cum)).astype(BF16)
        kd = (k * jnp.exp(-cum)).astype(BF16)
        kt = k * jnp.exp(tot - cum)
        sc = lax.dot_general(qd, kd, (((1,), (1,)), ((), ())), preferred_element_type=F32)
        sc = jnp.where(tril, sc, 0.0).astype(BF16)
        vh = v_ref[:, vs]
        o_intra = jnp.dot(sc, vh, preferred_element_type=F32)
        ktcat = jnp.concatenate(
            [jnp.where(rsub == i, kt, 0.0).astype(BF16) for i in range(nsb)], axis=1)
        ut = lax.dot_general(vh, ktcat, (((0,), (0,)), ((), ())), preferred_element_type=F32)
        st = st_ref[h]
        parts = []
        for i in range(nsb):
            r0 = i * GLA_BLOCK
            parts.append(lax.dot_general(qd[r0:r0 + GLA_BLOCK], st.astype(BF16),
                                         (((1,), (1,)), ((), ())), preferred_element_type=F32))
            st = st * jnp.exp(tot[r0:r0 + 1, :]) + ut[:, i * dk:(i + 1) * dk]
        st_ref[h] = st
        o_ref[:, vs] = o_intra + jnp.concatenate(parts, axis=0)

    @pl.when(ti == pl.num_programs(1) - 1)
    def _():
        for h in range(heads):
            sfin_ref[0, h] = st_ref[h].T


def _gla_scan(q, k, v, la, s0, batch, seq, heads):
    n, dq = q.shape
    dvt = v.shape[1]
    dk, dv = dq // heads, dvt // heads
    tt = SGU_LEN
    nt = seq // tt
    row = lambda b, t: (b * nt + t, 0)
    st = lambda b, t: (b, 0, 0, 0)
    return pl.pallas_call(
        functools.partial(_gla_scan_kernel, heads=heads, dk=dk, dv=dv),
        grid=(batch, nt),
        in_specs=[pl.BlockSpec((tt, dq), row), pl.BlockSpec((tt, dq), row),
                  pl.BlockSpec((tt, dvt), row), pl.BlockSpec((tt, dq), row),
                  pl.BlockSpec((1, heads, dk, dv), st)],
        out_specs=[pl.BlockSpec((tt, dvt), row), pl.BlockSpec((1, heads, dk, dv), st)],
        out_shape=[jax.ShapeDtypeStruct((n, dvt), F32),
                   jax.ShapeDtypeStruct((batch, heads, dk, dv), F32)],
        scratch_shapes=[pltpu.VMEM((heads, dv, dk), F32)],
        compiler_params=_cparams("parallel", "arbitrary"),
        name="gla_scan",
    )(q, k, v, la, s0)


def _gla_out_kernel(o_ref, r_ref, ng_ref, wo_ref, x_ref, g_ref, b_ref, y_ref, *, alpha, heads):
    o = o_ref[...]
    dv = o.shape[1] // heads
    normed = []
    for h in range(heads):
        oh = o[:, h * dv:(h + 1) * dv]
        ms = jnp.mean(oh * oh, axis=-1, keepdims=True)
        normed.append(oh * lax.rsqrt(ms + LN_EPS) * ng_ref[...])
    gated = jnp.concatenate(normed, axis=1) * _silu(r_ref[...])
    y = jnp.dot(gated.astype(BF16), wo_ref[...], preferred_element_type=F32)
    y_ref[...] = _layer_norm(alpha * x_ref[...] + y, g_ref[...], b_ref[...])


def _gla_out(o, r, norm_g, w_o, x, g, b, alpha, heads):
    n, dvt = o.shape
    d = x.shape[1]
    tm = _tile(n, 512)
    row = lambda i: (i, 0)
    const = lambda i: (0, 0)
    return pl.pallas_call(
        functools.partial(_gla_out_kernel, alpha=alpha, heads=heads),
        grid=(n // tm,),
        in_specs=[pl.BlockSpec((tm, dvt), row), pl.BlockSpec((tm, dvt), row),
                  pl.BlockSpec(norm_g.shape, const), pl.BlockSpec(w_o.shape, const),
                  pl.BlockSpec((tm, d), row), pl.BlockSpec((1, d), const),
                  pl.BlockSpec((1, d), const)],
        out_specs=pl.BlockSpec((tm, d), row),
        out_shape=jax.ShapeDtypeStruct((n, d), F32),
        compiler_params=_cparams("parallel"),
        name="gla_out",
    )(o, r, norm_g, w_o, x, g, b)


def _pool_kernel(x_ref, prev_ref, hist_ref, w_ref, sc_ref, g_ref, b_ref, o_ref, buf_ref,
                 *, alpha, n_valid):
    ti = pl.program_id(1)
    tm, d = x_ref.shape
    x = x_ref[...]

    @pl.when(ti == 0)
    def _():
        buf_ref[0:POOL_PAD, :] = hist_ref[0]

    @pl.when(ti > 0)
    def _():
        buf_ref[0:POOL_PAD, :] = prev_ref[...]

    buf_ref[POOL_PAD:POOL_PAD + tm, :] = x
    gd = d // len(POOL_WINDOWS)
    pos = ti * tm + lax.broadcasted_iota(jnp.int32, (tm, 1), 0)
    ys = []
    for g, win in enumerate(POOL_WINDOWS):
        cs = slice(g * gd, (g + 1) * gd)
        acc = x[:, cs]
        for j in range(1, win):
            acc = acc + buf_ref[POOL_PAD - j:POOL_PAD - j + tm, cs]
        count = jnp.minimum(pos + 1 + n_valid, win).astype(F32)
        dlt = acc / count - x[:, cs]
        ys.append(jnp.dot(dlt.astype(BF16), w_ref[g], preferred_element_type=F32))
    y = jnp.concatenate(ys, axis=1) * sc_ref[...]
    o_ref[...] = _layer_norm(alpha * x + y, g_ref[...], b_ref[...])


def _pool(x, hist, w_pool, scale, g, b, alpha, batch, seq, n_valid):
    n, d = x.shape
    tm = _tile(seq, 256)
    nt = seq // tm
    ratio = tm // POOL_PAD
    const2 = lambda bb, t: (0, 0)
    row = lambda bb, t: (bb * nt + t, 0)
    return pl.pallas_call(
        functools.partial(_pool_kernel, alpha=alpha, n_valid=n_valid),
        grid=(batch, nt),
        in_specs=[pl.BlockSpec((tm, d), row),
                  pl.BlockSpec((POOL_PAD, d),
                               lambda bb, t: (jnp.maximum((bb * nt + t) * ratio - 1, 0), 0)),
                  pl.BlockSpec((1, POOL_PAD, d), lambda bb, t: (bb, 0, 0)),
                  pl.BlockSpec(w_pool.shape, lambda bb, t: (0, 0, 0)),
                  pl.BlockSpec((1, d), const2), pl.BlockSpec((1, d), const2),
                  pl.BlockSpec((1, d), const2)],
        out_specs=pl.BlockSpec((tm, d), row),
        out_shape=jax.ShapeDtypeStruct((n, d), F32),
        scratch_shapes=[pltpu.VMEM((POOL_PAD + tm, d), F32)],
        compiler_params=_cparams("parallel", "arbitrary"),
        name="pool",
    )(x, x, hist, w_pool, scale, g, b)


def _moe_route_math(x, wr_hi, wr_lo, br, n_exp, n_grp):
    hi, lo = _split_bf16(x)
    logit = (jnp.dot(hi, wr_hi, preferred_element_type=F32)
             + jnp.dot(lo, wr_hi, preferred_element_type=F32)
             + jnp.dot(hi, wr_lo, preferred_element_type=F32)) + br
    lane_i = lax.broadcasted_iota(jnp.int32, logit.shape, 1)
    lane = lane_i.astype(F32)
    neg = jnp.float32(-jnp.inf)
    big = jnp.float32(LANES)
    is_grp = (lane_i >= n_exp) & (lane_i < n_exp + n_grp)
    gl = jnp.where(is_grp, logit, neg)
    gmax = jnp.max(gl, axis=-1, keepdims=True)
    g_sel = jnp.min(jnp.where(gl == gmax, lane, big), axis=-1, keepdims=True) - n_exp
    g_prob = 1.0 / jnp.sum(jnp.where(is_grp, jnp.exp(gl - gmax), 0.0), axis=-1, keepdims=True)
    lane_grp = (lane_i // MOE_EXPERTS_PER_GROUP).astype(F32)
    in_grp = (lane_i < n_exp) & (lane_grp == g_sel)
    el = jnp.where(in_grp, logit, neg)
    m1 = jnp.max(el, axis=-1, keepdims=True)
    i1 = jnp.min(jnp.where(el == m1, lane, big), axis=-1, keepdims=True)
    el2 = jnp.where(lane == i1, neg, el)
    m2 = jnp.max(el2, axis=-1, keepdims=True)
    i2 = jnp.min(jnp.where(el2 == m2, lane, big), axis=-1, keepdims=True)
    e2 = jnp.exp(m2 - m1)
    w1 = g_prob / (1.0 + e2)
    w2 = g_prob * e2 / (1.0 + e2)
    return i1, i2, w1, w2


_META_E1, _META_E2, _META_R1, _META_R2, _META_W1, _META_W2 = range(6)


def _moe_route_kernel(x_ref, wrh_ref, wrl_ref, br_ref, meta_ref, cnt_ref, *, n_exp, n_grp):
    i1, i2, w1, w2 = _moe_route_math(x_ref[...], wrh_ref[...], wrl_ref[...], br_ref[...], n_exp, n_grp)
    tm = x_ref.shape[0]
    lane = lax.broadcasted_iota(jnp.int32, (tm, LANES), 1).astype(F32)
    m1 = jnp.where(lane == i1, 1.0, 0.0)
    m2 = jnp.where(lane == i2, 1.0, 0.0)
    both = m1 + m2
    r = lax.broadcasted_iota(jnp.int32, (tm, tm), 0)
    c = lax.broadcasted_iota(jnp.int32, (tm, tm), 1)
    before = jnp.where(c < r, 1.0, 0.0).astype(BF16)
    rank = jnp.dot(before, both.astype(BF16), preferred_element_type=F32)
    r1 = jnp.sum(m1 * rank, axis=-1, keepdims=True)
    r2 = jnp.sum(m2 * rank, axis=-1, keepdims=True)
    meta = jnp.zeros((tm, LANES), F32)
    for idx, col in ((_META_E1, i1), (_META_E2, i2), (_META_R1, r1), (_META_R2, r2),
                     (_META_W1, w1), (_META_W2, w2)):
        meta = jnp.where(lane == idx, col, meta)
    meta_ref[...] = meta
    cnt_ref[0] = jnp.broadcast_to(jnp.sum(both, axis=0, keepdims=True), cnt_ref.shape[1:])


def _moe_route(x, wr_hi, wr_lo, br, n_exp, n_grp):
    n, d = x.shape
    tm = MOE_TILE
    const = lambda i: (0, 0)
    return pl.pallas_call(
        functools.partial(_moe_route_kernel, n_exp=n_exp, n_grp=n_grp),
        grid=(n // tm,),
        in_specs=[pl.BlockSpec((tm, d), lambda i: (i, 0)),
                  pl.BlockSpec(wr_hi.shape, const), pl.BlockSpec(wr_lo.shape, const),
                  pl.BlockSpec((1, LANES), const)],
        out_specs=[pl.BlockSpec((tm, LANES), lambda i: (i, 0)),
                   pl.BlockSpec((1, 8, LANES), lambda i: (i, 0, 0))],
        out_shape=[jax.ShapeDtypeStruct((n, LANES), F32),
                   jax.ShapeDtypeStruct((n // tm, 8, LANES), F32)],
        compiler_params=_cparams("parallel"),
        name="moe_route",
    )(x, wr_hi, wr_lo, br)


def _moe_placement(meta, base_row, n_stage, weighted):
    tm = meta.shape[0]
    lane = lax.broadcasted_iota(jnp.int32, (tm, LANES), 1).astype(F32)
    col = lax.broadcasted_iota(jnp.int32, (tm, n_stage), 1).astype(F32)
    out = None
    for e_idx, r_idx, w_idx in ((_META_E1, _META_R1, _META_W1), (_META_E2, _META_R2, _META_W2)):
        e = meta[:, e_idx:e_idx + 1]
        pos = jnp.sum(jnp.where(lane == e, base_row, 0.0), axis=-1, keepdims=True) + meta[:, r_idx:r_idx + 1]
        val = meta[:, w_idx:w_idx + 1] if weighted else 1.0
        term = jnp.where(col == pos, val, 0.0)
        out = term if out is None else out + term
    return out.astype(BF16)


def _moe_segment_copies(tile, n_exp, seg_start, seg_len, stage_base, copy_rows):
    for e in range(n_exp):
        idx = tile * n_exp + e
        src, dst, n16 = stage_base[idx], seg_start[idx], seg_len[idx] // MOE_ALIGN

        def piece(c, carry, src=src, dst=dst):
            copy_rows(pl.multiple_of(dst + c * MOE_ALIGN, MOE_ALIGN),
                      pl.multiple_of(src + c * MOE_ALIGN, MOE_ALIGN))
            return carry

        lax.fori_loop(0, n16, piece, 0)


def _moe_experts_kernel(seg_start, seg_len, stage_base, off,
                        x_ref, meta_ref, base_ref, wup_ref, wdn_ref, g_ref, b_ref, o_ref,
                        buf_ref, stg_ref, *, alpha, n_exp, tiles_per_unit, hidden):
    u, s = pl.program_id(0), pl.program_id(1)
    n_stage = stg_ref.shape[0]

    @pl.when((u == 0) & (s == 0))
    def _():
        buf_ref[...] = jnp.zeros_like(buf_ref)

    @pl.when(s < tiles_per_unit)
    def _():
        place = _moe_placement(meta_ref[...], base_ref[0, 0:1, :], n_stage, weighted=False)
        xb = x_ref[...].astype(BF16)
        for c0 in range(0, n_stage, 2 * LANES):
            rows = lax.dot_general(place[:, c0:c0 + 2 * LANES], xb, (((0,), (0,)), ((), ())),
                                   preferred_element_type=F32)
            stg_ref[c0:c0 + 2 * LANES, :] = rows.astype(BF16)

        def copy_rows(unit_row, stage_row):
            buf_ref[pl.ds(unit_row, MOE_ALIGN), :] = stg_ref[pl.ds(stage_row, MOE_ALIGN), :]

        _moe_segment_copies(u * tiles_per_unit + s, n_exp, seg_start, seg_len, stage_base, copy_rows)

    @pl.when((s >= tiles_per_unit) & (s < tiles_per_unit + n_exp))
    def _():
        e = s - tiles_per_unit
        start = off[u * (n_exp + 1) + e]
        n = off[u * (n_exp + 1) + e + 1] - start

        def ffn_rows(r0, size, n_valid):
            r0 = pl.multiple_of(r0, MOE_ALIGN)
            xs = buf_ref[pl.ds(r0, size), :]
            h = jnp.dot(xs, wup_ref[0], preferred_element_type=F32)
            act = _silu(h[:, :hidden]) * h[:, hidden:]
            y = jnp.dot(act.astype(BF16), wdn_ref[0], preferred_element_type=F32)
            if n_valid is not None:
                row = lax.broadcasted_iota(jnp.int32, y.shape, 0)
                y = jnp.where(row < n_valid, y, xs.astype(F32))
            buf_ref[pl.ds(r0, size), :] = y.astype(BF16)

        n_full = n // MOE_FFN_ROWS

        def full(c, carry):
            ffn_rows(start + c * MOE_FFN_ROWS, MOE_FFN_ROWS, None)
            return carry

        lax.fori_loop(0, n_full, full, 0)
        rem = n - n_full * MOE_FFN_ROWS
        tail = start + n_full * MOE_FFN_ROWS
        for size in range(LANES, MOE_FFN_ROWS + 1, LANES):
            @pl.when((rem > size - LANES) & (rem <= size))
            def _(size=size):
                ffn_rows(tail, size, rem)

    @pl.when(s >= tiles_per_unit + n_exp)
    def _():
        t = s - tiles_per_unit - n_exp

        def copy_rows(unit_row, stage_row):
            stg_ref[pl.ds(stage_row, MOE_ALIGN), :] = buf_ref[pl.ds(unit_row, MOE_ALIGN), :]

        _moe_segment_copies(u * tiles_per_unit + t, n_exp, seg_start, seg_len, stage_base, copy_rows)
        place = _moe_placement(meta_ref[...], base_ref[0, 0:1, :], n_stage, weighted=True)
        f = jnp.dot(place, stg_ref[...], preferred_element_type=F32)
        o_ref[...] = _layer_norm(alpha * x_ref[...] + f, g_ref[...], b_ref[...])


def _moe(x, wr_hi, wr_lo, br, w_up, w_dn, g, b, alpha, n_grp):
    n, d = x.shape
    n_exp, _, two_f = w_up.shape
    hidden = two_f // 2
    tm = MOE_TILE
    unit = min(MOE_UNIT, n)
    tpu = unit // tm
    n_tiles, n_units = n // tm, n // unit
    worst_pad = MOE_ALIGN - 1
    n_stage = -(-(2 * tm + n_exp * worst_pad) // (2 * LANES)) * (2 * LANES)
    cap = -(-(2 * unit + n_exp * tpu * worst_pad) // MOE_FFN_ROWS) * MOE_FFN_ROWS + MOE_FFN_ROWS

    meta, cnt = _moe_route(x, wr_hi, wr_lo, br, n_exp, n_grp)
    cnt = cnt[:, 0, :n_exp].astype(jnp.int32)
    seg_len = (cnt + worst_pad) // MOE_ALIGN * MOE_ALIGN
    stage_base = jnp.cumsum(seg_len, axis=1) - seg_len
    sl = seg_len.reshape(n_units, tpu, n_exp)
    unit_len = jnp.sum(sl, axis=1)
    off = jnp.concatenate([jnp.zeros((n_units, 1), jnp.int32), jnp.cumsum(unit_len, axis=1)], axis=1)
    seg_start = off[:, None, :n_exp] + jnp.cumsum(sl, axis=1) - sl
    base_rows = jnp.zeros((n_tiles, 8, LANES), F32).at[:, 0, :n_exp].set(stage_base.astype(F32))
    tables = (seg_start.reshape(-1), seg_len.reshape(-1), stage_base.reshape(-1), off.reshape(-1))

    def tile_of(u, s):
        t = jnp.where(s < tpu, s, jnp.where(s >= tpu + n_exp, s - tpu - n_exp, tpu - 1))
        return u * tpu + t

    tile_row = lambda u, s, *_: (tile_of(u, s), 0)
    tile_row3 = lambda u, s, *_: (tile_of(u, s), 0, 0)
    expert = lambda u, s, *_: (jnp.clip(s - tpu, 0, n_exp - 1), 0, 0)
    out_row = lambda u, s, *_: (u * tpu + jnp.clip(s - tpu - n_exp, 0, tpu - 1), 0)
    const = lambda u, s, *_: (0, 0)

    return pl.pallas_call(
        functools.partial(_moe_experts_kernel, alpha=alpha, n_exp=n_exp, tiles_per_unit=tpu,
                          hidden=hidden),
        grid_spec=pltpu.PrefetchScalarGridSpec(
            num_scalar_prefetch=4, grid=(n_units, 2 * tpu + n_exp),
            in_specs=[pl.BlockSpec((tm, d), tile_row), pl.BlockSpec((tm, LANES), tile_row),
                      pl.BlockSpec((1, 8, LANES), tile_row3),
                      pl.BlockSpec((1, d, two_f), expert), pl.BlockSpec((1, hidden, d), expert),
                      pl.BlockSpec((1, d), const), pl.BlockSpec((1, d), const)],
            out_specs=pl.BlockSpec((tm, d), out_row),
            scratch_shapes=[pltpu.VMEM((cap, d), BF16), pltpu.VMEM((n_stage, d), BF16)]),
        out_shape=jax.ShapeDtypeStruct((n, d), F32),
        compiler_params=_cparams("arbitrary", "arbitrary"),
        name="moe_experts",
    )(*tables, x, meta, base_rows, w_up, w_dn, g, b)


def _router_params(w_group, b_group, w_router, b_router):
    d, n_grp = w_group.shape
    n_exp = w_router.shape[1]
    w = jnp.zeros((d, LANES), F32).at[:, :n_exp].set(w_router).at[:, n_exp:n_exp + n_grp].set(w_group)
    hi = w.astype(BF16)
    lo = (w - hi.astype(F32)).astype(BF16)
    br = jnp.zeros((1, LANES), F32).at[0, :n_exp].set(b_router).at[0, n_exp:n_exp + n_grp].set(b_group)
    return hi, lo, br


def _sgu_mix_params(w_s, b_s, t_len, gd):
    groups = w_s.shape[0]
    L = min(t_len, SGU_LEN)
    pos = jnp.arange(L)
    allowed = (pos[None, :] // CHUNK) <= (pos[:, None] // CHUNK)
    w = jnp.where(allowed[None], w_s[:, :L, :L], 0.0)
    rep = SGU_LEN // L
    if rep > 1:
        eye = jnp.eye(rep, dtype=F32)
        w = jnp.einsum("ab,gts->gatbs", eye, w).reshape(groups, SGU_LEN, SGU_LEN)
    bias = jnp.tile(b_s[:, :L], (1, rep))
    bias = jnp.broadcast_to(bias[:, :, None], (groups, SGU_LEN, gd))
    return w.astype(BF16), bias.astype(F32)


def kernel(x_prompt, x_sample, cache_sb_k, cache_sb_v, state_gla, state_pool, sb_w_qkv, sb_w_o, sgu_w_in, sgu_ln_g, sgu_ln_b, sgu_w_s, sgu_b_s, sgu_w_out, gla_w_in, gla_w_g1, gla_w_g2, gla_b_g, gla_norm_g, gla_w_o, pool_w, pool_scale, ln1_g, ln1_b, ln2_g, ln2_b, moe_w_group, moe_b_group, moe_w_router, moe_b_router, moe_w_up, moe_w_down):
    depth = ln1_g.shape[0]
    d = x_prompt.shape[-1]
    alpha = (2.0 * depth) ** 0.25
    sb_heads, sb_hd = cache_sb_k.shape[3], cache_sb_k.shape[4]
    gla_heads, gla_dk, gla_dv = state_gla.shape[2], state_gla.shape[3], state_gla.shape[4]
    n_grp = moe_w_group.shape[-1]
    dq, dvt = gla_heads * gla_dk, gla_heads * gla_dv
    n_hist = state_pool.shape[2]
    sgu_groups = sgu_w_s.shape[1]
    sgu_width = sgu_w_out.shape[1]

    bf = lambda a: a.astype(BF16)
    vec = lambda a: a.reshape(1, -1)

    def run(x3, past):
        batch, seq, _ = x3.shape
        n = batch * seq
        x = x3.reshape(n, d)
        sb_k, sb_v, sgu_v, gla_s, pool_h = [], [], [], [], []
        for i in range(depth):
            m, j = i % 4, i // 4
            g1, b1 = vec(ln1_g[i]), vec(ln1_b[i])
            if m == 0:
                wq, wk, wv = (bf(sb_w_qkv[j][:, c * d:(c + 1) * d]) for c in range(3))
                from_t = lambda a: a.reshape(batch, sb_heads, sb_hd, -1).transpose(0, 3, 1, 2)
                to_t = lambda a: a.transpose(0, 2, 3, 1).reshape(batch, d, -1)
                if past is None:
                    q, kt, vt, ktb, vtb = _sb_qkv(x, wq, wk.T, wv.T, batch, seq, sb_hd ** -0.5)
                    o = _sb_prompt(q, ktb, vtb, batch, seq, sb_hd)
                    sb_k.append(from_t(kt))
                    sb_v.append(from_t(vt))
                else:
                    q, k, v, kb, vb = _proj(
                        x, [wq, wk, wv],
                        [(0, BF16, sb_hd ** -0.5), (1, F32, 1.0), (2, F32, 1.0), (1, BF16, 1.0), (2, BF16, 1.0)],
                        "sb_qkv")
                    o = _sb_sample(q, kb, vb, to_t(past[0][j]), to_t(past[1][j]), batch, seq, sb_hd)
                    sb_k.append(k.reshape(batch, seq, sb_heads, sb_hd))
                    sb_v.append(v.reshape(batch, seq, sb_heads, sb_hd))
                x = _mm_res_ln(o, bf(sb_w_o[j]), x, g1, b1, alpha, "sb_out")
            elif m == 1:
                ws_mix, bs_mix = _sgu_mix_params(sgu_w_s[j], sgu_b_s[j], seq, sgu_width // sgu_groups)
                x, v = _sgu(x, bf(sgu_w_in[j]), vec(sgu_ln_g[j]), vec(sgu_ln_b[j]), ws_mix, bs_mix,
                            bf(sgu_w_out[j]), g1, b1, alpha, past is not None)
                if v is not None:
                    sgu_v.append(v.reshape(batch, seq, sgu_width))
            elif m == 2:
                w_in = gla_w_in[j]
                wg1 = jnp.zeros((d, LANES), F32).at[:, :gla_w_g1.shape[-1]].set(gla_w_g1[j])
                wg2 = jnp.zeros((LANES, dq), F32).at[:gla_w_g2.shape[1]].set(gla_w_g2[j])
                q, k, v, r, la = _gla_proj(
                    x, bf(w_in[:, :dq]), bf(w_in[:, dq:2 * dq]), bf(w_in[:, 2 * dq:2 * dq + dvt]),
                    bf(w_in[:, 2 * dq + dvt:]), bf(wg1), bf(wg2), vec(gla_b_g[j]), gla_dk ** -0.5)
                if past is None:
                    s0 = jnp.zeros((batch, gla_heads, gla_dk, gla_dv), F32)
                    seq_p = seq
                else:
                    s0 = past[2][j]
                    seq_p = -(-seq // SGU_LEN) * SGU_LEN
                    padrows = lambda a: jnp.pad(a.reshape(batch, seq, -1),
                                                ((0, 0), (0, seq_p - seq), (0, 0))).reshape(batch * seq_p, -1)
                    q, k, v, la = padrows(q), padrows(k), padrows(v), padrows(la)
                o, s = _gla_scan(q, k, v, la, s0, batch, seq_p, gla_heads)
                if seq_p != seq:
                    o = o.reshape(batch, seq_p, dvt)[:, :seq].reshape(n, dvt)
                gla_s.append(s)
                x = _gla_out(o, r, vec(gla_norm_g[j]), bf(gla_w_o[j]), x, g1, b1, alpha, gla_heads)
            else:
                if past is None:
                    hist = jnp.zeros((batch, POOL_PAD, d), F32)
                    n_valid = 0
                else:
                    hist = jnp.pad(past[3][j], ((0, 0), (POOL_PAD - n_hist, 0), (0, 0)))
                    n_valid = n_hist
                x_in = x.reshape(batch, seq, d)
                if past is None:
                    pool_h.append(x_in[:, seq - n_hist:])
                else:
                    pool_h.append(jnp.concatenate([past[3][j], x_in], axis=1)[:, -n_hist:])
                x = _pool(x, hist, bf(pool_w[j]), vec(pool_scale[j]), g1, b1, alpha, batch, seq, n_valid)
            wr_hi, wr_lo, br = _router_params(moe_w_group[i], moe_b_group[i], moe_w_router[i], moe_b_router[i])
            x = _moe(x, wr_hi, wr_lo, br, bf(moe_w_up[i]), bf(moe_w_down[i]),
                     vec(ln2_g[i]), vec(ln2_b[i]), alpha, n_grp)
        return x.reshape(batch, seq, d), sb_k, sb_v, sgu_v, gla_s, pool_h

    y_p, kp, vp, _, sp, hp = run(x_prompt, None)
    y_s, ks, vs, us, ss, hs = run(x_sample, (cache_sb_k, cache_sb_v, state_gla, state_pool))
    return (y_p, y_s, jnp.stack(kp), jnp.stack(vp), jnp.stack(ks), jnp.stack(vs), jnp.stack(us),
            jnp.stack(sp), jnp.stack(ss), jnp.stack(hp), jnp.stack(hs))
```

```python
import functools
import math

import jax
import jax.numpy as jnp
from jax import lax
from jax.experimental import pallas as pl
from jax.experimental.pallas import tpu as pltpu

F32 = jnp.float32
BF16 = jnp.bfloat16

LN_EPS = 1e-5
CHUNK = 64
SGU_LEN = 128
SB_QBLOCK = 128
SB_SPAN_BLOCKS = 3
SB_LANE_GROUPS = 2
SB_UNDERFLOW = 104.0
GLA_BLOCK = 16
GLA_TAU = 16.0
POOL_WINDOWS = (2, 4, 8, 16)
POOL_PAD = 16
MOE_EXPERTS_PER_GROUP = 4
MOE_TILE = 256
MOE_ROUTE_TILES = 2
MOE_STEP_TILES = 2
MOE_UNIT = 4096
MOE_ALIGN = 16
MOE_FFN_ROWS = 512
LANES = 128
VMEM_LIMIT = 56 * 1024 * 1024


def _cparams(*sem):
    return pltpu.CompilerParams(dimension_semantics=sem, vmem_limit_bytes=VMEM_LIMIT)


def _tile(n, pref):
    t = min(n, pref)
    while n % t:
        t -= 8
    return t


def _layer_norm(x, g, b):
    mu = jnp.mean(x, axis=-1, keepdims=True)
    xc = x - mu
    var = jnp.mean(xc * xc, axis=-1, keepdims=True)
    return xc * lax.rsqrt(var + LN_EPS) * g + b


def _split_bf16(x):
    hi = x.astype(BF16)
    lo = (x - hi.astype(F32)).astype(BF16)
    return hi, lo


def _log_sigmoid(z):
    return jnp.minimum(z, 0.0) - jnp.log(1.0 + jnp.exp(-jnp.abs(z)))


def _silu(x):
    return x / (1.0 + jnp.exp(-x))


def _proj_kernel(x_ref, *refs, outs):
    n_w = len(refs) - len(outs)
    xb = x_ref[...].astype(BF16)
    ys = [jnp.dot(xb, w_ref[...], preferred_element_type=F32) for w_ref in refs[:n_w]]
    for (wi, _, scale), o_ref in zip(outs, refs[n_w:]):
        o_ref[...] = (ys[wi] * scale).astype(o_ref.dtype)


def _proj(x, ws, outs, name):
    n, k = x.shape
    tm = _tile(n, 512)
    return pl.pallas_call(
        functools.partial(_proj_kernel, outs=tuple(outs)),
        grid=(n // tm,),
        in_specs=[pl.BlockSpec((tm, k), lambda i: (i, 0))]
        + [pl.BlockSpec(w.shape, lambda i: (0, 0)) for w in ws],
        out_specs=[pl.BlockSpec((tm, ws[wi].shape[1]), lambda i: (i, 0)) for wi, _, _ in outs],
        out_shape=[jax.ShapeDtypeStruct((n, ws[wi].shape[1]), dt) for wi, dt, _ in outs],
        compiler_params=_cparams("parallel"),
        name=name,
    )(x, *ws)


def _mm_res_ln_kernel(a_ref, w_ref, res_ref, g_ref, b_ref, o_ref, *, alpha):
    y = jnp.dot(a_ref[...].astype(BF16), w_ref[...], preferred_element_type=F32)
    o_ref[...] = _layer_norm(alpha * res_ref[...] + y, g_ref[...], b_ref[...])


def _mm_res_ln(a, w, res, g, b, alpha, name):
    n, k = a.shape
    d = w.shape[1]
    tm = _tile(n, 512)
    return pl.pallas_call(
        functools.partial(_mm_res_ln_kernel, alpha=alpha),
        grid=(n // tm,),
        in_specs=[pl.BlockSpec((tm, k), lambda i: (i, 0)),
                  pl.BlockSpec((k, d), lambda i: (0, 0)),
                  pl.BlockSpec((tm, d), lambda i: (i, 0)),
                  pl.BlockSpec((1, d), lambda i: (0, 0)),
                  pl.BlockSpec((1, d), lambda i: (0, 0))],
        out_specs=pl.BlockSpec((tm, d), lambda i: (i, 0)),
        out_shape=jax.ShapeDtypeStruct((n, d), F32),
        compiler_params=_cparams("parallel"),
        name=name,
    )(a, w, res, g, b)


def _sb_later_selector(span):
    s = jnp.arange(span)
    return (s[:, None] > s[None, :]).astype(BF16)


def _sb_stack_heads(q, head_dim):
    lane = lax.broadcasted_iota(jnp.int32, q.shape, 1)
    parts = [jnp.where((lane >= h * head_dim) & (lane < (h + 1) * head_dim), q, jnp.zeros_like(q))
             for h in range(LANES // head_dim)]
    return jnp.concatenate(parts, axis=0)


def _sb_unstack_heads(acc, tq, head_dim):
    lane = lax.broadcasted_iota(jnp.int32, (tq, LANES), 1)
    out = acc[:tq]
    for h in range(1, LANES // head_dim):
        out = jnp.where(lane >= h * head_dim, acc[h * tq:(h + 1) * tq], out)
    return out


def _sb_later_sums(lstay, u_ref, off):
    n = lstay.shape[1]
    hi, lo = _split_bf16(lstay)
    cols = []
    for c0 in range(0, n, 2 * LANES):
        c1 = min(c0 + 2 * LANES, n)
        u = u_ref[off + c0:off + n, off + c0:off + c1]
        cols.append(jnp.dot(hi[:, c0:], u, preferred_element_type=F32)
                    + jnp.dot(lo[:, c0:], u, preferred_element_type=F32))
    return cols[0] if len(cols) == 1 else jnp.concatenate(cols, axis=1)


def _sb_weights(z, valid, u_ref, carry):
    ls = _log_sigmoid(z)
    lstay = ls - z
    if valid is not None:
        lstay = jnp.where(valid, lstay, 0.0)
    later = _sb_later_sums(lstay, u_ref, 0)
    if carry is not None:
        later = later + carry
    w = jnp.exp(ls + later)
    if valid is not None:
        w = jnp.where(valid, w, 0.0)
    return w.astype(BF16), jnp.sum(lstay, axis=-1, keepdims=True)


def _dot_t(a, bt):
    return lax.dot_general(a, bt, (((1,), (1,)), ((), ())), preferred_element_type=F32)


def _sb_qkv_kernel(x_ref, wq_ref, wkt_ref, wvt_ref, q_ref, kt_ref, vt_ref, ktb_ref, vtb_ref, *, qscale):
    xb = x_ref[...].astype(BF16)
    q_ref[...] = (jnp.dot(xb, wq_ref[...], preferred_element_type=F32) * qscale).astype(BF16)
    kt = _dot_t(wkt_ref[...], xb)
    kt_ref[...] = kt
    ktb_ref[...] = kt.astype(BF16)
    vt = _dot_t(wvt_ref[...], xb)
    vt_ref[...] = vt
    vtb_ref[...] = vt.astype(BF16)


def _sb_qkv(x, wq, wkt, wvt, batch, seq, qscale):
    n, d = x.shape
    tm = _tile(seq, 512)
    nt = seq // tm
    row = lambda b, t: (b * nt + t, 0)
    const = lambda b, t: (0, 0)
    col = pl.BlockSpec((None, d, tm), lambda b, t: (b, 0, t))
    return pl.pallas_call(
        functools.partial(_sb_qkv_kernel, qscale=qscale),
        grid=(batch, nt),
        in_specs=[pl.BlockSpec((tm, d), row), pl.BlockSpec((d, d), const),
                  pl.BlockSpec((d, d), const), pl.BlockSpec((d, d), const)],
        out_specs=[pl.BlockSpec((tm, d), row), col, col, col, col],
        out_shape=[jax.ShapeDtypeStruct((n, d), BF16),
                   jax.ShapeDtypeStruct((batch, d, seq), F32), jax.ShapeDtypeStruct((batch, d, seq), F32),
                   jax.ShapeDtypeStruct((batch, d, seq), BF16), jax.ShapeDtypeStruct((batch, d, seq), BF16)],
        compiler_params=_cparams("parallel", "parallel"),
        name="sb_qkv",
    )(x, wq, wkt, wvt)


def _sb_prompt_kernel(q_ref, *refs, head_dim, n_span):
    kt_blocks, vt_blocks = refs[:n_span], refs[n_span:2 * n_span]
    kt_hbm, vt_hbm, u_ref, o_ref, kbuf, vbuf, sem = refs[2 * n_span:]
    b, hp, qi = pl.program_id(0), pl.program_id(1), pl.program_id(2)
    tq = q_ref.shape[0]
    width = q_ref.shape[1]
    rows = tq * (LANES // head_dim)
    span = n_span * SB_QBLOCK
    first = jnp.maximum(qi - (n_span - 1), 0)
    row = lax.broadcasted_iota(jnp.int32, (rows, span), 0)
    col = lax.broadcasted_iota(jnp.int32, (rows, span), 1)
    valid = (first * SB_QBLOCK + col) < (qi * tq + (row & (tq - 1)))
    n_groups = width // LANES
    qss, accs, carries = [], [], []
    for g in range(n_groups):
        ls = slice(g * LANES, (g + 1) * LANES)
        qs = _sb_stack_heads(q_ref[:, ls], head_dim)
        kt = jnp.concatenate([r[ls, :] for r in kt_blocks], axis=1)
        vt = jnp.concatenate([r[ls, :] for r in vt_blocks], axis=1)
        w, tot = _sb_weights(jnp.dot(qs, kt, preferred_element_type=F32), valid, u_ref, None)
        qss.append(qs)
        accs.append(_dot_t(w, vt))
        carries.append(tot)

    def cond(state):
        j, _, carries = state
        worst = functools.reduce(jnp.maximum, [jnp.max(c) for c in carries])
        return (j >= 0) & (worst > -SB_UNDERFLOW)

    def body(state):
        j, accs, carries = state
        copies = []
        for g in range(n_groups):
            src = (b, pl.ds(hp * width + g * LANES, LANES), pl.ds(j * SB_QBLOCK, SB_QBLOCK))
            copies.append(pltpu.make_async_copy(kt_hbm.at[src], kbuf.at[g], sem.at[g, 0]))
            copies.append(pltpu.make_async_copy(vt_hbm.at[src], vbuf.at[g], sem.at[g, 1]))
        for c in copies:
            c.start()
        for c in copies:
            c.wait()
        new_accs, new_carries = [], []
        for g in range(n_groups):
            w, tot = _sb_weights(jnp.dot(qss[g], kbuf[g], preferred_element_type=F32), None, u_ref,
                                 carries[g])
            new_accs.append(accs[g] + _dot_t(w, vbuf[g]))
            new_carries.append(carries[g] + tot)
        return j - 1, tuple(new_accs), tuple(new_carries)

    _, accs, _ = lax.while_loop(cond, body, (first - 1, tuple(accs), tuple(carries)))
    o_ref[...] = jnp.concatenate([_sb_unstack_heads(a, tq, head_dim) for a in accs],
                                 axis=1).astype(o_ref.dtype)


def _sb_prompt(q, kt, vt, batch, seq, head_dim):
    n, d = q.shape
    nqb = seq // SB_QBLOCK
    n_span = min(SB_SPAN_BLOCKS, nqb)
    u = _sb_later_selector(n_span * SB_QBLOCK)
    width = SB_LANE_GROUPS * LANES

    def span_block(i):
        return pl.BlockSpec((None, width, SB_QBLOCK),
                            lambda b, hp, qi: (b, hp, jnp.maximum(qi - (n_span - 1), 0) + i))

    blocks = [span_block(i) for i in range(n_span)]
    q_spec = pl.BlockSpec((SB_QBLOCK, width), lambda b, hp, qi: (b * nqb + qi, hp))
    return pl.pallas_call(
        functools.partial(_sb_prompt_kernel, head_dim=head_dim, n_span=n_span),
        grid=(batch, d // width, nqb),
        in_specs=[q_spec] + blocks + blocks
        + [pl.BlockSpec(memory_space=pl.ANY), pl.BlockSpec(memory_space=pl.ANY),
           pl.BlockSpec(u.shape, lambda b, hp, qi: (0, 0))],
        out_specs=q_spec,
        out_shape=jax.ShapeDtypeStruct((n, d), BF16),
        scratch_shapes=[pltpu.VMEM((SB_LANE_GROUPS, LANES, SB_QBLOCK), BF16),
                        pltpu.VMEM((SB_LANE_GROUPS, LANES, SB_QBLOCK), BF16),
                        pltpu.SemaphoreType.DMA((SB_LANE_GROUPS, 2))],
        compiler_params=_cparams("parallel", "parallel", "arbitrary"),
        name="sb_prompt",
    )(q, *([kt] * n_span), *([vt] * n_span), kt, vt, u)


def _sb_sample_kernel(q_ref, kn_ref, vn_ref, kt_ref, vt_ref, kp_hbm, vp_hbm, u_ref, o_ref,
                      kbuf, vbuf, sem, *, head_dim, past):
    b, hp = pl.program_id(0), pl.program_id(1)
    tq = q_ref.shape[0]
    tail = kt_ref.shape[1]
    qs = _sb_stack_heads(q_ref[...], head_dim)
    rows = qs.shape[0]
    pad = jnp.zeros((SB_QBLOCK - tq, LANES), BF16)
    kn = jnp.concatenate([kn_ref[...], pad], axis=0)
    vn = jnp.concatenate([vn_ref[...], pad], axis=0)
    z = jnp.concatenate([jnp.dot(qs, kt_ref[...].astype(BF16), preferred_element_type=F32),
                         _dot_t(qs, kn)], axis=1)
    span = tail + SB_QBLOCK
    row = lax.broadcasted_iota(jnp.int32, (rows, span), 0)
    col = lax.broadcasted_iota(jnp.int32, (rows, span), 1)
    valid = col < tail + (row & (tq - 1))
    w, carry = _sb_weights(z, valid, u_ref, None)
    acc = (_dot_t(w[:, :tail], vt_ref[...].astype(BF16))
           + jnp.dot(w[:, tail:], vn, preferred_element_type=F32))

    def cond(state):
        j, _, carry = state
        return (j >= 0) & (jnp.max(carry) > -SB_UNDERFLOW)

    def body(state):
        j, acc, carry = state
        src = (b, pl.ds(hp * LANES, LANES), pl.ds(j * SB_QBLOCK, SB_QBLOCK))
        kc = pltpu.make_async_copy(kp_hbm.at[src], kbuf, sem.at[0])
        vc = pltpu.make_async_copy(vp_hbm.at[src], vbuf, sem.at[1])
        kc.start()
        vc.start()
        kc.wait()
        vc.wait()
        w, tot = _sb_weights(jnp.dot(qs, kbuf[...].astype(BF16), preferred_element_type=F32),
                             None, u_ref, carry)
        return j - 1, acc + _dot_t(w, vbuf[...].astype(BF16)), carry + tot

    _, acc, _ = lax.while_loop(cond, body, ((past - tail) // SB_QBLOCK - 1, acc, carry))
    o_ref[...] = _sb_unstack_heads(acc, tq, head_dim).astype(o_ref.dtype)


def _sb_sample(q, kn, vn, kpt, vpt, batch, tq, head_dim):
    n, d = q.shape
    past = kpt.shape[2]
    tail = min(SB_SPAN_BLOCKS - 1, past // SB_QBLOCK) * SB_QBLOCK
    u = _sb_later_selector(tail + SB_QBLOCK)
    new = pl.BlockSpec((tq, LANES), lambda b, hp: (b, hp))
    tail_spec = pl.BlockSpec((None, LANES, tail), lambda b, hp: (b, hp, past // tail - 1))
    return pl.pallas_call(
        functools.partial(_sb_sample_kernel, head_dim=head_dim, past=past),
        grid=(batch, d // LANES),
        in_specs=[new, new, new, tail_spec, tail_spec,
                  pl.BlockSpec(memory_space=pl.ANY), pl.BlockSpec(memory_space=pl.ANY),
                  pl.BlockSpec(u.shape, lambda b, hp: (0, 0))],
        out_specs=new,
        out_shape=jax.ShapeDtypeStruct((n, d), BF16),
        scratch_shapes=[pltpu.VMEM((LANES, SB_QBLOCK), F32), pltpu.VMEM((LANES, SB_QBLOCK), F32),
                        pltpu.SemaphoreType.DMA((2,))],
        compiler_params=_cparams("parallel", "parallel"),
        name="sb_sample",
    )(q, kn, vn, kpt, vpt, kpt, vpt, u)


def _gelu_tanh(x):
    c = math.sqrt(2.0 / math.pi)
    return 0.5 * x * (1.0 + jnp.tanh(c * (x + 0.044715 * (x * x * x))))


def _sgu_kernel(x_ref, win_ref, lng_ref, lnb_ref, ws_ref, bs_ref, wout_ref, g_ref, b_ref,
                o_ref, *v_out, alpha, width, groups):
    x = x_ref[...]
    z = _gelu_tanh(jnp.dot(x.astype(BF16), win_ref[...], preferred_element_type=F32))
    u = z[:, :width]
    v = _layer_norm(z[:, width:], lng_ref[...], lnb_ref[...])
    if v_out:
        v_out[0][...] = v
    tm = x.shape[0]
    gd = width // groups
    vb = v.astype(BF16)
    rows = []
    for c in range(tm // SGU_LEN):
        r0 = c * SGU_LEN
        cols = []
        for g in range(groups):
            mixed = jnp.dot(ws_ref[g], vb[r0:r0 + SGU_LEN, g * gd:(g + 1) * gd],
                            preferred_element_type=F32) + bs_ref[g]
            cols.append(mixed)
        rows.append(jnp.concatenate(cols, axis=1))
    mixed = rows[0] if len(rows) == 1 else jnp.concatenate(rows, axis=0)
    y = jnp.dot((u * mixed).astype(BF16), wout_ref[...], preferred_element_type=F32)
    o_ref[...] = _layer_norm(alpha * x + y, g_ref[...], b_ref[...])


def _sgu(x, w_in, ln_g, ln_b, ws_mix, bs_mix, w_out, g, b, alpha, want_v):
    n, d = x.shape
    width = w_out.shape[0]
    groups = ws_mix.shape[0]
    tm = _tile(n, 256)
    const2 = lambda i: (0, 0)
    const3 = lambda i: (0, 0, 0)
    out_specs = [pl.BlockSpec((tm, d), lambda i: (i, 0))]
    out_shape = [jax.ShapeDtypeStruct((n, d), F32)]
    if want_v:
        out_specs.append(pl.BlockSpec((tm, width), lambda i: (i, 0)))
        out_shape.append(jax.ShapeDtypeStruct((n, width), F32))
    res = pl.pallas_call(
        functools.partial(_sgu_kernel, alpha=alpha, width=width, groups=groups),
        grid=(n // tm,),
        in_specs=[pl.BlockSpec((tm, d), lambda i: (i, 0)),
                  pl.BlockSpec(w_in.shape, const2),
                  pl.BlockSpec((1, width), const2),
                  pl.BlockSpec((1, width), const2),
                  pl.BlockSpec(ws_mix.shape, const3),
                  pl.BlockSpec(bs_mix.shape, const3),
                  pl.BlockSpec(w_out.shape, const2),
                  pl.BlockSpec((1, d), const2),
                  pl.BlockSpec((1, d), const2)],
        out_specs=out_specs,
        out_shape=out_shape,
        compiler_params=_cparams("parallel"),
        name="sgu",
    )(x, w_in, ln_g, ln_b, ws_mix, bs_mix, w_out, g, b)
    return res[0], (res[1] if want_v else None)


def _gla_proj_kernel(x_ref, wq_ref, wk_ref, wv_ref, wr_ref, wg1_ref, wg2_ref, bg_ref,
                     q_ref, k_ref, v_ref, r_ref, la_ref, *, qscale):
    xb = x_ref[...].astype(BF16)
    q_ref[...] = jnp.dot(xb, wq_ref[...], preferred_element_type=F32) * qscale
    k_ref[...] = jnp.dot(xb, wk_ref[...], preferred_element_type=F32)
    v_ref[...] = jnp.dot(xb, wv_ref[...], preferred_element_type=F32).astype(BF16)
    r_ref[...] = jnp.dot(xb, wr_ref[...], preferred_element_type=F32)
    t = jnp.dot(xb, wg1_ref[...], preferred_element_type=F32)
    gate = jnp.dot(t.astype(BF16), wg2_ref[...], preferred_element_type=F32) + bg_ref[...]
    la_ref[...] = _log_sigmoid(gate) / GLA_TAU


def _gla_proj(x, wq, wk, wv, wr, wg1, wg2, bg, qscale):
    n, d = x.shape
    dq, dv = wq.shape[1], wv.shape[1]
    tm = _tile(n, 512)
    const = lambda i: (0, 0)
    row = lambda i: (i, 0)
    return pl.pallas_call(
        functools.partial(_gla_proj_kernel, qscale=qscale),
        grid=(n // tm,),
        in_specs=[pl.BlockSpec((tm, d), row)]
        + [pl.BlockSpec(w.shape, const) for w in (wq, wk, wv, wr, wg1, wg2, bg)],
        out_specs=[pl.BlockSpec((tm, dq), row), pl.BlockSpec((tm, dq), row),
                   pl.BlockSpec((tm, dv), row), pl.BlockSpec((tm, dv), row),
                   pl.BlockSpec((tm, dq), row)],
        out_shape=[jax.ShapeDtypeStruct((n, dq), F32), jax.ShapeDtypeStruct((n, dq), F32),
                   jax.ShapeDtypeStruct((n, dv), BF16), jax.ShapeDtypeStruct((n, dv), F32),
                   jax.ShapeDtypeStruct((n, dq), F32)],
        compiler_params=_cparams("parallel"),
        name="gla_proj",
    )(x, wq, wk, wv, wr, wg1, wg2, bg)


def _gla_scan_kernel(q_ref, k_ref, v_ref, la_ref, s0_ref, o_ref, sfin_ref, st_ref,
                     *, heads, dk, dv):
    ti = pl.program_id(1)
    tt = q_ref.shape[0]
    nsb = tt // GLA_BLOCK

    @pl.when(ti == 0)
    def _():
        for h in range(heads):
            st_ref[h] = s0_ref[0, h].T

    row = lax.broadcasted_iota(jnp.int32, (tt, tt), 0)
    col = lax.broadcasted_iota(jnp.int32, (tt, tt), 1)
    same = (row // GLA_BLOCK) == (col // GLA_BLOCK)
    tril = same & (col <= row)
    bdcat = jnp.concatenate([jnp.where(tril, 1.0, 0.0), jnp.where(same, 1.0, 0.0)],
                            axis=0).astype(BF16)
    rsub = lax.broadcasted_iota(jnp.int32, (tt, dk), 0) // GLA_BLOCK

    for h in range(heads):
        ks = slice(h * dk, (h + 1) * dk)
        vs = slice(h * dv, (h + 1) * dv)
        hi, lo = _split_bf16(la_ref[:, ks])
        ct = (jnp.dot(bdcat, hi, preferred_element_type=F32)
              + jnp.dot(bdcat, lo, preferred_element_type=F32))
        cum, tot = ct[:tt], ct[tt:]
        k = k_ref[:, ks]
        qd = (q_ref[:, ks] * jnp.exp(cum)).astype(BF16)
        kd = (k * jnp.exp(-cum)).astype(BF16)
        kt = k * jnp.exp(tot - cum)
        sc = lax.dot_general(qd, kd, (((1,), (1,)), ((), ())), preferred_element_type=F32)
        sc = jnp.where(tril, sc, 0.0).astype(BF16)
        vh = v_ref[:, vs]
        o_intra = jnp.dot(sc, vh, preferred_element_type=F32)
        ktcat = jnp.concatenate(
            [jnp.where(rsub == i, kt, 0.0).astype(BF16) for i in range(nsb)], axis=1)
        ut = lax.dot_general(vh, ktcat, (((0,), (0,)), ((), ())), preferred_element_type=F32)
        st = st_ref[h]
        parts = []
        for i in range(nsb):
            r0 = i * GLA_BLOCK
            parts.append(lax.dot_general(qd[r0:r0 + GLA_BLOCK], st.astype(BF16),
                                         (((1,), (1,)), ((), ())), preferred_element_type=F32))
            st = st * jnp.exp(tot[r0:r0 + 1, :]) + ut[:, i * dk:(i + 1) * dk]
        st_ref[h] = st
        o_ref[:, vs] = o_intra + jnp.concatenate(parts, axis=0)

    @pl.when(ti == pl.num_programs(1) - 1)
    def _():
        for h in range(heads):
            sfin_ref[0, h] = st_ref[h].T


def _gla_scan(q, k, v, la, s0, batch, seq, heads):
    n, dq = q.shape
    dvt = v.shape[1]
    dk, dv = dq // heads, dvt // heads
    tt = SGU_LEN
    nt = seq // tt
    row = lambda b, t: (b * nt + t, 0)
    st = lambda b, t: (b, 0, 0, 0)
    return pl.pallas_call(
        functools.partial(_gla_scan_kernel, heads=heads, dk=dk, dv=dv),
        grid=(batch, nt),
        in_specs=[pl.BlockSpec((tt, dq), row), pl.BlockSpec((tt, dq), row),
                  pl.BlockSpec((tt, dvt), row), pl.BlockSpec((tt, dq), row),
                  pl.BlockSpec((1, heads, dk, dv), st)],
        out_specs=[pl.BlockSpec((tt, dvt), row), pl.BlockSpec((1, heads, dk, dv), st)],
        out_shape=[jax.ShapeDtypeStruct((n, dvt), F32),
                   jax.ShapeDtypeStruct((batch, heads, dk, dv), F32)],
        scratch_shapes=[pltpu.VMEM((heads, dv, dk), F32)],
        compiler_params=_cparams("parallel", "arbitrary"),
        name="gla_scan",
    )(q, k, v, la, s0)


def _gla_out_kernel(o_ref, r_ref, ng_ref, wo_ref, x_ref, g_ref, b_ref, y_ref, *, alpha, heads):
    o = o_ref[...]
    dv = o.shape[1] // heads
    normed = []
    for h in range(heads):
        oh = o[:, h * dv:(h + 1) * dv]
        ms = jnp.mean(oh * oh, axis=-1, keepdims=True)
        normed.append(oh * lax.rsqrt(ms + LN_EPS) * ng_ref[...])
    gated = jnp.concatenate(normed, axis=1) * _silu(r_ref[...])
    y = jnp.dot(gated.astype(BF16), wo_ref[...], preferred_element_type=F32)
    y_ref[...] = _layer_norm(alpha * x_ref[...] + y, g_ref[...], b_ref[...])


def _gla_out(o, r, norm_g, w_o, x, g, b, alpha, heads):
    n, dvt = o.shape
    d = x.shape[1]
    tm = _tile(n, 512)
    row = lambda i: (i, 0)
    const = lambda i: (0, 0)
    return pl.pallas_call(
        functools.partial(_gla_out_kernel, alpha=alpha, heads=heads),
        grid=(n // tm,),
        in_specs=[pl.BlockSpec((tm, dvt), row), pl.BlockSpec((tm, dvt), row),
                  pl.BlockSpec(norm_g.shape, const), pl.BlockSpec(w_o.shape, const),
                  pl.BlockSpec((tm, d), row), pl.BlockSpec((1, d), const),
                  pl.BlockSpec((1, d), const)],
        out_specs=pl.BlockSpec((tm, d), row),
        out_shape=jax.ShapeDtypeStruct((n, d), F32),
        compiler_params=_cparams("parallel"),
        name="gla_out",
    )(o, r, norm_g, w_o, x, g, b)


def _pool_kernel(x_ref, prev_ref, hist_ref, w_ref, sc_ref, g_ref, b_ref, o_ref, buf_ref,
                 *, alpha, n_valid):
    ti = pl.program_id(1)
    tm, d = x_ref.shape
    x = x_ref[...]

    @pl.when(ti == 0)
    def _():
        buf_ref[0:POOL_PAD, :] = hist_ref[0]

    @pl.when(ti > 0)
    def _():
        buf_ref[0:POOL_PAD, :] = prev_ref[...]

    buf_ref[POOL_PAD:POOL_PAD + tm, :] = x
    gd = d // len(POOL_WINDOWS)
    pos = ti * tm + lax.broadcasted_iota(jnp.int32, (tm, 1), 0)
    ys = []
    for g, win in enumerate(POOL_WINDOWS):
        cs = slice(g * gd, (g + 1) * gd)
        acc = x[:, cs]
        for j in range(1, win):
            acc = acc + buf_ref[POOL_PAD - j:POOL_PAD - j + tm, cs]
        count = jnp.minimum(pos + 1 + n_valid, win).astype(F32)
        dlt = acc / count - x[:, cs]
        ys.append(jnp.dot(dlt.astype(BF16), w_ref[g], preferred_element_type=F32))
    y = jnp.concatenate(ys, axis=1) * sc_ref[...]
    o_ref[...] = _layer_norm(alpha * x + y, g_ref[...], b_ref[...])


def _pool(x, hist, w_pool, scale, g, b, alpha, batch, seq, n_valid):
    n, d = x.shape
    tm = _tile(seq, 256)
    nt = seq // tm
    ratio = tm // POOL_PAD
    const2 = lambda bb, t: (0, 0)
    row = lambda bb, t: (bb * nt + t, 0)
    return pl.pallas_call(
        functools.partial(_pool_kernel, alpha=alpha, n_valid=n_valid),
        grid=(batch, nt),
        in_specs=[pl.BlockSpec((tm, d), row),
                  pl.BlockSpec((POOL_PAD, d),
                               lambda bb, t: (jnp.maximum((bb * nt + t) * ratio - 1, 0), 0)),
                  pl.BlockSpec((1, POOL_PAD, d), lambda bb, t: (bb, 0, 0)),
                  pl.BlockSpec(w_pool.shape, lambda bb, t: (0, 0, 0)),
                  pl.BlockSpec((1, d), const2), pl.BlockSpec((1, d), const2),
                  pl.BlockSpec((1, d), const2)],
        out_specs=pl.BlockSpec((tm, d), row),
        out_shape=jax.ShapeDtypeStruct((n, d), F32),
        scratch_shapes=[pltpu.VMEM((POOL_PAD + tm, d), F32)],
        compiler_params=_cparams("parallel", "arbitrary"),
        name="pool",
    )(x, x, hist, w_pool, scale, g, b)


def _moe_route_math(x, wr_hi, wr_lo, br, n_exp, n_grp):
    hi, lo = _split_bf16(x)
    logit = (jnp.dot(hi, wr_hi, preferred_element_type=F32)
             + jnp.dot(lo, wr_hi, preferred_element_type=F32)
             + jnp.dot(hi, wr_lo, preferred_element_type=F32)) + br
    lane_i = lax.broadcasted_iota(jnp.int32, logit.shape, 1)
    lane = lane_i.astype(F32)
    neg = jnp.float32(-jnp.inf)
    big = jnp.float32(LANES)
    is_grp = (lane_i >= n_exp) & (lane_i < n_exp + n_grp)
    gl = jnp.where(is_grp, logit, neg)
    gmax = jnp.max(gl, axis=-1, keepdims=True)
    g_sel = jnp.min(jnp.where(gl == gmax, lane, big), axis=-1, keepdims=True) - n_exp
    g_prob = 1.0 / jnp.sum(jnp.where(is_grp, jnp.exp(gl - gmax), 0.0), axis=-1, keepdims=True)
    lane_grp = (lane_i // MOE_EXPERTS_PER_GROUP).astype(F32)
    in_grp = (lane_i < n_exp) & (lane_grp == g_sel)
    el = jnp.where(in_grp, logit, neg)
    m1 = jnp.max(el, axis=-1, keepdims=True)
    i1 = jnp.min(jnp.where(el == m1, lane, big), axis=-1, keepdims=True)
    el2 = jnp.where(lane == i1, neg, el)
    m2 = jnp.max(el2, axis=-1, keepdims=True)
    i2 = jnp.min(jnp.where(el2 == m2, lane, big), axis=-1, keepdims=True)
    e2 = jnp.exp(m2 - m1)
    w1 = g_prob / (1.0 + e2)
    w2 = g_prob * e2 / (1.0 + e2)
    return i1, i2, w1, w2


_META_E1, _META_E2, _META_R1, _META_R2, _META_W1, _META_W2 = range(6)


def _moe_route_kernel(x_ref, wrh_ref, wrl_ref, br_ref, meta_ref, cnt_ref, *, n_exp, n_grp):
    tm = MOE_TILE
    lane = lax.broadcasted_iota(jnp.int32, (tm, LANES), 1).astype(F32)
    r = lax.broadcasted_iota(jnp.int32, (tm, tm), 0)
    c = lax.broadcasted_iota(jnp.int32, (tm, tm), 1)
    before = jnp.where(c < r, 1.0, 0.0).astype(BF16)
    metas, counts = [], []
    for t in range(x_ref.shape[0] // tm):
        i1, i2, w1, w2 = _moe_route_math(x_ref[t * tm:(t + 1) * tm, :], wrh_ref[...], wrl_ref[...],
                                         br_ref[...], n_exp, n_grp)
        m1 = jnp.where(lane == i1, 1.0, 0.0)
        m2 = jnp.where(lane == i2, 1.0, 0.0)
        both = m1 + m2
        rank = jnp.dot(before, both.astype(BF16), preferred_element_type=F32)
        r1 = jnp.sum(m1 * rank, axis=-1, keepdims=True)
        r2 = jnp.sum(m2 * rank, axis=-1, keepdims=True)
        meta = jnp.zeros((tm, LANES), F32)
        for idx, col in ((_META_E1, i1), (_META_E2, i2), (_META_R1, r1), (_META_R2, r2),
                         (_META_W1, w1), (_META_W2, w2)):
            meta = jnp.where(lane == idx, col, meta)
        metas.append(meta)
        counts.append(jnp.broadcast_to(jnp.sum(both, axis=0, keepdims=True), cnt_ref.shape[1:]))
    meta_ref[...] = jnp.concatenate(metas, axis=0)
    cnt_ref[...] = jnp.stack(counts, axis=0)


def _moe_route(x, wr_hi, wr_lo, br, n_exp, n_grp):
    n, d = x.shape
    tm = MOE_TILE
    tiles = min(MOE_ROUTE_TILES, n // tm)
    const = lambda i: (0, 0)
    return pl.pallas_call(
        functools.partial(_moe_route_kernel, n_exp=n_exp, n_grp=n_grp),
        grid=(n // (tiles * tm),),
        in_specs=[pl.BlockSpec((tiles * tm, d), lambda i: (i, 0)),
                  pl.BlockSpec(wr_hi.shape, const), pl.BlockSpec(wr_lo.shape, const),
                  pl.BlockSpec((1, LANES), const)],
        out_specs=[pl.BlockSpec((tiles * tm, LANES), lambda i: (i, 0)),
                   pl.BlockSpec((tiles, 8, LANES), lambda i: (i, 0, 0))],
        out_shape=[jax.ShapeDtypeStruct((n, LANES), F32),
                   jax.ShapeDtypeStruct((n // tm, 8, LANES), F32)],
        compiler_params=_cparams("parallel"),
        name="moe_route",
    )(x, wr_hi, wr_lo, br)


def _moe_placement(meta, n_stage, weighted):
    tm = meta.shape[0]
    col = lax.broadcasted_iota(jnp.int32, (tm, n_stage), 1).astype(F32)
    out = None
    for r_idx, w_idx in ((_META_R1, _META_W1), (_META_R2, _META_W2)):
        val = meta[:, w_idx:w_idx + 1] if weighted else 1.0
        term = jnp.where(col == meta[:, r_idx:r_idx + 1], val, 0.0)
        out = term if out is None else out + term
    return out.astype(BF16)


def _moe_segment_copies(tile, n_exp, seg_start, seg_len, stage_base, copy_rows):
    big = 2 * MOE_ALIGN
    for e in range(n_exp):
        idx = tile * n_exp + e
        src, dst, n = stage_base[idx], seg_start[idx], seg_len[idx]

        def piece(c, carry, src=src, dst=dst):
            copy_rows(pl.multiple_of(dst + c * big, MOE_ALIGN), pl.multiple_of(src + c * big, MOE_ALIGN), big)
            return carry

        lax.fori_loop(0, n // big, piece, 0)

        @pl.when(n % big != 0)
        def _(src=src, dst=dst, n=n):
            done = n // big * big
            copy_rows(pl.multiple_of(dst + done, MOE_ALIGN), pl.multiple_of(src + done, MOE_ALIGN), MOE_ALIGN)


def _moe_experts_kernel(seg_start, seg_len, stage_base, off,
                        x_ref, meta_ref, wup_ref, wdn_ref, g_ref, b_ref, o_ref,
                        buf_ref, stg_ref, *, alpha, n_exp, steps_per_unit, hidden):
    u, s = pl.program_id(0), pl.program_id(1)
    tiles, n_stage = stg_ref.shape[0], stg_ref.shape[1]
    tm = x_ref.shape[0] // tiles

    @pl.when((u == 0) & (s == 0))
    def _():
        buf_ref[...] = jnp.zeros_like(buf_ref)

    @pl.when(s < steps_per_unit)
    def _():
        for k in range(tiles):
            place = _moe_placement(meta_ref[k * tm:(k + 1) * tm, :], n_stage, weighted=False)
            xb = x_ref[k * tm:(k + 1) * tm, :].astype(BF16)
            for c0 in range(0, n_stage, 2 * LANES):
                rows = lax.dot_general(place[:, c0:c0 + 2 * LANES], xb, (((0,), (0,)), ((), ())),
                                       preferred_element_type=F32)
                stg_ref[k, c0:c0 + 2 * LANES, :] = rows.astype(BF16)
        for k in range(tiles):
            def copy_rows(unit_row, stage_row, rows, k=k):
                buf_ref[pl.ds(unit_row, rows), :] = stg_ref[k, pl.ds(stage_row, rows), :]

            _moe_segment_copies((u * steps_per_unit + s) * tiles + k, n_exp, seg_start, seg_len,
                                stage_base, copy_rows)

    @pl.when((s >= steps_per_unit) & (s < steps_per_unit + n_exp))
    def _():
        e = s - steps_per_unit
        start = off[u * (n_exp + 1) + e]
        n = off[u * (n_exp + 1) + e + 1] - start

        def ffn_rows(r0, size, n_valid):
            r0 = pl.multiple_of(r0, MOE_ALIGN)
            xs = buf_ref[pl.ds(r0, size), :]
            h = jnp.dot(xs, wup_ref[0], preferred_element_type=F32)
            act = _silu(h[:, :hidden]) * h[:, hidden:]
            y = jnp.dot(act.astype(BF16), wdn_ref[0], preferred_element_type=F32)
            if n_valid is not None:
                row = lax.broadcasted_iota(jnp.int32, y.shape, 0)
                y = jnp.where(row < n_valid, y, xs.astype(F32))
            buf_ref[pl.ds(r0, size), :] = y.astype(BF16)

        n_full = n // MOE_FFN_ROWS

        def full(c, carry):
            ffn_rows(start + c * MOE_FFN_ROWS, MOE_FFN_ROWS, None)
            return carry

        lax.fori_loop(0, n_full, full, 0)
        rem = n - n_full * MOE_FFN_ROWS
        tail = start + n_full * MOE_FFN_ROWS
        for size in range(LANES, MOE_FFN_ROWS + 1, LANES):
            @pl.when((rem > size - LANES) & (rem <= size))
            def _(size=size):
                ffn_rows(tail, size, rem)

    @pl.when(s >= steps_per_unit + n_exp)
    def _():
        t = s - steps_per_unit - n_exp
        for k in range(tiles):
            def copy_rows(unit_row, stage_row, rows, k=k):
                stg_ref[k, pl.ds(stage_row, rows), :] = buf_ref[pl.ds(unit_row, rows), :]

            _moe_segment_copies((u * steps_per_unit + t) * tiles + k, n_exp, seg_start, seg_len,
                                stage_base, copy_rows)
        outs = []
        for k in range(tiles):
            place = _moe_placement(meta_ref[k * tm:(k + 1) * tm, :], n_stage, weighted=True)
            f = jnp.dot(place, stg_ref[k], preferred_element_type=F32)
            outs.append(_layer_norm(alpha * x_ref[k * tm:(k + 1) * tm, :] + f, g_ref[...], b_ref[...]))
        o_ref[...] = jnp.concatenate(outs, axis=0)


def _moe(x, wr_hi, wr_lo, br, w_up, w_dn, g, b, alpha, n_grp):
    n, d = x.shape
    n_exp, _, two_f = w_up.shape
    hidden = two_f // 2
    tm = MOE_TILE
    unit = min(MOE_UNIT, n)
    tpu = unit // tm
    n_tiles, n_units = n // tm, n // unit
    worst_pad = MOE_ALIGN - 1
    n_stage = -(-(2 * tm + n_exp * worst_pad) // (2 * LANES)) * (2 * LANES)
    cap = -(-(2 * unit + n_exp * tpu * worst_pad) // MOE_FFN_ROWS) * MOE_FFN_ROWS + MOE_FFN_ROWS

    meta, cnt = _moe_route(x, wr_hi, wr_lo, br, n_exp, n_grp)
    cnt = cnt[:, 0, :n_exp].astype(jnp.int32)
    seg_len = (cnt + worst_pad) // MOE_ALIGN * MOE_ALIGN
    stage_base = jnp.cumsum(seg_len, axis=1) - seg_len
    sl = seg_len.reshape(n_units, tpu, n_exp)
    unit_len = jnp.sum(sl, axis=1)
    off = jnp.concatenate([jnp.zeros((n_units, 1), jnp.int32), jnp.cumsum(unit_len, axis=1)], axis=1)
    seg_start = off[:, None, :n_exp] + jnp.cumsum(sl, axis=1) - sl
    base_of = jnp.repeat(stage_base.astype(F32), tm, axis=0)
    experts = jnp.arange(n_exp, dtype=F32)
    for e_idx, r_idx in ((_META_E1, _META_R1), (_META_E2, _META_R2)):
        base = jnp.sum(jnp.where(meta[:, e_idx:e_idx + 1] == experts, base_of, 0.0), axis=1)
        meta = meta.at[:, r_idx].add(base)
    tables = (seg_start.reshape(-1), seg_len.reshape(-1), stage_base.reshape(-1), off.reshape(-1))

    tiles = min(MOE_STEP_TILES, tpu)
    spu = tpu // tiles

    def block_of(u, s):
        t = jnp.where(s < spu, s, jnp.where(s >= spu + n_exp, s - spu - n_exp, spu - 1))
        return u * spu + t

    tile_row = lambda u, s, *_: (block_of(u, s), 0)
    expert = lambda u, s, *_: (jnp.clip(s - spu, 0, n_exp - 1), 0, 0)
    out_row = lambda u, s, *_: (u * spu + jnp.clip(s - spu - n_exp, 0, spu - 1), 0)
    const = lambda u, s, *_: (0, 0)

    return pl.pallas_call(
        functools.partial(_moe_experts_kernel, alpha=alpha, n_exp=n_exp, steps_per_unit=spu,
                          hidden=hidden),
        grid_spec=pltpu.PrefetchScalarGridSpec(
            num_scalar_prefetch=4, grid=(n_units, 2 * spu + n_exp),
            in_specs=[pl.BlockSpec((tiles * tm, d), tile_row), pl.BlockSpec((tiles * tm, LANES), tile_row),
                      pl.BlockSpec((1, d, two_f), expert), pl.BlockSpec((1, hidden, d), expert),
                      pl.BlockSpec((1, d), const), pl.BlockSpec((1, d), const)],
            out_specs=pl.BlockSpec((tiles * tm, d), out_row),
            scratch_shapes=[pltpu.VMEM((cap, d), BF16), pltpu.VMEM((tiles, n_stage, d), BF16)]),
        out_shape=jax.ShapeDtypeStruct((n, d), F32),
        compiler_params=_cparams("arbitrary", "arbitrary"),
        name="moe_experts",
    )(*tables, x, meta, w_up, w_dn, g, b)


def _router_params(w_group, b_group, w_router, b_router):
    d, n_grp = w_group.shape
    n_exp = w_router.shape[1]
    w = jnp.zeros((d, LANES), F32).at[:, :n_exp].set(w_router).at[:, n_exp:n_exp + n_grp].set(w_group)
    hi = w.astype(BF16)
    lo = (w - hi.astype(F32)).astype(BF16)
    br = jnp.zeros((1, LANES), F32).at[0, :n_exp].set(b_router).at[0, n_exp:n_exp + n_grp].set(b_group)
    return hi, lo, br


def _sgu_mix_params(w_s, b_s, t_len, gd):
    groups = w_s.shape[0]
    L = min(t_len, SGU_LEN)
    pos = jnp.arange(L)
    allowed = (pos[None, :] // CHUNK) <= (pos[:, None] // CHUNK)
    w = jnp.where(allowed[None], w_s[:, :L, :L], 0.0)
    rep = SGU_LEN // L
    if rep > 1:
        eye = jnp.eye(rep, dtype=F32)
        w = jnp.einsum("ab,gts->gatbs", eye, w).reshape(groups, SGU_LEN, SGU_LEN)
    bias = jnp.tile(b_s[:, :L], (1, rep))
    bias = jnp.broadcast_to(bias[:, :, None], (groups, SGU_LEN, gd))
    return w.astype(BF16), bias.astype(F32)


def kernel(x_prompt, x_sample, cache_sb_k, cache_sb_v, state_gla, state_pool, sb_w_qkv, sb_w_o, sgu_w_in, sgu_ln_g, sgu_ln_b, sgu_w_s, sgu_b_s, sgu_w_out, gla_w_in, gla_w_g1, gla_w_g2, gla_b_g, gla_norm_g, gla_w_o, pool_w, pool_scale, ln1_g, ln1_b, ln2_g, ln2_b, moe_w_group, moe_b_group, moe_w_router, moe_b_router, moe_w_up, moe_w_down):
    depth = ln1_g.shape[0]
    d = x_prompt.shape[-1]
    alpha = (2.0 * depth) ** 0.25
    sb_heads, sb_hd = cache_sb_k.shape[3], cache_sb_k.shape[4]
    gla_heads, gla_dk, gla_dv = state_gla.shape[2], state_gla.shape[3], state_gla.shape[4]
    n_grp = moe_w_group.shape[-1]
    dq, dvt = gla_heads * gla_dk, gla_heads * gla_dv
    n_hist = state_pool.shape[2]
    sgu_groups = sgu_w_s.shape[1]
    sgu_width = sgu_w_out.shape[1]

    bf = lambda a: a.astype(BF16)
    vec = lambda a: a.reshape(1, -1)

    def run(x3, past):
        batch, seq, _ = x3.shape
        n = batch * seq
        x = x3.reshape(n, d)
        sb_k, sb_v, sgu_v, gla_s, pool_h = [], [], [], [], []
        for i in range(depth):
            m, j = i % 4, i // 4
            g1, b1 = vec(ln1_g[i]), vec(ln1_b[i])
            if m == 0:
                wq, wk, wv = (bf(sb_w_qkv[j][:, c * d:(c + 1) * d]) for c in range(3))
                from_t = lambda a: a.reshape(batch, sb_heads, sb_hd, -1).transpose(0, 3, 1, 2)
                to_t = lambda a: a.transpose(0, 2, 3, 1).reshape(batch, d, -1)
                if past is None:
                    q, kt, vt, ktb, vtb = _sb_qkv(x, wq, wk.T, wv.T, batch, seq, sb_hd ** -0.5)
                    o = _sb_prompt(q, ktb, vtb, batch, seq, sb_hd)
                    sb_k.append(from_t(kt))
                    sb_v.append(from_t(vt))
                else:
                    q, k, v, kb, vb = _proj(
                        x, [wq, wk, wv],
                        [(0, BF16, sb_hd ** -0.5), (1, F32, 1.0), (2, F32, 1.0), (1, BF16, 1.0), (2, BF16, 1.0)],
                        "sb_qkv")
                    o = _sb_sample(q, kb, vb, to_t(past[0][j]), to_t(past[1][j]), batch, seq, sb_hd)
                    sb_k.append(k.reshape(batch, seq, sb_heads, sb_hd))
                    sb_v.append(v.reshape(batch, seq, sb_heads, sb_hd))
                x = _mm_res_ln(o, bf(sb_w_o[j]), x, g1, b1, alpha, "sb_out")
            elif m == 1:
                ws_mix, bs_mix = _sgu_mix_params(sgu_w_s[j], sgu_b_s[j], seq, sgu_width // sgu_groups)
                x, v = _sgu(x, bf(sgu_w_in[j]), vec(sgu_ln_g[j]), vec(sgu_ln_b[j]), ws_mix, bs_mix,
                            bf(sgu_w_out[j]), g1, b1, alpha, past is not None)
                if v is not None:
                    sgu_v.append(v.reshape(batch, seq, sgu_width))
            elif m == 2:
                w_in = gla_w_in[j]
                wg1 = jnp.zeros((d, LANES), F32).at[:, :gla_w_g1.shape[-1]].set(gla_w_g1[j])
                wg2 = jnp.zeros((LANES, dq), F32).at[:gla_w_g2.shape[1]].set(gla_w_g2[j])
                q, k, v, r, la = _gla_proj(
                    x, bf(w_in[:, :dq]), bf(w_in[:, dq:2 * dq]), bf(w_in[:, 2 * dq:2 * dq + dvt]),
                    bf(w_in[:, 2 * dq + dvt:]), bf(wg1), bf(wg2), vec(gla_b_g[j]), gla_dk ** -0.5)
                if past is None:
                    s0 = jnp.zeros((batch, gla_heads, gla_dk, gla_dv), F32)
                    seq_p = seq
                else:
                    s0 = past[2][j]
                    seq_p = -(-seq // SGU_LEN) * SGU_LEN
                    padrows = lambda a: jnp.pad(a.reshape(batch, seq, -1),
                                                ((0, 0), (0, seq_p - seq), (0, 0))).reshape(batch * seq_p, -1)
                    q, k, v, la = padrows(q), padrows(k), padrows(v), padrows(la)
                o, s = _gla_scan(q, k, v, la, s0, batch, seq_p, gla_heads)
                if seq_p != seq:
                    o = o.reshape(batch, seq_p, dvt)[:, :seq].reshape(n, dvt)
                gla_s.append(s)
                x = _gla_out(o, r, vec(gla_norm_g[j]), bf(gla_w_o[j]), x, g1, b1, alpha, gla_heads)
            else:
                if past is None:
                    hist = jnp.zeros((batch, POOL_PAD, d), F32)
                    n_valid = 0
                else:
                    hist = jnp.pad(past[3][j], ((0, 0), (POOL_PAD - n_hist, 0), (0, 0)))
                    n_valid = n_hist
                x_in = x.reshape(batch, seq, d)
                if past is None:
                    pool_h.append(x_in[:, seq - n_hist:])
                else:
                    pool_h.append(jnp.concatenate([past[3][j], x_in], axis=1)[:, -n_hist:])
                x = _pool(x, hist, bf(pool_w[j]), vec(pool_scale[j]), g1, b1, alpha, batch, seq, n_valid)
            wr_hi, wr_lo, br = _router_params(moe_w_group[i], moe_b_group[i], moe_w_router[i], moe_b_router[i])
            x = _moe(x, wr_hi, wr_lo, br, bf(moe_w_up[i]), bf(moe_w_down[i]),
                     vec(ln2_g[i]), vec(ln2_b[i]), alpha, n_grp)
        return x.reshape(batch, seq, d), sb_k, sb_v, sgu_v, gla_s, pool_h

    y_p, kp, vp, _, sp, hp = run(x_prompt, None)
    y_s, ks, vs, us, ss, hs = run(x_sample, (cache_sb_k, cache_sb_v, state_gla, state_pool))
    return (y_p, y_s, jnp.stack(kp), jnp.stack(vp), jnp.stack(ks), jnp.stack(vs), jnp.stack(us),
            jnp.stack(sp), jnp.stack(ss), jnp.stack(hp), jnp.stack(hs))
```

```python
import functools
import math

import jax
import jax.numpy as jnp
from jax import lax
from jax.experimental import pallas as pl
from jax.experimental.pallas import tpu as pltpu

F32 = jnp.float32
BF16 = jnp.bfloat16

LN_EPS = 1e-5
CHUNK = 64
SGU_LEN = 128
SB_QBLOCK = 128
SB_SPAN_BLOCKS = 3
SB_LANE_GROUPS = 2
SB_UNDERFLOW = 104.0
GLA_BLOCK = 16
GLA_TAU = 16.0
POOL_WINDOWS = (2, 4, 8, 16)
POOL_PAD = 16
MOE_EXPERTS_PER_GROUP = 4
MOE_TILE = 256
MOE_ROUTE_TILES = 2
MOE_STEP_TILES = 2
MOE_UNIT = 4096
MOE_ALIGN = 16
MOE_FFN_ROWS = 512
LANES = 128
VMEM_LIMIT = 56 * 1024 * 1024


def _cparams(*sem):
    return pltpu.CompilerParams(dimension_semantics=sem, vmem_limit_bytes=VMEM_LIMIT)


def _tile(n, pref):
    t = min(n, pref)
    while n % t:
        t -= 8
    return t


def _layer_norm(x, g, b):
    mu = jnp.mean(x, axis=-1, keepdims=True)
    xc = x - mu
    var = jnp.mean(xc * xc, axis=-1, keepdims=True)
    return xc * lax.rsqrt(var + LN_EPS) * g + b


def _split_bf16(x):
    hi = x.astype(BF16)
    lo = (x - hi.astype(F32)).astype(BF16)
    return hi, lo


def _log_sigmoid(z):
    return jnp.minimum(z, 0.0) - jnp.log(1.0 + jnp.exp(-jnp.abs(z)))


def _silu(x):
    return x / (1.0 + jnp.exp(-x))


def _proj_kernel(x_ref, *refs, outs):
    n_w = len(refs) - len(outs)
    xb = x_ref[...].astype(BF16)
    ys = [jnp.dot(xb, w_ref[...], preferred_element_type=F32) for w_ref in refs[:n_w]]
    for (wi, _, scale), o_ref in zip(outs, refs[n_w:]):
        o_ref[...] = (ys[wi] * scale).astype(o_ref.dtype)


def _proj(x, ws, outs, name):
    n, k = x.shape
    tm = _tile(n, 512)
    return pl.pallas_call(
        functools.partial(_proj_kernel, outs=tuple(outs)),
        grid=(n // tm,),
        in_specs=[pl.BlockSpec((tm, k), lambda i: (i, 0))]
        + [pl.BlockSpec(w.shape, lambda i: (0, 0)) for w in ws],
        out_specs=[pl.BlockSpec((tm, ws[wi].shape[1]), lambda i: (i, 0)) for wi, _, _ in outs],
        out_shape=[jax.ShapeDtypeStruct((n, ws[wi].shape[1]), dt) for wi, dt, _ in outs],
        compiler_params=_cparams("parallel"),
        name=name,
    )(x, *ws)


def _mm_res_ln_kernel(a_ref, w_ref, res_ref, g_ref, b_ref, o_ref, *, alpha):
    y = jnp.dot(a_ref[...].astype(BF16), w_ref[...], preferred_element_type=F32)
    o_ref[...] = _layer_norm(alpha * res_ref[...] + y, g_ref[...], b_ref[...])


def _mm_res_ln(a, w, res, g, b, alpha, name):
    n, k = a.shape
    d = w.shape[1]
    tm = _tile(n, 512)
    return pl.pallas_call(
        functools.partial(_mm_res_ln_kernel, alpha=alpha),
        grid=(n // tm,),
        in_specs=[pl.BlockSpec((tm, k), lambda i: (i, 0)),
                  pl.BlockSpec((k, d), lambda i: (0, 0)),
                  pl.BlockSpec((tm, d), lambda i: (i, 0)),
                  pl.BlockSpec((1, d), lambda i: (0, 0)),
                  pl.BlockSpec((1, d), lambda i: (0, 0))],
        out_specs=pl.BlockSpec((tm, d), lambda i: (i, 0)),
        out_shape=jax.ShapeDtypeStruct((n, d), F32),
        compiler_params=_cparams("parallel"),
        name=name,
    )(a, w, res, g, b)


def _sb_later_selector(span):
    s = jnp.arange(span)
    return (s[:, None] > s[None, :]).astype(BF16)


def _sb_stack_heads(q, head_dim):
    lane = lax.broadcasted_iota(jnp.int32, q.shape, 1)
    parts = [jnp.where((lane >= h * head_dim) & (lane < (h + 1) * head_dim), q, jnp.zeros_like(q))
             for h in range(LANES // head_dim)]
    return jnp.concatenate(parts, axis=0)


def _sb_unstack_heads(acc, tq, head_dim):
    lane = lax.broadcasted_iota(jnp.int32, (tq, LANES), 1)
    out = acc[:tq]
    for h in range(1, LANES // head_dim):
        out = jnp.where(lane >= h * head_dim, acc[h * tq:(h + 1) * tq], out)
    return out


def _sb_later_sums(lstay, u_ref, off):
    n = lstay.shape[1]
    terms = lstay.astype(BF16)
    cols = []
    for c0 in range(0, n, 2 * LANES):
        c1 = min(c0 + 2 * LANES, n)
        cols.append(jnp.dot(terms[:, c0:], u_ref[off + c0:off + n, off + c0:off + c1],
                            preferred_element_type=F32))
    return cols[0] if len(cols) == 1 else jnp.concatenate(cols, axis=1)


def _sb_weights(z, valid, u_ref, carry):
    ls = _log_sigmoid(z)
    lstay = ls - z
    if valid is not None:
        lstay = jnp.where(valid, lstay, 0.0)
    later = _sb_later_sums(lstay, u_ref, 0)
    if carry is not None:
        later = later + carry
    w = jnp.exp(ls + later)
    if valid is not None:
        w = jnp.where(valid, w, 0.0)
    return w.astype(BF16), jnp.sum(lstay, axis=-1, keepdims=True)


def _dot_t(a, bt):
    return lax.dot_general(a, bt, (((1,), (1,)), ((), ())), preferred_element_type=F32)


def _sb_qkv_kernel(x_ref, wq_ref, wkt_ref, wvt_ref, q_ref, kt_ref, vt_ref, ktb_ref, vtb_ref, *, qscale):
    xb = x_ref[...].astype(BF16)
    q_ref[...] = (jnp.dot(xb, wq_ref[...], preferred_element_type=F32) * qscale).astype(BF16)
    kt = _dot_t(wkt_ref[...], xb)
    kt_ref[...] = kt
    ktb_ref[...] = kt.astype(BF16)
    vt = _dot_t(wvt_ref[...], xb)
    vt_ref[...] = vt
    vtb_ref[...] = vt.astype(BF16)


def _sb_qkv(x, wq, wkt, wvt, batch, seq, qscale):
    n, d = x.shape
    tm = _tile(seq, 512)
    nt = seq // tm
    row = lambda b, t: (b * nt + t, 0)
    const = lambda b, t: (0, 0)
    col = pl.BlockSpec((None, d, tm), lambda b, t: (b, 0, t))
    return pl.pallas_call(
        functools.partial(_sb_qkv_kernel, qscale=qscale),
        grid=(batch, nt),
        in_specs=[pl.BlockSpec((tm, d), row), pl.BlockSpec((d, d), const),
                  pl.BlockSpec((d, d), const), pl.BlockSpec((d, d), const)],
        out_specs=[pl.BlockSpec((tm, d), row), col, col, col, col],
        out_shape=[jax.ShapeDtypeStruct((n, d), BF16),
                   jax.ShapeDtypeStruct((batch, d, seq), F32), jax.ShapeDtypeStruct((batch, d, seq), F32),
                   jax.ShapeDtypeStruct((batch, d, seq), BF16), jax.ShapeDtypeStruct((batch, d, seq), BF16)],
        compiler_params=_cparams("parallel", "parallel"),
        name="sb_qkv",
    )(x, wq, wkt, wvt)


def _sb_prompt_kernel(q_ref, *refs, head_dim, n_span):
    kt_blocks, vt_blocks = refs[:n_span], refs[n_span:2 * n_span]
    kt_hbm, vt_hbm, u_ref, o_ref, kbuf, vbuf, sem = refs[2 * n_span:]
    b, hp, qi = pl.program_id(0), pl.program_id(1), pl.program_id(2)
    tq = q_ref.shape[0]
    width = q_ref.shape[1]
    rows = tq * (LANES // head_dim)
    span = n_span * SB_QBLOCK
    first = jnp.maximum(qi - (n_span - 1), 0)
    row = lax.broadcasted_iota(jnp.int32, (rows, span), 0)
    col = lax.broadcasted_iota(jnp.int32, (rows, span), 1)
    valid = (first * SB_QBLOCK + col) < (qi * tq + (row & (tq - 1)))
    n_groups = width // LANES
    qss, accs, carries = [], [], []
    for g in range(n_groups):
        ls = slice(g * LANES, (g + 1) * LANES)
        qs = _sb_stack_heads(q_ref[:, ls], head_dim)
        kt = jnp.concatenate([r[ls, :] for r in kt_blocks], axis=1)
        vt = jnp.concatenate([r[ls, :] for r in vt_blocks], axis=1)
        w, tot = _sb_weights(jnp.dot(qs, kt, preferred_element_type=F32), valid, u_ref, None)
        qss.append(qs)
        accs.append(_dot_t(w, vt))
        carries.append(tot)

    def cond(state):
        j, _, carries = state
        worst = functools.reduce(jnp.maximum, [jnp.max(c) for c in carries])
        return (j >= 0) & (worst > -SB_UNDERFLOW)

    def body(state):
        j, accs, carries = state
        copies = []
        for g in range(n_groups):
            src = (b, pl.ds(hp * width + g * LANES, LANES), pl.ds(j * SB_QBLOCK, SB_QBLOCK))
            copies.append(pltpu.make_async_copy(kt_hbm.at[src], kbuf.at[g], sem.at[g, 0]))
            copies.append(pltpu.make_async_copy(vt_hbm.at[src], vbuf.at[g], sem.at[g, 1]))
        for c in copies:
            c.start()
        for c in copies:
            c.wait()
        new_accs, new_carries = [], []
        for g in range(n_groups):
            w, tot = _sb_weights(jnp.dot(qss[g], kbuf[g], preferred_element_type=F32), None, u_ref,
                                 carries[g])
            new_accs.append(accs[g] + _dot_t(w, vbuf[g]))
            new_carries.append(carries[g] + tot)
        return j - 1, tuple(new_accs), tuple(new_carries)

    _, accs, _ = lax.while_loop(cond, body, (first - 1, tuple(accs), tuple(carries)))
    o_ref[...] = jnp.concatenate([_sb_unstack_heads(a, tq, head_dim) for a in accs],
                                 axis=1).astype(o_ref.dtype)


def _sb_prompt(q, kt, vt, batch, seq, head_dim):
    n, d = q.shape
    nqb = seq // SB_QBLOCK
    n_span = min(SB_SPAN_BLOCKS, nqb)
    u = _sb_later_selector(n_span * SB_QBLOCK)
    width = SB_LANE_GROUPS * LANES

    def span_block(i):
        return pl.BlockSpec((None, width, SB_QBLOCK),
                            lambda b, hp, qi: (b, hp, jnp.maximum(qi - (n_span - 1), 0) + i))

    blocks = [span_block(i) for i in range(n_span)]
    q_spec = pl.BlockSpec((SB_QBLOCK, width), lambda b, hp, qi: (b * nqb + qi, hp))
    return pl.pallas_call(
        functools.partial(_sb_prompt_kernel, head_dim=head_dim, n_span=n_span),
        grid=(batch, d // width, nqb),
        in_specs=[q_spec] + blocks + blocks
        + [pl.BlockSpec(memory_space=pl.ANY), pl.BlockSpec(memory_space=pl.ANY),
           pl.BlockSpec(u.shape, lambda b, hp, qi: (0, 0))],
        out_specs=q_spec,
        out_shape=jax.ShapeDtypeStruct((n, d), BF16),
        scratch_shapes=[pltpu.VMEM((SB_LANE_GROUPS, LANES, SB_QBLOCK), BF16),
                        pltpu.VMEM((SB_LANE_GROUPS, LANES, SB_QBLOCK), BF16),
                        pltpu.SemaphoreType.DMA((SB_LANE_GROUPS, 2))],
        compiler_params=_cparams("parallel", "parallel", "arbitrary"),
        name="sb_prompt",
    )(q, *([kt] * n_span), *([vt] * n_span), kt, vt, u)


def _sb_sample_kernel(q_ref, kn_ref, vn_ref, kt_ref, vt_ref, kp_hbm, vp_hbm, u_ref, o_ref,
                      kbuf, vbuf, sem, *, head_dim, past):
    b, hp = pl.program_id(0), pl.program_id(1)
    tq = q_ref.shape[0]
    tail = kt_ref.shape[1]
    qs = _sb_stack_heads(q_ref[...], head_dim)
    rows = qs.shape[0]
    pad = jnp.zeros((SB_QBLOCK - tq, LANES), BF16)
    kn = jnp.concatenate([kn_ref[...], pad], axis=0)
    vn = jnp.concatenate([vn_ref[...], pad], axis=0)
    z = jnp.concatenate([jnp.dot(qs, kt_ref[...].astype(BF16), preferred_element_type=F32),
                         _dot_t(qs, kn)], axis=1)
    span = tail + SB_QBLOCK
    row = lax.broadcasted_iota(jnp.int32, (rows, span), 0)
    col = lax.broadcasted_iota(jnp.int32, (rows, span), 1)
    valid = col < tail + (row & (tq - 1))
    w, carry = _sb_weights(z, valid, u_ref, None)
    acc = (_dot_t(w[:, :tail], vt_ref[...].astype(BF16))
           + jnp.dot(w[:, tail:], vn, preferred_element_type=F32))

    def cond(state):
        j, _, carry = state
        return (j >= 0) & (jnp.max(carry) > -SB_UNDERFLOW)

    def body(state):
        j, acc, carry = state
        src = (b, pl.ds(hp * LANES, LANES), pl.ds(j * SB_QBLOCK, SB_QBLOCK))
        kc = pltpu.make_async_copy(kp_hbm.at[src], kbuf, sem.at[0])
        vc = pltpu.make_async_copy(vp_hbm.at[src], vbuf, sem.at[1])
        kc.start()
        vc.start()
        kc.wait()
        vc.wait()
        w, tot = _sb_weights(jnp.dot(qs, kbuf[...].astype(BF16), preferred_element_type=F32),
                             None, u_ref, carry)
        return j - 1, acc + _dot_t(w, vbuf[...].astype(BF16)), carry + tot

    _, acc, _ = lax.while_loop(cond, body, ((past - tail) // SB_QBLOCK - 1, acc, carry))
    o_ref[...] = _sb_unstack_heads(acc, tq, head_dim).astype(o_ref.dtype)


def _sb_sample(q, kn, vn, kpt, vpt, batch, tq, head_dim):
    n, d = q.shape
    past = kpt.shape[2]
    tail = min(SB_SPAN_BLOCKS - 1, past // SB_QBLOCK) * SB_QBLOCK
    u = _sb_later_selector(tail + SB_QBLOCK)
    new = pl.BlockSpec((tq, LANES), lambda b, hp: (b, hp))
    tail_spec = pl.BlockSpec((None, LANES, tail), lambda b, hp: (b, hp, past // tail - 1))
    return pl.pallas_call(
        functools.partial(_sb_sample_kernel, head_dim=head_dim, past=past),
        grid=(batch, d // LANES),
        in_specs=[new, new, new, tail_spec, tail_spec,
                  pl.BlockSpec(memory_space=pl.ANY), pl.BlockSpec(memory_space=pl.ANY),
                  pl.BlockSpec(u.shape, lambda b, hp: (0, 0))],
        out_specs=new,
        out_shape=jax.ShapeDtypeStruct((n, d), BF16),
        scratch_shapes=[pltpu.VMEM((LANES, SB_QBLOCK), F32), pltpu.VMEM((LANES, SB_QBLOCK), F32),
                        pltpu.SemaphoreType.DMA((2,))],
        compiler_params=_cparams("parallel", "parallel"),
        name="sb_sample",
    )(q, kn, vn, kpt, vpt, kpt, vpt, u)


def _gelu_tanh(x):
    c = math.sqrt(2.0 / math.pi)
    return 0.5 * x * (1.0 + jnp.tanh(c * (x + 0.044715 * (x * x * x))))


def _sgu_kernel(x_ref, win_ref, lng_ref, lnb_ref, ws_ref, bs_ref, wout_ref, g_ref, b_ref,
                o_ref, *v_out, alpha, width, groups):
    x = x_ref[...]
    z = _gelu_tanh(jnp.dot(x.astype(BF16), win_ref[...], preferred_element_type=F32))
    u = z[:, :width]
    v = _layer_norm(z[:, width:], lng_ref[...], lnb_ref[...])
    if v_out:
        v_out[0][...] = v
    tm = x.shape[0]
    gd = width // groups
    vb = v.astype(BF16)
    rows = []
    for c in range(tm // SGU_LEN):
        r0 = c * SGU_LEN
        cols = []
        for g in range(groups):
            mixed = jnp.dot(ws_ref[g], vb[r0:r0 + SGU_LEN, g * gd:(g + 1) * gd],
                            preferred_element_type=F32) + bs_ref[g]
            cols.append(mixed)
        rows.append(jnp.concatenate(cols, axis=1))
    mixed = rows[0] if len(rows) == 1 else jnp.concatenate(rows, axis=0)
    y = jnp.dot((u * mixed).astype(BF16), wout_ref[...], preferred_element_type=F32)
    o_ref[...] = _layer_norm(alpha * x + y, g_ref[...], b_ref[...])


def _sgu(x, w_in, ln_g, ln_b, ws_mix, bs_mix, w_out, g, b, alpha, want_v):
    n, d = x.shape
    width = w_out.shape[0]
    groups = ws_mix.shape[0]
    tm = _tile(n, 256)
    const2 = lambda i: (0, 0)
    const3 = lambda i: (0, 0, 0)
    out_specs = [pl.BlockSpec((tm, d), lambda i: (i, 0))]
    out_shape = [jax.ShapeDtypeStruct((n, d), F32)]
    if want_v:
        out_specs.append(pl.BlockSpec((tm, width), lambda i: (i, 0)))
        out_shape.append(jax.ShapeDtypeStruct((n, width), F32))
    res = pl.pallas_call(
        functools.partial(_sgu_kernel, alpha=alpha, width=width, groups=groups),
        grid=(n // tm,),
        in_specs=[pl.BlockSpec((tm, d), lambda i: (i, 0)),
                  pl.BlockSpec(w_in.shape, const2),
                  pl.BlockSpec((1, width), const2),
                  pl.BlockSpec((1, width), const2),
                  pl.BlockSpec(ws_mix.shape, const3),
                  pl.BlockSpec(bs_mix.shape, const3),
                  pl.BlockSpec(w_out.shape, const2),
                  pl.BlockSpec((1, d), const2),
                  pl.BlockSpec((1, d), const2)],
        out_specs=out_specs,
        out_shape=out_shape,
        compiler_params=_cparams("parallel"),
        name="sgu",
    )(x, w_in, ln_g, ln_b, ws_mix, bs_mix, w_out, g, b)
    return res[0], (res[1] if want_v else None)


def _gla_proj_kernel(x_ref, wq_ref, wk_ref, wv_ref, wr_ref, wg1_ref, wg2_ref, bg_ref,
                     q_ref, k_ref, v_ref, r_ref, la_ref, *, qscale):
    xb = x_ref[...].astype(BF16)
    q_ref[...] = jnp.dot(xb, wq_ref[...], preferred_element_type=F32) * qscale
    k_ref[...] = jnp.dot(xb, wk_ref[...], preferred_element_type=F32)
    v_ref[...] = jnp.dot(xb, wv_ref[...], preferred_element_type=F32).astype(BF16)
    r_ref[...] = jnp.dot(xb, wr_ref[...], preferred_element_type=F32)
    t = jnp.dot(xb, wg1_ref[...], preferred_element_type=F32)
    gate = jnp.dot(t.astype(BF16), wg2_ref[...], preferred_element_type=F32) + bg_ref[...]
    la_ref[...] = _log_sigmoid(gate) / GLA_TAU


def _gla_proj(x, wq, wk, wv, wr, wg1, wg2, bg, qscale):
    n, d = x.shape
    dq, dv = wq.shape[1], wv.shape[1]
    tm = _tile(n, 512)
    const = lambda i: (0, 0)
    row = lambda i: (i, 0)
    return pl.pallas_call(
        functools.partial(_gla_proj_kernel, qscale=qscale),
        grid=(n // tm,),
        in_specs=[pl.BlockSpec((tm, d), row)]
        + [pl.BlockSpec(w.shape, const) for w in (wq, wk, wv, wr, wg1, wg2, bg)],
        out_specs=[pl.BlockSpec((tm, dq), row), pl.BlockSpec((tm, dq), row),
                   pl.BlockSpec((tm, dv), row), pl.BlockSpec((tm, dv), row),
                   pl.BlockSpec((tm, dq), row)],
        out_shape=[jax.ShapeDtypeStruct((n, dq), F32), jax.ShapeDtypeStruct((n, dq), F32),
                   jax.ShapeDtypeStruct((n, dv), BF16), jax.ShapeDtypeStruct((n, dv), F32),
                   jax.ShapeDtypeStruct((n, dq), F32)],
        compiler_params=_cparams("parallel"),
        name="gla_proj",
    )(x, wq, wk, wv, wr, wg1, wg2, bg)


def _gla_scan_kernel(q_ref, k_ref, v_ref, la_ref, s0_ref, o_ref, sfin_ref, st_ref,
                     *, heads, dk, dv):
    ti = pl.program_id(1)
    tt = q_ref.shape[0]
    nsb = tt // GLA_BLOCK

    @pl.when(ti == 0)
    def _():
        for h in range(heads):
            st_ref[h] = s0_ref[0, h].T

    row = lax.broadcasted_iota(jnp.int32, (tt, tt), 0)
    col = lax.broadcasted_iota(jnp.int32, (tt, tt), 1)
    same = (row // GLA_BLOCK) == (col // GLA_BLOCK)
    tril = same & (col <= row)
    bdcat = jnp.concatenate([jnp.where(tril, 1.0, 0.0), jnp.where(same, 1.0, 0.0)],
                            axis=0).astype(BF16)
    rsub = lax.broadcasted_iota(jnp.int32, (tt, dk), 0) // GLA_BLOCK

    for h in range(heads):
        ks = slice(h * dk, (h + 1) * dk)
        vs = slice(h * dv, (h + 1) * dv)
        hi, lo = _split_bf16(la_ref[:, ks])
        ct = (jnp.dot(bdcat, hi, preferred_element_type=F32)
              + jnp.dot(bdcat, lo, preferred_element_type=F32))
        cum, tot = ct[:tt], ct[tt:]
        k = k_ref[:, ks]
        qd = (q_ref[:, ks] * jnp.exp(cum)).astype(BF16)
        kd = (k * jnp.exp(-cum)).astype(BF16)
        kt = k * jnp.exp(tot - cum)
        sc = lax.dot_general(qd, kd, (((1,), (1,)), ((), ())), preferred_element_type=F32)
        sc = jnp.where(tril, sc, 0.0).astype(BF16)
        vh = v_ref[:, vs]
        o_intra = jnp.dot(sc, vh, preferred_element_type=F32)
        ktcat = jnp.concatenate(
            [jnp.where(rsub == i, kt, 0.0).astype(BF16) for i in range(nsb)], axis=1)
        ut = lax.dot_general(vh, ktcat, (((0,), (0,)), ((), ())), preferred_element_type=F32)
        st = st_ref[h]
        parts = []
        for i in range(nsb):
            r0 = i * GLA_BLOCK
            parts.append(lax.dot_general(qd[r0:r0 + GLA_BLOCK], st.astype(BF16),
                                         (((1,), (1,)), ((), ())), preferred_element_type=F32))
            st = st * jnp.exp(tot[r0:r0 + 1, :]) + ut[:, i * dk:(i + 1) * dk]
        st_ref[h] = st
        o_ref[:, vs] = o_intra + jnp.concatenate(parts, axis=0)

    @pl.when(ti == pl.num_programs(1) - 1)
    def _():
        for h in range(heads):
            sfin_ref[0, h] = st_ref[h].T


def _gla_scan(q, k, v, la, s0, batch, seq, heads):
    n, dq = q.shape
    dvt = v.shape[1]
    dk, dv = dq // heads, dvt // heads
    tt = SGU_LEN
    nt = seq // tt
    row = lambda b, t: (b * nt + t, 0)
    st = lambda b, t: (b, 0, 0, 0)
    return pl.pallas_call(
        functools.partial(_gla_scan_kernel, heads=heads, dk=dk, dv=dv),
        grid=(batch, nt),
        in_specs=[pl.BlockSpec((tt, dq), row), pl.BlockSpec((tt, dq), row),
                  pl.BlockSpec((tt, dvt), row), pl.BlockSpec((tt, dq), row),
                  pl.BlockSpec((1, heads, dk, dv), st)],
        out_specs=[pl.BlockSpec((tt, dvt), row), pl.BlockSpec((1, heads, dk, dv), st)],
        out_shape=[jax.ShapeDtypeStruct((n, dvt), F32),
                   jax.ShapeDtypeStruct((batch, heads, dk, dv), F32)],
        scratch_shapes=[pltpu.VMEM((heads, dv, dk), F32)],
        compiler_params=_cparams("parallel", "arbitrary"),
        name="gla_scan",
    )(q, k, v, la, s0)


def _gla_out_kernel(o_ref, r_ref, ng_ref, wo_ref, x_ref, g_ref, b_ref, y_ref, *, alpha, heads):
    o = o_ref[...]
    dv = o.shape[1] // heads
    normed = []
    for h in range(heads):
        oh = o[:, h * dv:(h + 1) * dv]
        ms = jnp.mean(oh * oh, axis=-1, keepdims=True)
        normed.append(oh * lax.rsqrt(ms + LN_EPS) * ng_ref[...])
    gated = jnp.concatenate(normed, axis=1) * _silu(r_ref[...])
    y = jnp.dot(gated.astype(BF16), wo_ref[...], preferred_element_type=F32)
    y_ref[...] = _layer_norm(alpha * x_ref[...] + y, g_ref[...], b_ref[...])


def _gla_out(o, r, norm_g, w_o, x, g, b, alpha, heads):
    n, dvt = o.shape
    d = x.shape[1]
    tm = _tile(n, 512)
    row = lambda i: (i, 0)
    const = lambda i: (0, 0)
    return pl.pallas_call(
        functools.partial(_gla_out_kernel, alpha=alpha, heads=heads),
        grid=(n // tm,),
        in_specs=[pl.BlockSpec((tm, dvt), row), pl.BlockSpec((tm, dvt), row),
                  pl.BlockSpec(norm_g.shape, const), pl.BlockSpec(w_o.shape, const),
                  pl.BlockSpec((tm, d), row), pl.BlockSpec((1, d), const),
                  pl.BlockSpec((1, d), const)],
        out_specs=pl.BlockSpec((tm, d), row),
        out_shape=jax.ShapeDtypeStruct((n, d), F32),
        compiler_params=_cparams("parallel"),
        name="gla_out",
    )(o, r, norm_g, w_o, x, g, b)


def _pool_kernel(x_ref, prev_ref, hist_ref, w_ref, sc_ref, g_ref, b_ref, o_ref, buf_ref,
                 *, alpha, n_valid):
    ti = pl.program_id(1)
    tm, d = x_ref.shape
    x = x_ref[...]

    @pl.when(ti == 0)
    def _():
        buf_ref[0:POOL_PAD, :] = hist_ref[0]

    @pl.when(ti > 0)
    def _():
        buf_ref[0:POOL_PAD, :] = prev_ref[...]

    buf_ref[POOL_PAD:POOL_PAD + tm, :] = x
    gd = d // len(POOL_WINDOWS)
    pos = ti * tm + lax.broadcasted_iota(jnp.int32, (tm, 1), 0)
    ys = []
    for g, win in enumerate(POOL_WINDOWS):
        cs = slice(g * gd, (g + 1) * gd)
        acc = x[:, cs]
        for j in range(1, win):
            acc = acc + buf_ref[POOL_PAD - j:POOL_PAD - j + tm, cs]
        count = jnp.minimum(pos + 1 + n_valid, win).astype(F32)
        dlt = acc / count - x[:, cs]
        ys.append(jnp.dot(dlt.astype(BF16), w_ref[g], preferred_element_type=F32))
    y = jnp.concatenate(ys, axis=1) * sc_ref[...]
    o_ref[...] = _layer_norm(alpha * x + y, g_ref[...], b_ref[...])


def _pool(x, hist, w_pool, scale, g, b, alpha, batch, seq, n_valid):
    n, d = x.shape
    tm = _tile(seq, 256)
    nt = seq // tm
    ratio = tm // POOL_PAD
    const2 = lambda bb, t: (0, 0)
    row = lambda bb, t: (bb * nt + t, 0)
    return pl.pallas_call(
        functools.partial(_pool_kernel, alpha=alpha, n_valid=n_valid),
        grid=(batch, nt),
        in_specs=[pl.BlockSpec((tm, d), row),
                  pl.BlockSpec((POOL_PAD, d),
                               lambda bb, t: (jnp.maximum((bb * nt + t) * ratio - 1, 0), 0)),
                  pl.BlockSpec((1, POOL_PAD, d), lambda bb, t: (bb, 0, 0)),
                  pl.BlockSpec(w_pool.shape, lambda bb, t: (0, 0, 0)),
                  pl.BlockSpec((1, d), const2), pl.BlockSpec((1, d), const2),
                  pl.BlockSpec((1, d), const2)],
        out_specs=pl.BlockSpec((tm, d), row),
        out_shape=jax.ShapeDtypeStruct((n, d), F32),
        scratch_shapes=[pltpu.VMEM((POOL_PAD + tm, d), F32)],
        compiler_params=_cparams("parallel", "arbitrary"),
        name="pool",
    )(x, x, hist, w_pool, scale, g, b)


def _moe_route_math(x, wr_hi, wr_lo, br, n_exp, n_grp):
    hi, lo = _split_bf16(x)
    logit = (jnp.dot(hi, wr_hi, preferred_element_type=F32)
             + jnp.dot(lo, wr_hi, preferred_element_type=F32)
             + jnp.dot(hi, wr_lo, preferred_element_type=F32)) + br
    lane_i = lax.broadcasted_iota(jnp.int32, logit.shape, 1)
    lane = lane_i.astype(F32)
    neg = jnp.float32(-jnp.inf)
    big = jnp.float32(LANES)
    is_grp = (lane_i >= n_exp) & (lane_i < n_exp + n_grp)
    gl = jnp.where(is_grp, logit, neg)
    gmax = jnp.max(gl, axis=-1, keepdims=True)
    g_sel = jnp.min(jnp.where(gl == gmax, lane, big), axis=-1, keepdims=True) - n_exp
    g_prob = 1.0 / jnp.sum(jnp.where(is_grp, jnp.exp(gl - gmax), 0.0), axis=-1, keepdims=True)
    lane_grp = (lane_i // MOE_EXPERTS_PER_GROUP).astype(F32)
    in_grp = (lane_i < n_exp) & (lane_grp == g_sel)
    el = jnp.where(in_grp, logit, neg)
    m1 = jnp.max(el, axis=-1, keepdims=True)
    i1 = jnp.min(jnp.where(el == m1, lane, big), axis=-1, keepdims=True)
    el2 = jnp.where(lane == i1, neg, el)
    m2 = jnp.max(el2, axis=-1, keepdims=True)
    i2 = jnp.min(jnp.where(el2 == m2, lane, big), axis=-1, keepdims=True)
    e2 = jnp.exp(m2 - m1)
    w1 = g_prob / (1.0 + e2)
    w2 = g_prob * e2 / (1.0 + e2)
    return i1, i2, w1, w2


_META_E1, _META_E2, _META_R1, _META_R2, _META_W1, _META_W2 = range(6)


def _moe_route_kernel(x_ref, wrh_ref, wrl_ref, br_ref, meta_ref, cnt_ref, *, n_exp, n_grp):
    tm = MOE_TILE
    lane = lax.broadcasted_iota(jnp.int32, (tm, LANES), 1).astype(F32)
    r = lax.broadcasted_iota(jnp.int32, (tm, tm), 0)
    c = lax.broadcasted_iota(jnp.int32, (tm, tm), 1)
    before = jnp.where(c < r, 1.0, 0.0).astype(BF16)
    metas, counts = [], []
    for t in range(x_ref.shape[0] // tm):
        i1, i2, w1, w2 = _moe_route_math(x_ref[t * tm:(t + 1) * tm, :], wrh_ref[...], wrl_ref[...],
                                         br_ref[...], n_exp, n_grp)
        m1 = jnp.where(lane == i1, 1.0, 0.0)
        m2 = jnp.where(lane == i2, 1.0, 0.0)
        both = m1 + m2
        rank = jnp.dot(before, both.astype(BF16), preferred_element_type=F32)
        r1 = jnp.sum(m1 * rank, axis=-1, keepdims=True)
        r2 = jnp.sum(m2 * rank, axis=-1, keepdims=True)
        meta = jnp.zeros((tm, LANES), F32)
        for idx, col in ((_META_E1, i1), (_META_E2, i2), (_META_R1, r1), (_META_R2, r2),
                         (_META_W1, w1), (_META_W2, w2)):
            meta = jnp.where(lane == idx, col, meta)
        metas.append(meta)
        counts.append(jnp.broadcast_to(jnp.sum(both, axis=0, keepdims=True), cnt_ref.shape[1:]))
    meta_ref[...] = jnp.concatenate(metas, axis=0)
    cnt_ref[...] = jnp.stack(counts, axis=0)


def _moe_route(x, wr_hi, wr_lo, br, n_exp, n_grp):
    n, d = x.shape
    tm = MOE_TILE
    tiles = min(MOE_ROUTE_TILES, n // tm)
    const = lambda i: (0, 0)
    return pl.pallas_call(
        functools.partial(_moe_route_kernel, n_exp=n_exp, n_grp=n_grp),
        grid=(n // (tiles * tm),),
        in_specs=[pl.BlockSpec((tiles * tm, d), lambda i: (i, 0)),
                  pl.BlockSpec(wr_hi.shape, const), pl.BlockSpec(wr_lo.shape, const),
                  pl.BlockSpec((1, LANES), const)],
        out_specs=[pl.BlockSpec((tiles * tm, LANES), lambda i: (i, 0)),
                   pl.BlockSpec((tiles, 8, LANES), lambda i: (i, 0, 0))],
        out_shape=[jax.ShapeDtypeStruct((n, LANES), F32),
                   jax.ShapeDtypeStruct((n // tm, 8, LANES), F32)],
        compiler_params=_cparams("parallel"),
        name="moe_route",
    )(x, wr_hi, wr_lo, br)


def _moe_placement(meta, n_stage, weighted):
    tm = meta.shape[0]
    col = lax.broadcasted_iota(jnp.int32, (tm, n_stage), 1).astype(F32)
    out = None
    for r_idx, w_idx in ((_META_R1, _META_W1), (_META_R2, _META_W2)):
        val = meta[:, w_idx:w_idx + 1] if weighted else 1.0
        term = jnp.where(col == meta[:, r_idx:r_idx + 1], val, 0.0)
        out = term if out is None else out + term
    return out.astype(BF16)


def _moe_segment_copies(tile, n_exp, seg_start, seg_len, stage_base, copy_rows):
    big = 2 * MOE_ALIGN
    for e in range(n_exp):
        idx = tile * n_exp + e
        src, dst, n = stage_base[idx], seg_start[idx], seg_len[idx]

        def piece(c, carry, src=src, dst=dst):
            copy_rows(pl.multiple_of(dst + c * big, MOE_ALIGN), pl.multiple_of(src + c * big, MOE_ALIGN), big)
            return carry

        lax.fori_loop(0, n // big, piece, 0)

        @pl.when(n % big != 0)
        def _(src=src, dst=dst, n=n):
            done = n // big * big
            copy_rows(pl.multiple_of(dst + done, MOE_ALIGN), pl.multiple_of(src + done, MOE_ALIGN), MOE_ALIGN)


def _moe_experts_kernel(seg_start, seg_len, stage_base, off,
                        x_ref, meta_ref, wup_ref, wdn_ref, g_ref, b_ref, o_ref,
                        buf_ref, stg_ref, *, alpha, n_exp, steps_per_unit, hidden):
    u, s = pl.program_id(0), pl.program_id(1)
    tiles, n_stage = stg_ref.shape[0], stg_ref.shape[1]
    tm = x_ref.shape[0] // tiles

    @pl.when((u == 0) & (s == 0))
    def _():
        buf_ref[...] = jnp.zeros_like(buf_ref)

    @pl.when(s < steps_per_unit)
    def _():
        for k in range(tiles):
            place = _moe_placement(meta_ref[k * tm:(k + 1) * tm, :], n_stage, weighted=False)
            xb = x_ref[k * tm:(k + 1) * tm, :].astype(BF16)
            for c0 in range(0, n_stage, 2 * LANES):
                rows = lax.dot_general(place[:, c0:c0 + 2 * LANES], xb, (((0,), (0,)), ((), ())),
                                       preferred_element_type=F32)
                stg_ref[k, c0:c0 + 2 * LANES, :] = rows.astype(BF16)
        for k in range(tiles):
            def copy_rows(unit_row, stage_row, rows, k=k):
                buf_ref[pl.ds(unit_row, rows), :] = stg_ref[k, pl.ds(stage_row, rows), :]

            _moe_segment_copies((u * steps_per_unit + s) * tiles + k, n_exp, seg_start, seg_len,
                                stage_base, copy_rows)

    @pl.when((s >= steps_per_unit) & (s < steps_per_unit + n_exp))
    def _():
        e = s - steps_per_unit
        start = off[u * (n_exp + 1) + e]
        n = off[u * (n_exp + 1) + e + 1] - start

        def ffn_rows(r0, size, n_valid):
            r0 = pl.multiple_of(r0, MOE_ALIGN)
            xs = buf_ref[pl.ds(r0, size), :]
            h = jnp.dot(xs, wup_ref[0], preferred_element_type=F32)
            act = _silu(h[:, :hidden]) * h[:, hidden:]
            y = jnp.dot(act.astype(BF16), wdn_ref[0], preferred_element_type=F32)
            if n_valid is not None:
                row = lax.broadcasted_iota(jnp.int32, y.shape, 0)
                y = jnp.where(row < n_valid, y, xs.astype(F32))
            buf_ref[pl.ds(r0, size), :] = y.astype(BF16)

        n_full = n // MOE_FFN_ROWS

        def full(c, carry):
            ffn_rows(start + c * MOE_FFN_ROWS, MOE_FFN_ROWS, None)
            return carry

        lax.fori_loop(0, n_full, full, 0)
        rem = n - n_full * MOE_FFN_ROWS
        tail = start + n_full * MOE_FFN_ROWS
        for size in range(LANES, MOE_FFN_ROWS + 1, LANES):
            @pl.when((rem > size - LANES) & (rem <= size))
            def _(size=size):
                ffn_rows(tail, size, rem)

    @pl.when(s >= steps_per_unit + n_exp)
    def _():
        t = s - steps_per_unit - n_exp
        for k in range(tiles):
            def copy_rows(unit_row, stage_row, rows, k=k):
                stg_ref[k, pl.ds(stage_row, rows), :] = buf_ref[pl.ds(unit_row, rows), :]

            _moe_segment_copies((u * steps_per_unit + t) * tiles + k, n_exp, seg_start, seg_len,
                                stage_base, copy_rows)
        outs = []
        for k in range(tiles):
            place = _moe_placement(meta_ref[k * tm:(k + 1) * tm, :], n_stage, weighted=True)
            f = jnp.dot(place, stg_ref[k], preferred_element_type=F32)
            outs.append(_layer_norm(alpha * x_ref[k * tm:(k + 1) * tm, :] + f, g_ref[...], b_ref[...]))
        o_ref[...] = jnp.concatenate(outs, axis=0)


def _moe(x, wr_hi, wr_lo, br, w_up, w_dn, g, b, alpha, n_grp):
    n, d = x.shape
    n_exp, _, two_f = w_up.shape
    hidden = two_f // 2
    tm = MOE_TILE
    unit = min(MOE_UNIT, n)
    tpu = unit // tm
    n_tiles, n_units = n // tm, n // unit
    worst_pad = MOE_ALIGN - 1
    n_stage = -(-(2 * tm + n_exp * worst_pad) // (2 * LANES)) * (2 * LANES)
    cap = -(-(2 * unit + n_exp * tpu * worst_pad) // MOE_FFN_ROWS) * MOE_FFN_ROWS + MOE_FFN_ROWS

    meta, cnt = _moe_route(x, wr_hi, wr_lo, br, n_exp, n_grp)
    cnt = cnt[:, 0, :n_exp].astype(jnp.int32)
    seg_len = (cnt + worst_pad) // MOE_ALIGN * MOE_ALIGN
    stage_base = jnp.cumsum(seg_len, axis=1) - seg_len
    sl = seg_len.reshape(n_units, tpu, n_exp)
    unit_len = jnp.sum(sl, axis=1)
    off = jnp.concatenate([jnp.zeros((n_units, 1), jnp.int32), jnp.cumsum(unit_len, axis=1)], axis=1)
    seg_start = off[:, None, :n_exp] + jnp.cumsum(sl, axis=1) - sl
    base_of = jnp.repeat(stage_base.astype(F32), tm, axis=0)
    experts = jnp.arange(n_exp, dtype=F32)
    lane = jnp.arange(LANES)
    for e_idx, r_idx in ((_META_E1, _META_R1), (_META_E2, _META_R2)):
        base = jnp.sum(jnp.where(meta[:, e_idx:e_idx + 1] == experts, base_of, 0.0), axis=1, keepdims=True)
        meta = meta + jnp.where(lane == r_idx, base, 0.0)
    tables = (seg_start.reshape(-1), seg_len.reshape(-1), stage_base.reshape(-1), off.reshape(-1))

    tiles = min(MOE_STEP_TILES, tpu)
    spu = tpu // tiles

    def block_of(u, s):
        t = jnp.where(s < spu, s, jnp.where(s >= spu + n_exp, s - spu - n_exp, spu - 1))
        return u * spu + t

    tile_row = lambda u, s, *_: (block_of(u, s), 0)
    expert = lambda u, s, *_: (jnp.clip(s - spu, 0, n_exp - 1), 0, 0)
    out_row = lambda u, s, *_: (u * spu + jnp.clip(s - spu - n_exp, 0, spu - 1), 0)
    const = lambda u, s, *_: (0, 0)

    return pl.pallas_call(
        functools.partial(_moe_experts_kernel, alpha=alpha, n_exp=n_exp, steps_per_unit=spu,
                          hidden=hidden),
        grid_spec=pltpu.PrefetchScalarGridSpec(
            num_scalar_prefetch=4, grid=(n_units, 2 * spu + n_exp),
            in_specs=[pl.BlockSpec((tiles * tm, d), tile_row), pl.BlockSpec((tiles * tm, LANES), tile_row),
                      pl.BlockSpec((1, d, two_f), expert), pl.BlockSpec((1, hidden, d), expert),
                      pl.BlockSpec((1, d), const), pl.BlockSpec((1, d), const)],
            out_specs=pl.BlockSpec((tiles * tm, d), out_row),
            scratch_shapes=[pltpu.VMEM((cap, d), BF16), pltpu.VMEM((tiles, n_stage, d), BF16)]),
        out_shape=jax.ShapeDtypeStruct((n, d), F32),
        compiler_params=_cparams("arbitrary", "arbitrary"),
        name="moe_experts",
    )(*tables, x, meta, w_up, w_dn, g, b)


def _router_params(w_group, b_group, w_router, b_router):
    d, n_grp = w_group.shape
    n_exp = w_router.shape[1]
    w = jnp.zeros((d, LANES), F32).at[:, :n_exp].set(w_router).at[:, n_exp:n_exp + n_grp].set(w_group)
    hi = w.astype(BF16)
    lo = (w - hi.astype(F32)).astype(BF16)
    br = jnp.zeros((1, LANES), F32).at[0, :n_exp].set(b_router).at[0, n_exp:n_exp + n_grp].set(b_group)
    return hi, lo, br


def _sgu_mix_params(w_s, b_s, t_len, gd):
    groups = w_s.shape[0]
    L = min(t_len, SGU_LEN)
    pos = jnp.arange(L)
    allowed = (pos[None, :] // CHUNK) <= (pos[:, None] // CHUNK)
    w = jnp.where(allowed[None], w_s[:, :L, :L], 0.0)
    rep = SGU_LEN // L
    if rep > 1:
        eye = jnp.eye(rep, dtype=F32)
        w = jnp.einsum("ab,gts->gatbs", eye, w).reshape(groups, SGU_LEN, SGU_LEN)
    bias = jnp.tile(b_s[:, :L], (1, rep))
    bias = jnp.broadcast_to(bias[:, :, None], (groups, SGU_LEN, gd))
    return w.astype(BF16), bias.astype(F32)


def kernel(x_prompt, x_sample, cache_sb_k, cache_sb_v, state_gla, state_pool, sb_w_qkv, sb_w_o, sgu_w_in, sgu_ln_g, sgu_ln_b, sgu_w_s, sgu_b_s, sgu_w_out, gla_w_in, gla_w_g1, gla_w_g2, gla_b_g, gla_norm_g, gla_w_o, pool_w, pool_scale, ln1_g, ln1_b, ln2_g, ln2_b, moe_w_group, moe_b_group, moe_w_router, moe_b_router, moe_w_up, moe_w_down):
    depth = ln1_g.shape[0]
    d = x_prompt.shape[-1]
    alpha = (2.0 * depth) ** 0.25
    sb_heads, sb_hd = cache_sb_k.shape[3], cache_sb_k.shape[4]
    gla_heads, gla_dk, gla_dv = state_gla.shape[2], state_gla.shape[3], state_gla.shape[4]
    n_grp = moe_w_group.shape[-1]
    dq, dvt = gla_heads * gla_dk, gla_heads * gla_dv
    n_hist = state_pool.shape[2]
    sgu_groups = sgu_w_s.shape[1]
    sgu_width = sgu_w_out.shape[1]

    bf = lambda a: a.astype(BF16)
    vec = lambda a: a.reshape(1, -1)

    def run(x3, past):
        batch, seq, _ = x3.shape
        n = batch * seq
        x = x3.reshape(n, d)
        sb_k, sb_v, sgu_v, gla_s, pool_h = [], [], [], [], []
        for i in range(depth):
            m, j = i % 4, i // 4
            g1, b1 = vec(ln1_g[i]), vec(ln1_b[i])
            if m == 0:
                wq, wk, wv = (bf(sb_w_qkv[j][:, c * d:(c + 1) * d]) for c in range(3))
                from_t = lambda a: a.reshape(batch, sb_heads, sb_hd, -1).transpose(0, 3, 1, 2)
                to_t = lambda a: a.transpose(0, 2, 3, 1).reshape(batch, d, -1)
                if past is None:
                    q, kt, vt, ktb, vtb = _sb_qkv(x, wq, wk.T, wv.T, batch, seq, sb_hd ** -0.5)
                    o = _sb_prompt(q, ktb, vtb, batch, seq, sb_hd)
                    sb_k.append(from_t(kt))
                    sb_v.append(from_t(vt))
                else:
                    q, k, v, kb, vb = _proj(
                        x, [wq, wk, wv],
                        [(0, BF16, sb_hd ** -0.5), (1, F32, 1.0), (2, F32, 1.0), (1, BF16, 1.0), (2, BF16, 1.0)],
                        "sb_qkv")
                    o = _sb_sample(q, kb, vb, to_t(past[0][j]), to_t(past[1][j]), batch, seq, sb_hd)
                    sb_k.append(k.reshape(batch, seq, sb_heads, sb_hd))
                    sb_v.append(v.reshape(batch, seq, sb_heads, sb_hd))
                x = _mm_res_ln(o, bf(sb_w_o[j]), x, g1, b1, alpha, "sb_out")
            elif m == 1:
                ws_mix, bs_mix = _sgu_mix_params(sgu_w_s[j], sgu_b_s[j], seq, sgu_width // sgu_groups)
                x, v = _sgu(x, bf(sgu_w_in[j]), vec(sgu_ln_g[j]), vec(sgu_ln_b[j]), ws_mix, bs_mix,
                            bf(sgu_w_out[j]), g1, b1, alpha, past is not None)
                if v is not None:
                    sgu_v.append(v.reshape(batch, seq, sgu_width))
            elif m == 2:
                w_in = gla_w_in[j]
                wg1 = jnp.zeros((d, LANES), F32).at[:, :gla_w_g1.shape[-1]].set(gla_w_g1[j])
                wg2 = jnp.zeros((LANES, dq), F32).at[:gla_w_g2.shape[1]].set(gla_w_g2[j])
                q, k, v, r, la = _gla_proj(
                    x, bf(w_in[:, :dq]), bf(w_in[:, dq:2 * dq]), bf(w_in[:, 2 * dq:2 * dq + dvt]),
                    bf(w_in[:, 2 * dq + dvt:]), bf(wg1), bf(wg2), vec(gla_b_g[j]), gla_dk ** -0.5)
                if past is None:
                    s0 = jnp.zeros((batch, gla_heads, gla_dk, gla_dv), F32)
                    seq_p = seq
                else:
                    s0 = past[2][j]
                    seq_p = -(-seq // SGU_LEN) * SGU_LEN
                    padrows = lambda a: jnp.pad(a.reshape(batch, seq, -1),
                                                ((0, 0), (0, seq_p - seq), (0, 0))).reshape(batch * seq_p, -1)
                    q, k, v, la = padrows(q), padrows(k), padrows(v), padrows(la)
                o, s = _gla_scan(q, k, v, la, s0, batch, seq_p, gla_heads)
                if seq_p != seq:
                    o = o.reshape(batch, seq_p, dvt)[:, :seq].reshape(n, dvt)
                gla_s.append(s)
                x = _gla_out(o, r, vec(gla_norm_g[j]), bf(gla_w_o[j]), x, g1, b1, alpha, gla_heads)
            else:
                if past is None:
                    hist = jnp.zeros((batch, POOL_PAD, d), F32)
                    n_valid = 0
                else:
                    hist = jnp.pad(past[3][j], ((0, 0), (POOL_PAD - n_hist, 0), (0, 0)))
                    n_valid = n_hist
                x_in = x.reshape(batch, seq, d)
                if past is None:
                    pool_h.append(x_in[:, seq - n_hist:])
                else:
                    pool_h.append(jnp.concatenate([past[3][j], x_in], axis=1)[:, -n_hist:])
                x = _pool(x, hist, bf(pool_w[j]), vec(pool_scale[j]), g1, b1, alpha, batch, seq, n_valid)
            wr_hi, wr_lo, br = _router_params(moe_w_group[i], moe_b_group[i], moe_w_router[i], moe_b_router[i])
            x = _moe(x, wr_hi, wr_lo, br, bf(moe_w_up[i]), bf(moe_w_down[i]),
                     vec(ln2_g[i]), vec(ln2_b[i]), alpha, n_grp)
        return x.reshape(batch, seq, d), sb_k, sb_v, sgu_v, gla_s, pool_h

    y_p, kp, vp, _, sp, hp = run(x_prompt, None)
    y_s, ks, vs, us, ss, hs = run(x_sample, (cache_sb_k, cache_sb_v, state_gla, state_pool))
    return (y_p, y_s, jnp.stack(kp), jnp.stack(vp), jnp.stack(ks), jnp.stack(vs), jnp.stack(us),
            jnp.stack(sp), jnp.stack(ss), jnp.stack(hp), jnp.stack(hs))
```

```python
import functools
import math

import jax
import jax.numpy as jnp
from jax import lax
from jax.experimental import pallas as pl
from jax.experimental.pallas import tpu as pltpu

F32 = jnp.float32
BF16 = jnp.bfloat16

LN_EPS = 1e-5
CHUNK = 64
SGU_LEN = 128
SB_QBLOCK = 128
SB_SPAN_BLOCKS = 3
SB_LANE_GROUPS = 4
SB_UNDERFLOW = 104.0
SB_MASKED = 1.0e4
GLA_BLOCK = 16
GLA_TAU = 16.0
POOL_WINDOWS = (2, 4, 8, 16)
POOL_PAD = 16
MOE_EXPERTS_PER_GROUP = 4
MOE_TILE = 256
MOE_ROUTE_TILES = 2
MOE_STEP_TILES = 2
MOE_UNIT = 4096
MOE_ALIGN = 16
MOE_FFN_ROWS = 512
LANES = 128
VMEM_LIMIT = 56 * 1024 * 1024


def _cparams(*sem):
    return pltpu.CompilerParams(dimension_semantics=sem, vmem_limit_bytes=VMEM_LIMIT)


def _tile(n, pref):
    t = min(n, pref)
    while n % t:
        t -= 8
    return t


def _layer_norm(x, g, b):
    mu = jnp.mean(x, axis=-1, keepdims=True)
    xc = x - mu
    var = jnp.mean(xc * xc, axis=-1, keepdims=True)
    return xc * lax.rsqrt(var + LN_EPS) * g + b


def _split_bf16(x):
    hi = x.astype(BF16)
    lo = (x - hi.astype(F32)).astype(BF16)
    return hi, lo


def _log_sigmoid(z):
    return jnp.minimum(z, 0.0) - jnp.log(1.0 + jnp.exp(-jnp.abs(z)))


def _silu(x):
    return x / (1.0 + jnp.exp(-x))


def _proj_kernel(x_ref, *refs, outs):
    n_w = len(refs) - len(outs)
    xb = x_ref[...].astype(BF16)
    ys = [jnp.dot(xb, w_ref[...], preferred_element_type=F32) for w_ref in refs[:n_w]]
    for (wi, _, scale), o_ref in zip(outs, refs[n_w:]):
        o_ref[...] = (ys[wi] * scale).astype(o_ref.dtype)


def _proj(x, ws, outs, name):
    n, k = x.shape
    tm = _tile(n, 512)
    return pl.pallas_call(
        functools.partial(_proj_kernel, outs=tuple(outs)),
        grid=(n // tm,),
        in_specs=[pl.BlockSpec((tm, k), lambda i: (i, 0))]
        + [pl.BlockSpec(w.shape, lambda i: (0, 0)) for w in ws],
        out_specs=[pl.BlockSpec((tm, ws[wi].shape[1]), lambda i: (i, 0)) for wi, _, _ in outs],
        out_shape=[jax.ShapeDtypeStruct((n, ws[wi].shape[1]), dt) for wi, dt, _ in outs],
        compiler_params=_cparams("parallel"),
        name=name,
    )(x, *ws)


def _mm_res_ln_kernel(a_ref, w_ref, res_ref, g_ref, b_ref, o_ref, *, alpha):
    y = jnp.dot(a_ref[...].astype(BF16), w_ref[...], preferred_element_type=F32)
    o_ref[...] = _layer_norm(alpha * res_ref[...] + y, g_ref[...], b_ref[...])


def _mm_res_ln(a, w, res, g, b, alpha, name):
    n, k = a.shape
    d = w.shape[1]
    tm = _tile(n, 512)
    return pl.pallas_call(
        functools.partial(_mm_res_ln_kernel, alpha=alpha),
        grid=(n // tm,),
        in_specs=[pl.BlockSpec((tm, k), lambda i: (i, 0)),
                  pl.BlockSpec((k, d), lambda i: (0, 0)),
                  pl.BlockSpec((tm, d), lambda i: (i, 0)),
                  pl.BlockSpec((1, d), lambda i: (0, 0)),
                  pl.BlockSpec((1, d), lambda i: (0, 0))],
        out_specs=pl.BlockSpec((tm, d), lambda i: (i, 0)),
        out_shape=jax.ShapeDtypeStruct((n, d), F32),
        compiler_params=_cparams("parallel"),
        name=name,
    )(a, w, res, g, b)


def _sb_later_selector(span):
    s = jnp.arange(span)
    return (s[:, None] > s[None, :]).astype(BF16)


def _sb_stack_heads(q, head_dim):
    lane = lax.broadcasted_iota(jnp.int32, q.shape, 1)
    parts = [jnp.where((lane >= h * head_dim) & (lane < (h + 1) * head_dim), q, jnp.zeros_like(q))
             for h in range(LANES // head_dim)]
    return jnp.concatenate(parts, axis=0)


def _sb_unstack_heads(acc, tq, head_dim):
    lane = lax.broadcasted_iota(jnp.int32, (tq, LANES), 1)
    out = acc[:tq]
    for h in range(1, LANES // head_dim):
        out = jnp.where(lane >= h * head_dim, acc[h * tq:(h + 1) * tq], out)
    return out


def _sb_later_sums(lstay, u_ref, off):
    n = lstay.shape[1]
    terms = lstay.astype(BF16)
    cols = []
    for c0 in range(0, n, 2 * LANES):
        c1 = min(c0 + 2 * LANES, n)
        cols.append(jnp.dot(terms[:, c0:], u_ref[off + c0:off + n, off + c0:off + c1],
                            preferred_element_type=F32))
    return cols[0] if len(cols) == 1 else jnp.concatenate(cols, axis=1)


def _sb_weights(z, valid, u_ref, carry):
    if valid is not None:
        z = jnp.where(valid, z, -SB_MASKED)
    ls = _log_sigmoid(z)
    lstay = ls - z
    later = _sb_later_sums(lstay, u_ref, 0)
    if carry is not None:
        later = later + carry
    w = jnp.exp(ls + later)
    return w.astype(BF16), jnp.sum(lstay, axis=-1, keepdims=True)


def _dot_t(a, bt):
    return lax.dot_general(a, bt, (((1,), (1,)), ((), ())), preferred_element_type=F32)


def _sb_qkv_kernel(x_ref, wq_ref, wkt_ref, wvt_ref, q_ref, kt_ref, vt_ref, ktb_ref, vtb_ref, *, qscale):
    xb = x_ref[...].astype(BF16)
    q_ref[...] = (jnp.dot(xb, wq_ref[...], preferred_element_type=F32) * qscale).astype(BF16)
    kt = _dot_t(wkt_ref[...], xb)
    kt_ref[...] = kt
    ktb_ref[...] = kt.astype(BF16)
    vt = _dot_t(wvt_ref[...], xb)
    vt_ref[...] = vt
    vtb_ref[...] = vt.astype(BF16)


def _sb_qkv(x, wq, wkt, wvt, batch, seq, qscale):
    n, d = x.shape
    tm = _tile(seq, 512)
    nt = seq // tm
    row = lambda b, t: (b * nt + t, 0)
    const = lambda b, t: (0, 0)
    col = pl.BlockSpec((None, d, tm), lambda b, t: (b, 0, t))
    return pl.pallas_call(
        functools.partial(_sb_qkv_kernel, qscale=qscale),
        grid=(batch, nt),
        in_specs=[pl.BlockSpec((tm, d), row), pl.BlockSpec((d, d), const),
                  pl.BlockSpec((d, d), const), pl.BlockSpec((d, d), const)],
        out_specs=[pl.BlockSpec((tm, d), row), col, col, col, col],
        out_shape=[jax.ShapeDtypeStruct((n, d), BF16),
                   jax.ShapeDtypeStruct((batch, d, seq), F32), jax.ShapeDtypeStruct((batch, d, seq), F32),
                   jax.ShapeDtypeStruct((batch, d, seq), BF16), jax.ShapeDtypeStruct((batch, d, seq), BF16)],
        compiler_params=_cparams("parallel", "parallel"),
        name="sb_qkv",
    )(x, wq, wkt, wvt)


def _sb_prompt_kernel(q_ref, *refs, head_dim, n_span):
    kt_blocks, vt_blocks = refs[:n_span], refs[n_span:2 * n_span]
    kt_hbm, vt_hbm, u_ref, o_ref, kbuf, vbuf, sem = refs[2 * n_span:]
    b, hp, qi = pl.program_id(0), pl.program_id(1), pl.program_id(2)
    tq = q_ref.shape[0]
    width = q_ref.shape[1]
    rows = tq * (LANES // head_dim)
    span = n_span * SB_QBLOCK
    first = jnp.maximum(qi - (n_span - 1), 0)
    row = lax.broadcasted_iota(jnp.int32, (rows, span), 0)
    col = lax.broadcasted_iota(jnp.int32, (rows, span), 1)
    valid = (first * SB_QBLOCK + col) < (qi * tq + (row & (tq - 1)))
    n_groups = width // LANES
    qss, accs, carries = [], [], []
    for g in range(n_groups):
        ls = slice(g * LANES, (g + 1) * LANES)
        qs = _sb_stack_heads(q_ref[:, ls], head_dim)
        kt = jnp.concatenate([r[ls, :] for r in kt_blocks], axis=1)
        vt = jnp.concatenate([r[ls, :] for r in vt_blocks], axis=1)
        w, tot = _sb_weights(jnp.dot(qs, kt, preferred_element_type=F32), valid, u_ref, None)
        qss.append(qs)
        accs.append(_dot_t(w, vt))
        carries.append(tot)

    def cond(state):
        j, _, carries = state
        worst = functools.reduce(jnp.maximum, [jnp.max(c) for c in carries])
        return (j >= 0) & (worst > -SB_UNDERFLOW)

    def body(state):
        j, accs, carries = state
        copies = []
        for g in range(n_groups):
            src = (b, pl.ds(hp * width + g * LANES, LANES), pl.ds(j * SB_QBLOCK, SB_QBLOCK))
            copies.append(pltpu.make_async_copy(kt_hbm.at[src], kbuf.at[g], sem.at[g, 0]))
            copies.append(pltpu.make_async_copy(vt_hbm.at[src], vbuf.at[g], sem.at[g, 1]))
        for c in copies:
            c.start()
        for c in copies:
            c.wait()
        new_accs, new_carries = [], []
        for g in range(n_groups):
            w, tot = _sb_weights(jnp.dot(qss[g], kbuf[g], preferred_element_type=F32), None, u_ref,
                                 carries[g])
            new_accs.append(accs[g] + _dot_t(w, vbuf[g]))
            new_carries.append(carries[g] + tot)
        return j - 1, tuple(new_accs), tuple(new_carries)

    _, accs, _ = lax.while_loop(cond, body, (first - 1, tuple(accs), tuple(carries)))
    o_ref[...] = jnp.concatenate([_sb_unstack_heads(a, tq, head_dim) for a in accs],
                                 axis=1).astype(o_ref.dtype)


def _sb_prompt(q, kt, vt, batch, seq, head_dim):
    n, d = q.shape
    nqb = seq // SB_QBLOCK
    n_span = min(SB_SPAN_BLOCKS, nqb)
    u = _sb_later_selector(n_span * SB_QBLOCK)
    width = SB_LANE_GROUPS * LANES

    def span_block(i):
        return pl.BlockSpec((None, width, SB_QBLOCK),
                            lambda b, hp, qi: (b, hp, jnp.maximum(qi - (n_span - 1), 0) + i))

    blocks = [span_block(i) for i in range(n_span)]
    q_spec = pl.BlockSpec((SB_QBLOCK, width), lambda b, hp, qi: (b * nqb + qi, hp))
    return pl.pallas_call(
        functools.partial(_sb_prompt_kernel, head_dim=head_dim, n_span=n_span),
        grid=(batch, d // width, nqb),
        in_specs=[q_spec] + blocks + blocks
        + [pl.BlockSpec(memory_space=pl.ANY), pl.BlockSpec(memory_space=pl.ANY),
           pl.BlockSpec(u.shape, lambda b, hp, qi: (0, 0))],
        out_specs=q_spec,
        out_shape=jax.ShapeDtypeStruct((n, d), BF16),
        scratch_shapes=[pltpu.VMEM((SB_LANE_GROUPS, LANES, SB_QBLOCK), BF16),
                        pltpu.VMEM((SB_LANE_GROUPS, LANES, SB_QBLOCK), BF16),
                        pltpu.SemaphoreType.DMA((SB_LANE_GROUPS, 2))],
        compiler_params=_cparams("parallel", "parallel", "arbitrary"),
        name="sb_prompt",
    )(q, *([kt] * n_span), *([vt] * n_span), kt, vt, u)


def _sb_sample_kernel(q_ref, kn_ref, vn_ref, kt_ref, vt_ref, kp_hbm, vp_hbm, u_ref, o_ref,
                      kbuf, vbuf, sem, *, head_dim, past):
    b, hp = pl.program_id(0), pl.program_id(1)
    tq = q_ref.shape[0]
    tail = kt_ref.shape[1]
    qs = _sb_stack_heads(q_ref[...], head_dim)
    rows = qs.shape[0]
    pad = jnp.zeros((SB_QBLOCK - tq, LANES), BF16)
    kn = jnp.concatenate([kn_ref[...], pad], axis=0)
    vn = jnp.concatenate([vn_ref[...], pad], axis=0)
    z = jnp.concatenate([jnp.dot(qs, kt_ref[...].astype(BF16), preferred_element_type=F32),
                         _dot_t(qs, kn)], axis=1)
    span = tail + SB_QBLOCK
    row = lax.broadcasted_iota(jnp.int32, (rows, span), 0)
    col = lax.broadcasted_iota(jnp.int32, (rows, span), 1)
    valid = col < tail + (row & (tq - 1))
    w, carry = _sb_weights(z, valid, u_ref, None)
    acc = (_dot_t(w[:, :tail], vt_ref[...].astype(BF16))
           + jnp.dot(w[:, tail:], vn, preferred_element_type=F32))

    def cond(state):
        j, _, carry = state
        return (j >= 0) & (jnp.max(carry) > -SB_UNDERFLOW)

    def body(state):
        j, acc, carry = state
        src = (b, pl.ds(hp * LANES, LANES), pl.ds(j * SB_QBLOCK, SB_QBLOCK))
        kc = pltpu.make_async_copy(kp_hbm.at[src], kbuf, sem.at[0])
        vc = pltpu.make_async_copy(vp_hbm.at[src], vbuf, sem.at[1])
        kc.start()
        vc.start()
        kc.wait()
        vc.wait()
        w, tot = _sb_weights(jnp.dot(qs, kbuf[...].astype(BF16), preferred_element_type=F32),
                             None, u_ref, carry)
        return j - 1, acc + _dot_t(w, vbuf[...].astype(BF16)), carry + tot

    _, acc, _ = lax.while_loop(cond, body, ((past - tail) // SB_QBLOCK - 1, acc, carry))
    o_ref[...] = _sb_unstack_heads(acc, tq, head_dim).astype(o_ref.dtype)


def _sb_sample(q, kn, vn, kpt, vpt, batch, tq, head_dim):
    n, d = q.shape
    past = kpt.shape[2]
    tail = min(SB_SPAN_BLOCKS - 1, past // SB_QBLOCK) * SB_QBLOCK
    u = _sb_later_selector(tail + SB_QBLOCK)
    new = pl.BlockSpec((tq, LANES), lambda b, hp: (b, hp))
    tail_spec = pl.BlockSpec((None, LANES, tail), lambda b, hp: (b, hp, past // tail - 1))
    return pl.pallas_call(
        functools.partial(_sb_sample_kernel, head_dim=head_dim, past=past),
        grid=(batch, d // LANES),
        in_specs=[new, new, new, tail_spec, tail_spec,
                  pl.BlockSpec(memory_space=pl.ANY), pl.BlockSpec(memory_space=pl.ANY),
                  pl.BlockSpec(u.shape, lambda b, hp: (0, 0))],
        out_specs=new,
        out_shape=jax.ShapeDtypeStruct((n, d), BF16),
        scratch_shapes=[pltpu.VMEM((LANES, SB_QBLOCK), F32), pltpu.VMEM((LANES, SB_QBLOCK), F32),
                        pltpu.SemaphoreType.DMA((2,))],
        compiler_params=_cparams("parallel", "parallel"),
        name="sb_sample",
    )(q, kn, vn, kpt, vpt, kpt, vpt, u)


def _gelu_tanh(x):
    c = math.sqrt(2.0 / math.pi)
    return 0.5 * x * (1.0 + jnp.tanh(c * (x + 0.044715 * (x * x * x))))


def _sgu_kernel(x_ref, win_ref, lng_ref, lnb_ref, ws_ref, bs_ref, wout_ref, g_ref, b_ref,
                o_ref, *v_out, alpha, width, groups):
    x = x_ref[...]
    z = _gelu_tanh(jnp.dot(x.astype(BF16), win_ref[...], preferred_element_type=F32))
    u = z[:, :width]
    v = _layer_norm(z[:, width:], lng_ref[...], lnb_ref[...])
    if v_out:
        v_out[0][...] = v
    tm = x.shape[0]
    gd = width // groups
    vb = v.astype(BF16)
    rows = []
    for c in range(tm // SGU_LEN):
        r0 = c * SGU_LEN
        cols = []
        for g in range(groups):
            mixed = jnp.dot(ws_ref[g], vb[r0:r0 + SGU_LEN, g * gd:(g + 1) * gd],
                            preferred_element_type=F32) + bs_ref[g]
            cols.append(mixed)
        rows.append(jnp.concatenate(cols, axis=1))
    mixed = rows[0] if len(rows) == 1 else jnp.concatenate(rows, axis=0)
    y = jnp.dot((u * mixed).astype(BF16), wout_ref[...], preferred_element_type=F32)
    o_ref[...] = _layer_norm(alpha * x + y, g_ref[...], b_ref[...])


def _sgu(x, w_in, ln_g, ln_b, ws_mix, bs_mix, w_out, g, b, alpha, want_v):
    n, d = x.shape
    width = w_out.shape[0]
    groups = ws_mix.shape[0]
    tm = _tile(n, 256)
    const2 = lambda i: (0, 0)
    const3 = lambda i: (0, 0, 0)
    out_specs = [pl.BlockSpec((tm, d), lambda i: (i, 0))]
    out_shape = [jax.ShapeDtypeStruct((n, d), F32)]
    if want_v:
        out_specs.append(pl.BlockSpec((tm, width), lambda i: (i, 0)))
        out_shape.append(jax.ShapeDtypeStruct((n, width), F32))
    res = pl.pallas_call(
        functools.partial(_sgu_kernel, alpha=alpha, width=width, groups=groups),
        grid=(n // tm,),
        in_specs=[pl.BlockSpec((tm, d), lambda i: (i, 0)),
                  pl.BlockSpec(w_in.shape, const2),
                  pl.BlockSpec((1, width), const2),
                  pl.BlockSpec((1, width), const2),
                  pl.BlockSpec(ws_mix.shape, const3),
                  pl.BlockSpec(bs_mix.shape, const3),
                  pl.BlockSpec(w_out.shape, const2),
                  pl.BlockSpec((1, d), const2),
                  pl.BlockSpec((1, d), const2)],
        out_specs=out_specs,
        out_shape=out_shape,
        compiler_params=_cparams("parallel"),
        name="sgu",
    )(x, w_in, ln_g, ln_b, ws_mix, bs_mix, w_out, g, b)
    return res[0], (res[1] if want_v else None)


def _gla_proj_kernel(x_ref, wq_ref, wk_ref, wv_ref, wr_ref, wg1_ref, wg2_ref, bg_ref,
                     q_ref, k_ref, v_ref, r_ref, la_ref, *, qscale):
    xb = x_ref[...].astype(BF16)
    q_ref[...] = jnp.dot(xb, wq_ref[...], preferred_element_type=F32) * qscale
    k_ref[...] = jnp.dot(xb, wk_ref[...], preferred_element_type=F32)
    v_ref[...] = jnp.dot(xb, wv_ref[...], preferred_element_type=F32).astype(BF16)
    r_ref[...] = jnp.dot(xb, wr_ref[...], preferred_element_type=F32)
    t = jnp.dot(xb, wg1_ref[...], preferred_element_type=F32)
    gate = jnp.dot(t.astype(BF16), wg2_ref[...], preferred_element_type=F32) + bg_ref[...]
    la_ref[...] = _log_sigmoid(gate) / GLA_TAU


def _gla_proj(x, wq, wk, wv, wr, wg1, wg2, bg, qscale):
    n, d = x.shape
    dq, dv = wq.shape[1], wv.shape[1]
    tm = _tile(n, 512)
    const = lambda i: (0, 0)
    row = lambda i: (i, 0)
    return pl.pallas_call(
        functools.partial(_gla_proj_kernel, qscale=qscale),
        grid=(n // tm,),
        in_specs=[pl.BlockSpec((tm, d), row)]
        + [pl.BlockSpec(w.shape, const) for w in (wq, wk, wv, wr, wg1, wg2, bg)],
        out_specs=[pl.BlockSpec((tm, dq), row), pl.BlockSpec((tm, dq), row),
                   pl.BlockSpec((tm, dv), row), pl.BlockSpec((tm, dv), row),
                   pl.BlockSpec((tm, dq), row)],
        out_shape=[jax.ShapeDtypeStruct((n, dq), F32), jax.ShapeDtypeStruct((n, dq), F32),
                   jax.ShapeDtypeStruct((n, dv), BF16), jax.ShapeDtypeStruct((n, dv), F32),
                   jax.ShapeDtypeStruct((n, dq), F32)],
        compiler_params=_cparams("parallel"),
        name="gla_proj",
    )(x, wq, wk, wv, wr, wg1, wg2, bg)


def _gla_scan_kernel(q_ref, k_ref, v_ref, la_ref, s0_ref, o_ref, sfin_ref, st_ref,
                     *, heads, dk, dv):
    ti = pl.program_id(1)
    tt = q_ref.shape[0]
    nsb = tt // GLA_BLOCK

    @pl.when(ti == 0)
    def _():
        for h in range(heads):
            st_ref[h] = s0_ref[0, h].T

    row = lax.broadcasted_iota(jnp.int32, (tt, tt), 0)
    col = lax.broadcasted_iota(jnp.int32, (tt, tt), 1)
    same = (row // GLA_BLOCK) == (col // GLA_BLOCK)
    tril = same & (col <= row)
    bdcat = jnp.concatenate([jnp.where(tril, 1.0, 0.0), jnp.where(same, 1.0, 0.0)],
                            axis=0).astype(BF16)
    rsub = lax.broadcasted_iota(jnp.int32, (tt, dk), 0) // GLA_BLOCK

    for h in range(heads):
        ks = slice(h * dk, (h + 1) * dk)
        vs = slice(h * dv, (h + 1) * dv)
        hi, lo = _split_bf16(la_ref[:, ks])
        ct = (jnp.dot(bdcat, hi, preferred_element_type=F32)
              + jnp.dot(bdcat, lo, preferred_element_type=F32))
        cum, tot = ct[:tt], ct[tt:]
        k = k_ref[:, ks]
        qd = (q_ref[:, ks] * jnp.exp(cum)).astype(BF16)
        kd = (k * jnp.exp(-cum)).astype(BF16)
        kt = k * jnp.exp(tot - cum)
        sc = lax.dot_general(qd, kd, (((1,), (1,)), ((), ())), preferred_element_type=F32)
        sc = jnp.where(tril, sc, 0.0).astype(BF16)
        vh = v_ref[:, vs]
        o_intra = jnp.dot(sc, vh, preferred_element_type=F32)
        ktcat = jnp.concatenate(
            [jnp.where(rsub == i, kt, 0.0).astype(BF16) for i in range(nsb)], axis=1)
        ut = lax.dot_general(vh, ktcat, (((0,), (0,)), ((), ())), preferred_element_type=F32)
        st = st_ref[h]
        parts = []
        for i in range(nsb):
            r0 = i * GLA_BLOCK
            parts.append(lax.dot_general(qd[r0:r0 + GLA_BLOCK], st.astype(BF16),
                                         (((1,), (1,)), ((), ())), preferred_element_type=F32))
            st = st * jnp.exp(tot[r0:r0 + 1, :]) + ut[:, i * dk:(i + 1) * dk]
        st_ref[h] = st
        o_ref[:, vs] = o_intra + jnp.concatenate(parts, axis=0)

    @pl.when(ti == pl.num_programs(1) - 1)
    def _():
        for h in range(heads):
            sfin_ref[0, h] = st_ref[h].T


def _gla_scan(q, k, v, la, s0, batch, seq, heads):
    n, dq = q.shape
    dvt = v.shape[1]
    dk, dv = dq // heads, dvt // heads
    tt = SGU_LEN
    nt = seq // tt
    row = lambda b, t: (b * nt + t, 0)
    st = lambda b, t: (b, 0, 0, 0)
    return pl.pallas_call(
        functools.partial(_gla_scan_kernel, heads=heads, dk=dk, dv=dv),
        grid=(batch, nt),
        in_specs=[pl.BlockSpec((tt, dq), row), pl.BlockSpec((tt, dq), row),
                  pl.BlockSpec((tt, dvt), row), pl.BlockSpec((tt, dq), row),
                  pl.BlockSpec((1, heads, dk, dv), st)],
        out_specs=[pl.BlockSpec((tt, dvt), row), pl.BlockSpec((1, heads, dk, dv), st)],
        out_shape=[jax.ShapeDtypeStruct((n, dvt), F32),
                   jax.ShapeDtypeStruct((batch, heads, dk, dv), F32)],
        scratch_shapes=[pltpu.VMEM((heads, dv, dk), F32)],
        compiler_params=_cparams("parallel", "arbitrary"),
        name="gla_scan",
    )(q, k, v, la, s0)


def _gla_out_kernel(o_ref, r_ref, ng_ref, wo_ref, x_ref, g_ref, b_ref, y_ref, *, alpha, heads):
    o = o_ref[...]
    dv = o.shape[1] // heads
    normed = []
    for h in range(heads):
        oh = o[:, h * dv:(h + 1) * dv]
        ms = jnp.mean(oh * oh, axis=-1, keepdims=True)
        normed.append(oh * lax.rsqrt(ms + LN_EPS) * ng_ref[...])
    gated = jnp.concatenate(normed, axis=1) * _silu(r_ref[...])
    y = jnp.dot(gated.astype(BF16), wo_ref[...], preferred_element_type=F32)
    y_ref[...] = _layer_norm(alpha * x_ref[...] + y, g_ref[...], b_ref[...])


def _gla_out(o, r, norm_g, w_o, x, g, b, alpha, heads):
    n, dvt = o.shape
    d = x.shape[1]
    tm = _tile(n, 512)
    row = lambda i: (i, 0)
    const = lambda i: (0, 0)
    return pl.pallas_call(
        functools.partial(_gla_out_kernel, alpha=alpha, heads=heads),
        grid=(n // tm,),
        in_specs=[pl.BlockSpec((tm, dvt), row), pl.BlockSpec((tm, dvt), row),
                  pl.BlockSpec(norm_g.shape, const), pl.BlockSpec(w_o.shape, const),
                  pl.BlockSpec((tm, d), row), pl.BlockSpec((1, d), const),
                  pl.BlockSpec((1, d), const)],
        out_specs=pl.BlockSpec((tm, d), row),
        out_shape=jax.ShapeDtypeStruct((n, d), F32),
        compiler_params=_cparams("parallel"),
        name="gla_out",
    )(o, r, norm_g, w_o, x, g, b)


def _pool_kernel(x_ref, prev_ref, hist_ref, w_ref, sc_ref, g_ref, b_ref, o_ref, buf_ref,
                 *, alpha, n_valid):
    ti = pl.program_id(1)
    tm, d = x_ref.shape
    x = x_ref[...]

    @pl.when(ti == 0)
    def _():
        buf_ref[0:POOL_PAD, :] = hist_ref[0]

    @pl.when(ti > 0)
    def _():
        buf_ref[0:POOL_PAD, :] = prev_ref[...]

    buf_ref[POOL_PAD:POOL_PAD + tm, :] = x
    gd = d // len(POOL_WINDOWS)
    pos = ti * tm + lax.broadcasted_iota(jnp.int32, (tm, 1), 0)
    ys = []
    for g, win in enumerate(POOL_WINDOWS):
        cs = slice(g * gd, (g + 1) * gd)
        acc = x[:, cs]
        for j in range(1, win):
            acc = acc + buf_ref[POOL_PAD - j:POOL_PAD - j + tm, cs]
        count = jnp.minimum(pos + 1 + n_valid, win).astype(F32)
        dlt = acc / count - x[:, cs]
        ys.append(jnp.dot(dlt.astype(BF16), w_ref[g], preferred_element_type=F32))
    y = jnp.concatenate(ys, axis=1) * sc_ref[...]
    o_ref[...] = _layer_norm(alpha * x + y, g_ref[...], b_ref[...])


def _pool(x, hist, w_pool, scale, g, b, alpha, batch, seq, n_valid):
    n, d = x.shape
    tm = _tile(seq, 256)
    nt = seq // tm
    ratio = tm // POOL_PAD
    const2 = lambda bb, t: (0, 0)
    row = lambda bb, t: (bb * nt + t, 0)
    return pl.pallas_call(
        functools.partial(_pool_kernel, alpha=alpha, n_valid=n_valid),
        grid=(batch, nt),
        in_specs=[pl.BlockSpec((tm, d), row),
                  pl.BlockSpec((POOL_PAD, d),
                               lambda bb, t: (jnp.maximum((bb * nt + t) * ratio - 1, 0), 0)),
                  pl.BlockSpec((1, POOL_PAD, d), lambda bb, t: (bb, 0, 0)),
                  pl.BlockSpec(w_pool.shape, lambda bb, t: (0, 0, 0)),
                  pl.BlockSpec((1, d), const2), pl.BlockSpec((1, d), const2),
                  pl.BlockSpec((1, d), const2)],
        out_specs=pl.BlockSpec((tm, d), row),
        out_shape=jax.ShapeDtypeStruct((n, d), F32),
        scratch_shapes=[pltpu.VMEM((POOL_PAD + tm, d), F32)],
        compiler_params=_cparams("parallel", "arbitrary"),
        name="pool",
    )(x, x, hist, w_pool, scale, g, b)


def _moe_route_math(x, wr_hi, wr_lo, br, n_exp, n_grp):
    hi, lo = _split_bf16(x)
    both = jnp.dot(hi, jnp.concatenate([wr_hi, wr_lo], axis=1), preferred_element_type=F32)
    logit = (both[:, :LANES] + both[:, LANES:]
             + jnp.dot(lo, wr_hi, preferred_element_type=F32)) + br
    lane_i = lax.broadcasted_iota(jnp.int32, logit.shape, 1)
    lane = lane_i.astype(F32)
    neg = jnp.float32(-jnp.inf)
    big = jnp.float32(LANES)
    is_grp = (lane_i >= n_exp) & (lane_i < n_exp + n_grp)
    gl = jnp.where(is_grp, logit, neg)
    gmax = jnp.max(gl, axis=-1, keepdims=True)
    g_sel = jnp.min(jnp.where(gl == gmax, lane, big), axis=-1, keepdims=True) - n_exp
    g_prob = 1.0 / jnp.sum(jnp.where(is_grp, jnp.exp(gl - gmax), 0.0), axis=-1, keepdims=True)
    lane_grp = (lane_i // MOE_EXPERTS_PER_GROUP).astype(F32)
    in_grp = (lane_i < n_exp) & (lane_grp == g_sel)
    el = jnp.where(in_grp, logit, neg)
    m1 = jnp.max(el, axis=-1, keepdims=True)
    i1 = jnp.min(jnp.where(el == m1, lane, big), axis=-1, keepdims=True)
    el2 = jnp.where(lane == i1, neg, el)
    m2 = jnp.max(el2, axis=-1, keepdims=True)
    i2 = jnp.min(jnp.where(el2 == m2, lane, big), axis=-1, keepdims=True)
    e2 = jnp.exp(m2 - m1)
    w1 = g_prob / (1.0 + e2)
    w2 = g_prob * e2 / (1.0 + e2)
    return i1, i2, w1, w2


_META_E1, _META_E2, _META_R1, _META_R2, _META_W1, _META_W2 = range(6)


def _moe_route_kernel(x_ref, wrh_ref, wrl_ref, br_ref, meta_ref, cnt_ref, *, n_exp, n_grp):
    tm = MOE_TILE
    lane = lax.broadcasted_iota(jnp.int32, (tm, LANES), 1).astype(F32)
    r = lax.broadcasted_iota(jnp.int32, (tm, tm), 0)
    c = lax.broadcasted_iota(jnp.int32, (tm, tm), 1)
    before = jnp.where(c < r, 1.0, 0.0).astype(BF16)
    metas, counts = [], []
    for t in range(x_ref.shape[0] // tm):
        i1, i2, w1, w2 = _moe_route_math(x_ref[t * tm:(t + 1) * tm, :], wrh_ref[...], wrl_ref[...],
                                         br_ref[...], n_exp, n_grp)
        m1 = jnp.where(lane == i1, 1.0, 0.0)
        m2 = jnp.where(lane == i2, 1.0, 0.0)
        both = m1 + m2
        rank = jnp.dot(before, both.astype(BF16), preferred_element_type=F32)
        r1 = jnp.sum(m1 * rank, axis=-1, keepdims=True)
        r2 = jnp.sum(m2 * rank, axis=-1, keepdims=True)
        meta = jnp.zeros((tm, LANES), F32)
        for idx, col in ((_META_E1, i1), (_META_E2, i2), (_META_R1, r1), (_META_R2, r2),
                         (_META_W1, w1), (_META_W2, w2)):
            meta = jnp.where(lane == idx, col, meta)
        metas.append(meta)
        counts.append(jnp.broadcast_to(jnp.sum(both, axis=0, keepdims=True), cnt_ref.shape[1:]))
    meta_ref[...] = jnp.concatenate(metas, axis=0)
    cnt_ref[...] = jnp.stack(counts, axis=0)


def _moe_route(x, wr_hi, wr_lo, br, n_exp, n_grp):
    n, d = x.shape
    tm = MOE_TILE
    tiles = min(MOE_ROUTE_TILES, n // tm)
    const = lambda i: (0, 0)
    return pl.pallas_call(
        functools.partial(_moe_route_kernel, n_exp=n_exp, n_grp=n_grp),
        grid=(n // (tiles * tm),),
        in_specs=[pl.BlockSpec((tiles * tm, d), lambda i: (i, 0)),
                  pl.BlockSpec(wr_hi.shape, const), pl.BlockSpec(wr_lo.shape, const),
                  pl.BlockSpec((1, LANES), const)],
        out_specs=[pl.BlockSpec((tiles * tm, LANES), lambda i: (i, 0)),
                   pl.BlockSpec((tiles, 8, LANES), lambda i: (i, 0, 0))],
        out_shape=[jax.ShapeDtypeStruct((n, LANES), F32),
                   jax.ShapeDtypeStruct((n // tm, 8, LANES), F32)],
        compiler_params=_cparams("parallel"),
        name="moe_route",
    )(x, wr_hi, wr_lo, br)


def _moe_placement(meta, n_stage, weighted):
    tm = meta.shape[0]
    col = lax.broadcasted_iota(jnp.int32, (tm, n_stage), 1).astype(F32)
    out = None
    for r_idx, w_idx in ((_META_R1, _META_W1), (_META_R2, _META_W2)):
        val = meta[:, w_idx:w_idx + 1] if weighted else 1.0
        term = jnp.where(col == meta[:, r_idx:r_idx + 1], val, 0.0)
        out = term if out is None else out + term
    return out.astype(BF16)


def _moe_segment_copies(tile, n_exp, seg_start, seg_len, stage_base, copy_rows):
    big = 2 * MOE_ALIGN
    for e in range(n_exp):
        idx = tile * n_exp + e
        src, dst, n = stage_base[idx], seg_start[idx], seg_len[idx]

        def piece(c, carry, src=src, dst=dst):
            copy_rows(pl.multiple_of(dst + c * big, MOE_ALIGN), pl.multiple_of(src + c * big, MOE_ALIGN), big)
            return carry

        lax.fori_loop(0, n // big, piece, 0)

        @pl.when(n % big != 0)
        def _(src=src, dst=dst, n=n):
            done = n // big * big
            copy_rows(pl.multiple_of(dst + done, MOE_ALIGN), pl.multiple_of(src + done, MOE_ALIGN), MOE_ALIGN)


def _moe_experts_kernel(seg_start, seg_len, stage_base, off,
                        x_ref, meta_ref, wup_ref, wdn_ref, g_ref, b_ref, o_ref,
                        buf_ref, stg_ref, *, alpha, n_exp, steps_per_unit, hidden):
    u, s = pl.program_id(0), pl.program_id(1)
    tiles, n_stage = stg_ref.shape[0], stg_ref.shape[1]
    tm = x_ref.shape[0] // tiles

    @pl.when((u == 0) & (s == 0))
    def _():
        buf_ref[...] = jnp.zeros_like(buf_ref)

    @pl.when(s < steps_per_unit)
    def _():
        for k in range(tiles):
            place = _moe_placement(meta_ref[k * tm:(k + 1) * tm, :], n_stage, weighted=False)
            xb = x_ref[k * tm:(k + 1) * tm, :].astype(BF16)
            for c0 in range(0, n_stage, 2 * LANES):
                rows = lax.dot_general(place[:, c0:c0 + 2 * LANES], xb, (((0,), (0,)), ((), ())),
                                       preferred_element_type=F32)
                stg_ref[k, c0:c0 + 2 * LANES, :] = rows.astype(BF16)
        for k in range(tiles):
            def copy_rows(unit_row, stage_row, rows, k=k):
                buf_ref[pl.ds(unit_row, rows), :] = stg_ref[k, pl.ds(stage_row, rows), :]

            _moe_segment_copies((u * steps_per_unit + s) * tiles + k, n_exp, seg_start, seg_len,
                                stage_base, copy_rows)

    @pl.when((s >= steps_per_unit) & (s < steps_per_unit + n_exp))
    def _():
        e = s - steps_per_unit
        start = off[u * (n_exp + 1) + e]
        n = off[u * (n_exp + 1) + e + 1] - start

        def ffn_rows(r0, size, n_valid):
            r0 = pl.multiple_of(r0, MOE_ALIGN)
            xs = buf_ref[pl.ds(r0, size), :]
            h = jnp.dot(xs, wup_ref[0], preferred_element_type=F32)
            act = _silu(h[:, :hidden]) * h[:, hidden:]
            y = jnp.dot(act.astype(BF16), wdn_ref[0], preferred_element_type=F32)
            if n_valid is not None:
                row = lax.broadcasted_iota(jnp.int32, y.shape, 0)
                y = jnp.where(row < n_valid, y, xs.astype(F32))
            buf_ref[pl.ds(r0, size), :] = y.astype(BF16)

        n_full = n // MOE_FFN_ROWS

        def full(c, carry):
            ffn_rows(start + c * MOE_FFN_ROWS, MOE_FFN_ROWS, None)
            return carry

        lax.fori_loop(0, n_full, full, 0)
        rem = n - n_full * MOE_FFN_ROWS
        tail = start + n_full * MOE_FFN_ROWS
        for size in range(LANES, MOE_FFN_ROWS + 1, LANES):
            @pl.when((rem > size - LANES) & (rem <= size))
            def _(size=size):
                ffn_rows(tail, size, rem)

    @pl.when(s >= steps_per_unit + n_exp)
    def _():
        t = s - steps_per_unit - n_exp
        for k in range(tiles):
            def copy_rows(unit_row, stage_row, rows, k=k):
                stg_ref[k, pl.ds(stage_row, rows), :] = buf_ref[pl.ds(unit_row, rows), :]

            _moe_segment_copies((u * steps_per_unit + t) * tiles + k, n_exp, seg_start, seg_len,
                                stage_base, copy_rows)
        outs = []
        for k in range(tiles):
            place = _moe_placement(meta_ref[k * tm:(k + 1) * tm, :], n_stage, weighted=True)
            f = jnp.dot(place, stg_ref[k], preferred_element_type=F32)
            outs.append(_layer_norm(alpha * x_ref[k * tm:(k + 1) * tm, :] + f, g_ref[...], b_ref[...]))
        o_ref[...] = jnp.concatenate(outs, axis=0)


def _moe(x, wr_hi, wr_lo, br, w_up, w_dn, g, b, alpha, n_grp):
    n, d = x.shape
    n_exp, _, two_f = w_up.shape
    hidden = two_f // 2
    tm = MOE_TILE
    unit = min(MOE_UNIT, n)
    tpu = unit // tm
    n_tiles, n_units = n // tm, n // unit
    worst_pad = MOE_ALIGN - 1
    n_stage = -(-(2 * tm + n_exp * worst_pad) // (2 * LANES)) * (2 * LANES)
    cap = -(-(2 * unit + n_exp * tpu * worst_pad) // MOE_FFN_ROWS) * MOE_FFN_ROWS + MOE_FFN_ROWS

    meta, cnt = _moe_route(x, wr_hi, wr_lo, br, n_exp, n_grp)
    cnt = cnt[:, 0, :n_exp].astype(jnp.int32)
    seg_len = (cnt + worst_pad) // MOE_ALIGN * MOE_ALIGN
    stage_base = jnp.cumsum(seg_len, axis=1) - seg_len
    sl = seg_len.reshape(n_units, tpu, n_exp)
    unit_len = jnp.sum(sl, axis=1)
    off = jnp.concatenate([jnp.zeros((n_units, 1), jnp.int32), jnp.cumsum(unit_len, axis=1)], axis=1)
    seg_start = off[:, None, :n_exp] + jnp.cumsum(sl, axis=1) - sl
    base_of = jnp.repeat(stage_base.astype(F32), tm, axis=0)
    experts = jnp.arange(n_exp, dtype=F32)
    lane = jnp.arange(LANES)
    for e_idx, r_idx in ((_META_E1, _META_R1), (_META_E2, _META_R2)):
        base = jnp.sum(jnp.where(meta[:, e_idx:e_idx + 1] == experts, base_of, 0.0), axis=1, keepdims=True)
        meta = meta + jnp.where(lane == r_idx, base, 0.0)
    tables = (seg_start.reshape(-1), seg_len.reshape(-1), stage_base.reshape(-1), off.reshape(-1))

    tiles = min(MOE_STEP_TILES, tpu)
    spu = tpu // tiles

    def block_of(u, s):
        t = jnp.where(s < spu, s, jnp.where(s >= spu + n_exp, s - spu - n_exp, spu - 1))
        return u * spu + t

    tile_row = lambda u, s, *_: (block_of(u, s), 0)
    expert = lambda u, s, *_: (jnp.clip(s - spu, 0, n_exp - 1), 0, 0)
    out_row = lambda u, s, *_: (u * spu + jnp.clip(s - spu - n_exp, 0, spu - 1), 0)
    const = lambda u, s, *_: (0, 0)

    return pl.pallas_call(
        functools.partial(_moe_experts_kernel, alpha=alpha, n_exp=n_exp, steps_per_unit=spu,
                          hidden=hidden),
        grid_spec=pltpu.PrefetchScalarGridSpec(
            num_scalar_prefetch=4, grid=(n_units, 2 * spu + n_exp),
            in_specs=[pl.BlockSpec((tiles * tm, d), tile_row), pl.BlockSpec((tiles * tm, LANES), tile_row),
                      pl.BlockSpec((1, d, two_f), expert), pl.BlockSpec((1, hidden, d), expert),
                      pl.BlockSpec((1, d), const), pl.BlockSpec((1, d), const)],
            out_specs=pl.BlockSpec((tiles * tm, d), out_row),
            scratch_shapes=[pltpu.VMEM((cap, d), BF16), pltpu.VMEM((tiles, n_stage, d), BF16)]),
        out_shape=jax.ShapeDtypeStruct((n, d), F32),
        compiler_params=_cparams("arbitrary", "arbitrary"),
        name="moe_experts",
    )(*tables, x, meta, w_up, w_dn, g, b)


def _router_params(w_group, b_group, w_router, b_router):
    d, n_grp = w_group.shape
    n_exp = w_router.shape[1]
    w = jnp.zeros((d, LANES), F32).at[:, :n_exp].set(w_router).at[:, n_exp:n_exp + n_grp].set(w_group)
    hi = w.astype(BF16)
    lo = (w - hi.astype(F32)).astype(BF16)
    br = jnp.zeros((1, LANES), F32).at[0, :n_exp].set(b_router).at[0, n_exp:n_exp + n_grp].set(b_group)
    return hi, lo, br


def _sgu_mix_params(w_s, b_s, t_len, gd):
    groups = w_s.shape[0]
    L = min(t_len, SGU_LEN)
    pos = jnp.arange(L)
    allowed = (pos[None, :] // CHUNK) <= (pos[:, None] // CHUNK)
    w = jnp.where(allowed[None], w_s[:, :L, :L], 0.0)
    rep = SGU_LEN // L
    if rep > 1:
        eye = jnp.eye(rep, dtype=F32)
        w = jnp.einsum("ab,gts->gatbs", eye, w).reshape(groups, SGU_LEN, SGU_LEN)
    bias = jnp.tile(b_s[:, :L], (1, rep))
    bias = jnp.broadcast_to(bias[:, :, None], (groups, SGU_LEN, gd))
    return w.astype(BF16), bias.astype(F32)


def kernel(x_prompt, x_sample, cache_sb_k, cache_sb_v, state_gla, state_pool, sb_w_qkv, sb_w_o, sgu_w_in, sgu_ln_g, sgu_ln_b, sgu_w_s, sgu_b_s, sgu_w_out, gla_w_in, gla_w_g1, gla_w_g2, gla_b_g, gla_norm_g, gla_w_o, pool_w, pool_scale, ln1_g, ln1_b, ln2_g, ln2_b, moe_w_group, moe_b_group, moe_w_router, moe_b_router, moe_w_up, moe_w_down):
    depth = ln1_g.shape[0]
    d = x_prompt.shape[-1]
    alpha = (2.0 * depth) ** 0.25
    sb_heads, sb_hd = cache_sb_k.shape[3], cache_sb_k.shape[4]
    gla_heads, gla_dk, gla_dv = state_gla.shape[2], state_gla.shape[3], state_gla.shape[4]
    n_grp = moe_w_group.shape[-1]
    dq, dvt = gla_heads * gla_dk, gla_heads * gla_dv
    n_hist = state_pool.shape[2]
    sgu_groups = sgu_w_s.shape[1]
    sgu_width = sgu_w_out.shape[1]

    bf = lambda a: a.astype(BF16)
    vec = lambda a: a.reshape(1, -1)

    def run(x3, past):
        batch, seq, _ = x3.shape
        n = batch * seq
        x = x3.reshape(n, d)
        sb_k, sb_v, sgu_v, gla_s, pool_h = [], [], [], [], []
        for i in range(depth):
            m, j = i % 4, i // 4
            g1, b1 = vec(ln1_g[i]), vec(ln1_b[i])
            if m == 0:
                wq, wk, wv = (bf(sb_w_qkv[j][:, c * d:(c + 1) * d]) for c in range(3))
                from_t = lambda a: a.reshape(batch, sb_heads, sb_hd, -1).transpose(0, 3, 1, 2)
                to_t = lambda a: a.transpose(0, 2, 3, 1).reshape(batch, d, -1)
                if past is None:
                    q, kt, vt, ktb, vtb = _sb_qkv(x, wq, wk.T, wv.T, batch, seq, sb_hd ** -0.5)
                    o = _sb_prompt(q, ktb, vtb, batch, seq, sb_hd)
                    sb_k.append(from_t(kt))
                    sb_v.append(from_t(vt))
                else:
                    q, k, v, kb, vb = _proj(
                        x, [wq, wk, wv],
                        [(0, BF16, sb_hd ** -0.5), (1, F32, 1.0), (2, F32, 1.0), (1, BF16, 1.0), (2, BF16, 1.0)],
                        "sb_qkv")
                    o = _sb_sample(q, kb, vb, to_t(past[0][j]), to_t(past[1][j]), batch, seq, sb_hd)
                    sb_k.append(k.reshape(batch, seq, sb_heads, sb_hd))
                    sb_v.append(v.reshape(batch, seq, sb_heads, sb_hd))
                x = _mm_res_ln(o, bf(sb_w_o[j]), x, g1, b1, alpha, "sb_out")
            elif m == 1:
                ws_mix, bs_mix = _sgu_mix_params(sgu_w_s[j], sgu_b_s[j], seq, sgu_width // sgu_groups)
                x, v = _sgu(x, bf(sgu_w_in[j]), vec(sgu_ln_g[j]), vec(sgu_ln_b[j]), ws_mix, bs_mix,
                            bf(sgu_w_out[j]), g1, b1, alpha, past is not None)
                if v is not None:
                    sgu_v.append(v.reshape(batch, seq, sgu_width))
            elif m == 2:
                w_in = gla_w_in[j]
                wg1 = jnp.zeros((d, LANES), F32).at[:, :gla_w_g1.shape[-1]].set(gla_w_g1[j])
                wg2 = jnp.zeros((LANES, dq), F32).at[:gla_w_g2.shape[1]].set(gla_w_g2[j])
                q, k, v, r, la = _gla_proj(
                    x, bf(w_in[:, :dq]), bf(w_in[:, dq:2 * dq]), bf(w_in[:, 2 * dq:2 * dq + dvt]),
                    bf(w_in[:, 2 * dq + dvt:]), bf(wg1), bf(wg2), vec(gla_b_g[j]), gla_dk ** -0.5)
                if past is None:
                    s0 = jnp.zeros((batch, gla_heads, gla_dk, gla_dv), F32)
                    seq_p = seq
                else:
                    s0 = past[2][j]
                    seq_p = -(-seq // SGU_LEN) * SGU_LEN
                    padrows = lambda a: jnp.pad(a.reshape(batch, seq, -1),
                                                ((0, 0), (0, seq_p - seq), (0, 0))).reshape(batch * seq_p, -1)
                    q, k, v, la = padrows(q), padrows(k), padrows(v), padrows(la)
                o, s = _gla_scan(q, k, v, la, s0, batch, seq_p, gla_heads)
                if seq_p != seq:
                    o = o.reshape(batch, seq_p, dvt)[:, :seq].reshape(n, dvt)
                gla_s.append(s)
                x = _gla_out(o, r, vec(gla_norm_g[j]), bf(gla_w_o[j]), x, g1, b1, alpha, gla_heads)
            else:
                if past is None:
                    hist = jnp.zeros((batch, POOL_PAD, d), F32)
                    n_valid = 0
                else:
                    hist = jnp.pad(past[3][j], ((0, 0), (POOL_PAD - n_hist, 0), (0, 0)))
                    n_valid = n_hist
                x_in = x.reshape(batch, seq, d)
                if past is None:
                    pool_h.append(x_in[:, seq - n_hist:])
                else:
                    pool_h.append(jnp.concatenate([past[3][j], x_in], axis=1)[:, -n_hist:])
                x = _pool(x, hist, bf(pool_w[j]), vec(pool_scale[j]), g1, b1, alpha, batch, seq, n_valid)
            wr_hi, wr_lo, br = _router_params(moe_w_group[i], moe_b_group[i], moe_w_router[i], moe_b_router[i])
            x = _moe(x, wr_hi, wr_lo, br, bf(moe_w_up[i]), bf(moe_w_down[i]),
                     vec(ln2_g[i]), vec(ln2_b[i]), alpha, n_grp)
        return x.reshape(batch, seq, d), sb_k, sb_v, sgu_v, gla_s, pool_h

    y_p, kp, vp, _, sp, hp = run(x_prompt, None)
    y_s, ks, vs, us, ss, hs = run(x_sample, (cache_sb_k, cache_sb_v, state_gla, state_pool))
    return (y_p, y_s, jnp.stack(kp), jnp.stack(vp), jnp.stack(ks), jnp.stack(vs), jnp.stack(us),
            jnp.stack(sp), jnp.stack(ss), jnp.stack(hp), jnp.stack(hs))
```

```python
import functools
import math

import jax
import jax.numpy as jnp
from jax import lax
from jax.experimental import pallas as pl
from jax.experimental.pallas import tpu as pltpu

F32 = jnp.float32
BF16 = jnp.bfloat16

LN_EPS = 1e-5
CHUNK = 64
SGU_LEN = 128
SB_QBLOCK = 128
SB_SPAN_BLOCKS = 3
SB_LANE_GROUPS = 8
SB_UNDERFLOW = 104.0
SB_MASKED = 1.0e4
GLA_BLOCK = 16
GLA_TAU = 16.0
POOL_WINDOWS = (2, 4, 8, 16)
POOL_PAD = 16
MOE_EXPERTS_PER_GROUP = 4
MOE_TILE = 256
MOE_ROUTE_TILES = 2
MOE_STEP_TILES = 2
MOE_UNIT = 4096
MOE_ALIGN = 16
MOE_FFN_ROWS = 512
LANES = 128
VMEM_LIMIT = 56 * 1024 * 1024


def _cparams(*sem):
    return pltpu.CompilerParams(dimension_semantics=sem, vmem_limit_bytes=VMEM_LIMIT)


def _tile(n, pref):
    t = min(n, pref)
    while n % t:
        t -= 8
    return t


def _layer_norm(x, g, b):
    mu = jnp.mean(x, axis=-1, keepdims=True)
    xc = x - mu
    var = jnp.mean(xc * xc, axis=-1, keepdims=True)
    return xc * lax.rsqrt(var + LN_EPS) * g + b


def _split_bf16(x):
    hi = x.astype(BF16)
    lo = (x - hi.astype(F32)).astype(BF16)
    return hi, lo


def _log_sigmoid(z):
    return jnp.minimum(z, 0.0) - jnp.log(1.0 + jnp.exp(-jnp.abs(z)))


def _silu(x):
    return x / (1.0 + jnp.exp(-x))


def _proj_kernel(x_ref, *refs, outs):
    n_w = len(refs) - len(outs)
    xb = x_ref[...].astype(BF16)
    ys = [jnp.dot(xb, w_ref[...], preferred_element_type=F32) for w_ref in refs[:n_w]]
    for (wi, _, scale), o_ref in zip(outs, refs[n_w:]):
        o_ref[...] = (ys[wi] * scale).astype(o_ref.dtype)


def _proj(x, ws, outs, name):
    n, k = x.shape
    tm = _tile(n, 512)
    return pl.pallas_call(
        functools.partial(_proj_kernel, outs=tuple(outs)),
        grid=(n // tm,),
        in_specs=[pl.BlockSpec((tm, k), lambda i: (i, 0))]
        + [pl.BlockSpec(w.shape, lambda i: (0, 0)) for w in ws],
        out_specs=[pl.BlockSpec((tm, ws[wi].shape[1]), lambda i: (i, 0)) for wi, _, _ in outs],
        out_shape=[jax.ShapeDtypeStruct((n, ws[wi].shape[1]), dt) for wi, dt, _ in outs],
        compiler_params=_cparams("parallel"),
        name=name,
    )(x, *ws)


def _mm_res_ln_kernel(a_ref, w_ref, res_ref, g_ref, b_ref, o_ref, *, alpha):
    y = jnp.dot(a_ref[...].astype(BF16), w_ref[...], preferred_element_type=F32)
    o_ref[...] = _layer_norm(alpha * res_ref[...] + y, g_ref[...], b_ref[...])


def _mm_res_ln(a, w, res, g, b, alpha, name):
    n, k = a.shape
    d = w.shape[1]
    tm = _tile(n, 512)
    return pl.pallas_call(
        functools.partial(_mm_res_ln_kernel, alpha=alpha),
        grid=(n // tm,),
        in_specs=[pl.BlockSpec((tm, k), lambda i: (i, 0)),
                  pl.BlockSpec((k, d), lambda i: (0, 0)),
                  pl.BlockSpec((tm, d), lambda i: (i, 0)),
                  pl.BlockSpec((1, d), lambda i: (0, 0)),
                  pl.BlockSpec((1, d), lambda i: (0, 0))],
        out_specs=pl.BlockSpec((tm, d), lambda i: (i, 0)),
        out_shape=jax.ShapeDtypeStruct((n, d), F32),
        compiler_params=_cparams("parallel"),
        name=name,
    )(a, w, res, g, b)


def _sb_later_selector(span):
    s = jnp.arange(span)
    return (s[:, None] > s[None, :]).astype(BF16)


def _sb_stack_heads(q, head_dim):
    lane = lax.broadcasted_iota(jnp.int32, q.shape, 1)
    parts = [jnp.where((lane >= h * head_dim) & (lane < (h + 1) * head_dim), q, jnp.zeros_like(q))
             for h in range(LANES // head_dim)]
    return jnp.concatenate(parts, axis=0)


def _sb_unstack_heads(acc, tq, head_dim):
    lane = lax.broadcasted_iota(jnp.int32, (tq, LANES), 1)
    out = acc[:tq]
    for h in range(1, LANES // head_dim):
        out = jnp.where(lane >= h * head_dim, acc[h * tq:(h + 1) * tq], out)
    return out


def _sb_later_sums(lstay, u_ref, off):
    n = lstay.shape[1]
    terms = lstay.astype(BF16)
    cols = []
    for c0 in range(0, n, 2 * LANES):
        c1 = min(c0 + 2 * LANES, n)
        cols.append(jnp.dot(terms[:, c0:], u_ref[off + c0:off + n, off + c0:off + c1],
                            preferred_element_type=F32))
    return cols[0] if len(cols) == 1 else jnp.concatenate(cols, axis=1)


def _sb_weights(z, valid, u_ref, carry):
    if valid is not None:
        z = jnp.where(valid, z, -SB_MASKED)
    ls = _log_sigmoid(z)
    lstay = ls - z
    later = _sb_later_sums(lstay, u_ref, 0)
    if carry is not None:
        later = later + carry
    w = jnp.exp(ls + later)
    return w.astype(BF16), jnp.sum(lstay, axis=-1, keepdims=True)


def _dot_t(a, bt):
    return lax.dot_general(a, bt, (((1,), (1,)), ((), ())), preferred_element_type=F32)


def _sb_qkv_kernel(x_ref, wq_ref, wkt_ref, wvt_ref, q_ref, kt_ref, vt_ref, ktb_ref, vtb_ref, *, qscale):
    xb = x_ref[...].astype(BF16)
    q_ref[...] = (jnp.dot(xb, wq_ref[...], preferred_element_type=F32) * qscale).astype(BF16)
    kt = _dot_t(wkt_ref[...], xb)
    kt_ref[...] = kt
    ktb_ref[...] = kt.astype(BF16)
    vt = _dot_t(wvt_ref[...], xb)
    vt_ref[...] = vt
    vtb_ref[...] = vt.astype(BF16)


def _sb_qkv(x, wq, wkt, wvt, batch, seq, qscale):
    n, d = x.shape
    tm = _tile(seq, 512)
    nt = seq // tm
    row = lambda b, t: (b * nt + t, 0)
    const = lambda b, t: (0, 0)
    col = pl.BlockSpec((None, d, tm), lambda b, t: (b, 0, t))
    return pl.pallas_call(
        functools.partial(_sb_qkv_kernel, qscale=qscale),
        grid=(batch, nt),
        in_specs=[pl.BlockSpec((tm, d), row), pl.BlockSpec((d, d), const),
                  pl.BlockSpec((d, d), const), pl.BlockSpec((d, d), const)],
        out_specs=[pl.BlockSpec((tm, d), row), col, col, col, col],
        out_shape=[jax.ShapeDtypeStruct((n, d), BF16),
                   jax.ShapeDtypeStruct((batch, d, seq), F32), jax.ShapeDtypeStruct((batch, d, seq), F32),
                   jax.ShapeDtypeStruct((batch, d, seq), BF16), jax.ShapeDtypeStruct((batch, d, seq), BF16)],
        compiler_params=_cparams("parallel", "parallel"),
        name="sb_qkv",
    )(x, wq, wkt, wvt)


def _sb_prompt_kernel(q_ref, *refs, head_dim, n_span):
    kt_blocks, vt_blocks = refs[:n_span], refs[n_span:2 * n_span]
    kt_hbm, vt_hbm, u_ref, o_ref, kbuf, vbuf, sem = refs[2 * n_span:]
    b, hp, qi = pl.program_id(0), pl.program_id(1), pl.program_id(2)
    tq = q_ref.shape[0]
    width = q_ref.shape[1]
    rows = tq * (LANES // head_dim)
    span = n_span * SB_QBLOCK
    first = jnp.maximum(qi - (n_span - 1), 0)
    row = lax.broadcasted_iota(jnp.int32, (rows, span), 0)
    col = lax.broadcasted_iota(jnp.int32, (rows, span), 1)
    valid = (first * SB_QBLOCK + col) < (qi * tq + (row & (tq - 1)))
    n_groups = width // LANES
    qss, accs, carries = [], [], []
    for g in range(n_groups):
        ls = slice(g * LANES, (g + 1) * LANES)
        qs = _sb_stack_heads(q_ref[:, ls], head_dim)
        kt = jnp.concatenate([r[ls, :] for r in kt_blocks], axis=1)
        vt = jnp.concatenate([r[ls, :] for r in vt_blocks], axis=1)
        w, tot = _sb_weights(jnp.dot(qs, kt, preferred_element_type=F32), valid, u_ref, None)
        qss.append(qs)
        accs.append(_dot_t(w, vt))
        carries.append(tot)

    def cond(state):
        j, _, carries = state
        worst = functools.reduce(jnp.maximum, [jnp.max(c) for c in carries])
        return (j >= 0) & (worst > -SB_UNDERFLOW)

    def body(state):
        j, accs, carries = state
        copies = []
        for g in range(n_groups):
            src = (b, pl.ds(hp * width + g * LANES, LANES), pl.ds(j * SB_QBLOCK, SB_QBLOCK))
            copies.append(pltpu.make_async_copy(kt_hbm.at[src], kbuf.at[g], sem.at[g, 0]))
            copies.append(pltpu.make_async_copy(vt_hbm.at[src], vbuf.at[g], sem.at[g, 1]))
        for c in copies:
            c.start()
        for c in copies:
            c.wait()
        new_accs, new_carries = [], []
        for g in range(n_groups):
            w, tot = _sb_weights(jnp.dot(qss[g], kbuf[g], preferred_element_type=F32), None, u_ref,
                                 carries[g])
            new_accs.append(accs[g] + _dot_t(w, vbuf[g]))
            new_carries.append(carries[g] + tot)
        return j - 1, tuple(new_accs), tuple(new_carries)

    _, accs, _ = lax.while_loop(cond, body, (first - 1, tuple(accs), tuple(carries)))
    o_ref[...] = jnp.concatenate([_sb_unstack_heads(a, tq, head_dim) for a in accs],
                                 axis=1).astype(o_ref.dtype)


def _sb_prompt(q, kt, vt, batch, seq, head_dim):
    n, d = q.shape
    nqb = seq // SB_QBLOCK
    n_span = min(SB_SPAN_BLOCKS, nqb)
    u = _sb_later_selector(n_span * SB_QBLOCK)
    width = SB_LANE_GROUPS * LANES

    def span_block(i):
        return pl.BlockSpec((None, width, SB_QBLOCK),
                            lambda b, hp, qi: (b, hp, jnp.maximum(qi - (n_span - 1), 0) + i))

    blocks = [span_block(i) for i in range(n_span)]
    q_spec = pl.BlockSpec((SB_QBLOCK, width), lambda b, hp, qi: (b * nqb + qi, hp))
    return pl.pallas_call(
        functools.partial(_sb_prompt_kernel, head_dim=head_dim, n_span=n_span),
        grid=(batch, d // width, nqb),
        in_specs=[q_spec] + blocks + blocks
        + [pl.BlockSpec(memory_space=pl.ANY), pl.BlockSpec(memory_space=pl.ANY),
           pl.BlockSpec(u.shape, lambda b, hp, qi: (0, 0))],
        out_specs=q_spec,
        out_shape=jax.ShapeDtypeStruct((n, d), BF16),
        scratch_shapes=[pltpu.VMEM((SB_LANE_GROUPS, LANES, SB_QBLOCK), BF16),
                        pltpu.VMEM((SB_LANE_GROUPS, LANES, SB_QBLOCK), BF16),
                        pltpu.SemaphoreType.DMA((SB_LANE_GROUPS, 2))],
        compiler_params=_cparams("parallel", "parallel", "arbitrary"),
        name="sb_prompt",
    )(q, *([kt] * n_span), *([vt] * n_span), kt, vt, u)


def _sb_sample_kernel(q_ref, kn_ref, vn_ref, kt_ref, vt_ref, kp_hbm, vp_hbm, u_ref, o_ref,
                      kbuf, vbuf, sem, *, head_dim, past):
    b, hp = pl.program_id(0), pl.program_id(1)
    tq = q_ref.shape[0]
    tail = kt_ref.shape[1]
    qs = _sb_stack_heads(q_ref[...], head_dim)
    rows = qs.shape[0]
    pad = jnp.zeros((SB_QBLOCK - tq, LANES), BF16)
    kn = jnp.concatenate([kn_ref[...], pad], axis=0)
    vn = jnp.concatenate([vn_ref[...], pad], axis=0)
    z = jnp.concatenate([jnp.dot(qs, kt_ref[...].astype(BF16), preferred_element_type=F32),
                         _dot_t(qs, kn)], axis=1)
    span = tail + SB_QBLOCK
    row = lax.broadcasted_iota(jnp.int32, (rows, span), 0)
    col = lax.broadcasted_iota(jnp.int32, (rows, span), 1)
    valid = col < tail + (row & (tq - 1))
    w, carry = _sb_weights(z, valid, u_ref, None)
    acc = (_dot_t(w[:, :tail], vt_ref[...].astype(BF16))
           + jnp.dot(w[:, tail:], vn, preferred_element_type=F32))

    def cond(state):
        j, _, carry = state
        return (j >= 0) & (jnp.max(carry) > -SB_UNDERFLOW)

    def body(state):
        j, acc, carry = state
        src = (b, pl.ds(hp * LANES, LANES), pl.ds(j * SB_QBLOCK, SB_QBLOCK))
        kc = pltpu.make_async_copy(kp_hbm.at[src], kbuf, sem.at[0])
        vc = pltpu.make_async_copy(vp_hbm.at[src], vbuf, sem.at[1])
        kc.start()
        vc.start()
        kc.wait()
        vc.wait()
        w, tot = _sb_weights(jnp.dot(qs, kbuf[...].astype(BF16), preferred_element_type=F32),
                             None, u_ref, carry)
        return j - 1, acc + _dot_t(w, vbuf[...].astype(BF16)), carry + tot

    _, acc, _ = lax.while_loop(cond, body, ((past - tail) // SB_QBLOCK - 1, acc, carry))
    o_ref[...] = _sb_unstack_heads(acc, tq, head_dim).astype(o_ref.dtype)


def _sb_sample(q, kn, vn, kpt, vpt, batch, tq, head_dim):
    n, d = q.shape
    past = kpt.shape[2]
    tail = min(SB_SPAN_BLOCKS - 1, past // SB_QBLOCK) * SB_QBLOCK
    u = _sb_later_selector(tail + SB_QBLOCK)
    new = pl.BlockSpec((tq, LANES), lambda b, hp: (b, hp))
    tail_spec = pl.BlockSpec((None, LANES, tail), lambda b, hp: (b, hp, past // tail - 1))
    return pl.pallas_call(
        functools.partial(_sb_sample_kernel, head_dim=head_dim, past=past),
        grid=(batch, d // LANES),
        in_specs=[new, new, new, tail_spec, tail_spec,
                  pl.BlockSpec(memory_space=pl.ANY), pl.BlockSpec(memory_space=pl.ANY),
                  pl.BlockSpec(u.shape, lambda b, hp: (0, 0))],
        out_specs=new,
        out_shape=jax.ShapeDtypeStruct((n, d), BF16),
        scratch_shapes=[pltpu.VMEM((LANES, SB_QBLOCK), F32), pltpu.VMEM((LANES, SB_QBLOCK), F32),
                        pltpu.SemaphoreType.DMA((2,))],
        compiler_params=_cparams("parallel", "parallel"),
        name="sb_sample",
    )(q, kn, vn, kpt, vpt, kpt, vpt, u)


def _gelu_tanh(x):
    c = math.sqrt(2.0 / math.pi)
    return 0.5 * x * (1.0 + jnp.tanh(c * (x + 0.044715 * (x * x * x))))


def _sgu_kernel(x_ref, win_ref, lng_ref, lnb_ref, ws_ref, bs_ref, wout_ref, g_ref, b_ref,
                o_ref, *v_out, alpha, width, groups):
    x = x_ref[...]
    z = _gelu_tanh(jnp.dot(x.astype(BF16), win_ref[...], preferred_element_type=F32))
    u = z[:, :width]
    v = _layer_norm(z[:, width:], lng_ref[...], lnb_ref[...])
    if v_out:
        v_out[0][...] = v
    tm = x.shape[0]
    gd = width // groups
    vb = v.astype(BF16)
    rows = []
    for c in range(tm // SGU_LEN):
        r0 = c * SGU_LEN
        cols = []
        for g in range(groups):
            mixed = jnp.dot(ws_ref[g], vb[r0:r0 + SGU_LEN, g * gd:(g + 1) * gd],
                            preferred_element_type=F32) + bs_ref[g]
            cols.append(mixed)
        rows.append(jnp.concatenate(cols, axis=1))
    mixed = rows[0] if len(rows) == 1 else jnp.concatenate(rows, axis=0)
    y = jnp.dot((u * mixed).astype(BF16), wout_ref[...], preferred_element_type=F32)
    o_ref[...] = _layer_norm(alpha * x + y, g_ref[...], b_ref[...])


def _sgu(x, w_in, ln_g, ln_b, ws_mix, bs_mix, w_out, g, b, alpha, want_v):
    n, d = x.shape
    width = w_out.shape[0]
    groups = ws_mix.shape[0]
    tm = _tile(n, 256)
    const2 = lambda i: (0, 0)
    const3 = lambda i: (0, 0, 0)
    out_specs = [pl.BlockSpec((tm, d), lambda i: (i, 0))]
    out_shape = [jax.ShapeDtypeStruct((n, d), F32)]
    if want_v:
        out_specs.append(pl.BlockSpec((tm, width), lambda i: (i, 0)))
        out_shape.append(jax.ShapeDtypeStruct((n, width), F32))
    res = pl.pallas_call(
        functools.partial(_sgu_kernel, alpha=alpha, width=width, groups=groups),
        grid=(n // tm,),
        in_specs=[pl.BlockSpec((tm, d), lambda i: (i, 0)),
                  pl.BlockSpec(w_in.shape, const2),
                  pl.BlockSpec((1, width), const2),
                  pl.BlockSpec((1, width), const2),
                  pl.BlockSpec(ws_mix.shape, const3),
                  pl.BlockSpec(bs_mix.shape, const3),
                  pl.BlockSpec(w_out.shape, const2),
                  pl.BlockSpec((1, d), const2),
                  pl.BlockSpec((1, d), const2)],
        out_specs=out_specs,
        out_shape=out_shape,
        compiler_params=_cparams("parallel"),
        name="sgu",
    )(x, w_in, ln_g, ln_b, ws_mix, bs_mix, w_out, g, b)
    return res[0], (res[1] if want_v else None)


def _gla_proj_kernel(x_ref, wq_ref, wk_ref, wv_ref, wr_ref, wg1_ref, wg2_ref, bg_ref,
                     q_ref, k_ref, v_ref, r_ref, la_ref, *, qscale):
    xb = x_ref[...].astype(BF16)
    q_ref[...] = jnp.dot(xb, wq_ref[...], preferred_element_type=F32) * qscale
    k_ref[...] = jnp.dot(xb, wk_ref[...], preferred_element_type=F32)
    v_ref[...] = jnp.dot(xb, wv_ref[...], preferred_element_type=F32).astype(BF16)
    r_ref[...] = jnp.dot(xb, wr_ref[...], preferred_element_type=F32)
    t = jnp.dot(xb, wg1_ref[...], preferred_element_type=F32)
    gate = jnp.dot(t.astype(BF16), wg2_ref[...], preferred_element_type=F32) + bg_ref[...]
    la_ref[...] = _log_sigmoid(gate) / GLA_TAU


def _gla_proj(x, wq, wk, wv, wr, wg1, wg2, bg, qscale):
    n, d = x.shape
    dq, dv = wq.shape[1], wv.shape[1]
    tm = _tile(n, 512)
    const = lambda i: (0, 0)
    row = lambda i: (i, 0)
    return pl.pallas_call(
        functools.partial(_gla_proj_kernel, qscale=qscale),
        grid=(n // tm,),
        in_specs=[pl.BlockSpec((tm, d), row)]
        + [pl.BlockSpec(w.shape, const) for w in (wq, wk, wv, wr, wg1, wg2, bg)],
        out_specs=[pl.BlockSpec((tm, dq), row), pl.BlockSpec((tm, dq), row),
                   pl.BlockSpec((tm, dv), row), pl.BlockSpec((tm, dv), row),
                   pl.BlockSpec((tm, dq), row)],
        out_shape=[jax.ShapeDtypeStruct((n, dq), F32), jax.ShapeDtypeStruct((n, dq), F32),
                   jax.ShapeDtypeStruct((n, dv), BF16), jax.ShapeDtypeStruct((n, dv), F32),
                   jax.ShapeDtypeStruct((n, dq), F32)],
        compiler_params=_cparams("parallel"),
        name="gla_proj",
    )(x, wq, wk, wv, wr, wg1, wg2, bg)


def _gla_scan_kernel(q_ref, k_ref, v_ref, la_ref, s0_ref, o_ref, sfin_ref, st_ref,
                     *, heads, dk, dv):
    ti = pl.program_id(1)
    tt = q_ref.shape[0]
    nsb = tt // GLA_BLOCK

    @pl.when(ti == 0)
    def _():
        for h in range(heads):
            st_ref[h] = s0_ref[0, h].T

    row = lax.broadcasted_iota(jnp.int32, (tt, tt), 0)
    col = lax.broadcasted_iota(jnp.int32, (tt, tt), 1)
    same = (row // GLA_BLOCK) == (col // GLA_BLOCK)
    tril = same & (col <= row)
    bdcat = jnp.concatenate([jnp.where(tril, 1.0, 0.0), jnp.where(same, 1.0, 0.0)],
                            axis=0).astype(BF16)
    rsub = lax.broadcasted_iota(jnp.int32, (tt, dk), 0) // GLA_BLOCK

    for h in range(heads):
        ks = slice(h * dk, (h + 1) * dk)
        vs = slice(h * dv, (h + 1) * dv)
        hi, lo = _split_bf16(la_ref[:, ks])
        ct = (jnp.dot(bdcat, hi, preferred_element_type=F32)
              + jnp.dot(bdcat, lo, preferred_element_type=F32))
        cum, tot = ct[:tt], ct[tt:]
        k = k_ref[:, ks]
        qd = (q_ref[:, ks] * jnp.exp(cum)).astype(BF16)
        kd = (k * jnp.exp(-cum)).astype(BF16)
        kt = k * jnp.exp(tot - cum)
        sc = lax.dot_general(qd, kd, (((1,), (1,)), ((), ())), preferred_element_type=F32)
        sc = jnp.where(tril, sc, 0.0).astype(BF16)
        vh = v_ref[:, vs]
        o_intra = jnp.dot(sc, vh, preferred_element_type=F32)
        ktcat = jnp.concatenate(
            [jnp.where(rsub == i, kt, 0.0).astype(BF16) for i in range(nsb)], axis=1)
        ut = lax.dot_general(vh, ktcat, (((0,), (0,)), ((), ())), preferred_element_type=F32)
        st = st_ref[h]
        parts = []
        for i in range(nsb):
            r0 = i * GLA_BLOCK
            parts.append(lax.dot_general(qd[r0:r0 + GLA_BLOCK], st.astype(BF16),
                                         (((1,), (1,)), ((), ())), preferred_element_type=F32))
            st = st * jnp.exp(tot[r0:r0 + 1, :]) + ut[:, i * dk:(i + 1) * dk]
        st_ref[h] = st
        o_ref[:, vs] = o_intra + jnp.concatenate(parts, axis=0)

    @pl.when(ti == pl.num_programs(1) - 1)
    def _():
        for h in range(heads):
            sfin_ref[0, h] = st_ref[h].T


def _gla_scan(q, k, v, la, s0, batch, seq, heads):
    n, dq = q.shape
    dvt = v.shape[1]
    dk, dv = dq // heads, dvt // heads
    tt = SGU_LEN
    nt = seq // tt
    row = lambda b, t: (b * nt + t, 0)
    st = lambda b, t: (b, 0, 0, 0)
    return pl.pallas_call(
        functools.partial(_gla_scan_kernel, heads=heads, dk=dk, dv=dv),
        grid=(batch, nt),
        in_specs=[pl.BlockSpec((tt, dq), row), pl.BlockSpec((tt, dq), row),
                  pl.BlockSpec((tt, dvt), row), pl.BlockSpec((tt, dq), row),
                  pl.BlockSpec((1, heads, dk, dv), st)],
        out_specs=[pl.BlockSpec((tt, dvt), row), pl.BlockSpec((1, heads, dk, dv), st)],
        out_shape=[jax.ShapeDtypeStruct((n, dvt), F32),
                   jax.ShapeDtypeStruct((batch, heads, dk, dv), F32)],
        scratch_shapes=[pltpu.VMEM((heads, dv, dk), F32)],
        compiler_params=_cparams("parallel", "arbitrary"),
        name="gla_scan",
    )(q, k, v, la, s0)


def _gla_out_kernel(o_ref, r_ref, ng_ref, wo_ref, x_ref, g_ref, b_ref, y_ref, *, alpha, heads):
    o = o_ref[...]
    dv = o.shape[1] // heads
    normed = []
    for h in range(heads):
        oh = o[:, h * dv:(h + 1) * dv]
        ms = jnp.mean(oh * oh, axis=-1, keepdims=True)
        normed.append(oh * lax.rsqrt(ms + LN_EPS) * ng_ref[...])
    gated = jnp.concatenate(normed, axis=1) * _silu(r_ref[...])
    y = jnp.dot(gated.astype(BF16), wo_ref[...], preferred_element_type=F32)
    y_ref[...] = _layer_norm(alpha * x_ref[...] + y, g_ref[...], b_ref[...])


def _gla_out(o, r, norm_g, w_o, x, g, b, alpha, heads):
    n, dvt = o.shape
    d = x.shape[1]
    tm = _tile(n, 512)
    row = lambda i: (i, 0)
    const = lambda i: (0, 0)
    return pl.pallas_call(
        functools.partial(_gla_out_kernel, alpha=alpha, heads=heads),
        grid=(n // tm,),
        in_specs=[pl.BlockSpec((tm, dvt), row), pl.BlockSpec((tm, dvt), row),
                  pl.BlockSpec(norm_g.shape, const), pl.BlockSpec(w_o.shape, const),
                  pl.BlockSpec((tm, d), row), pl.BlockSpec((1, d), const),
                  pl.BlockSpec((1, d), const)],
        out_specs=pl.BlockSpec((tm, d), row),
        out_shape=jax.ShapeDtypeStruct((n, d), F32),
        compiler_params=_cparams("parallel"),
        name="gla_out",
    )(o, r, norm_g, w_o, x, g, b)


def _pool_kernel(x_ref, prev_ref, hist_ref, w_ref, sc_ref, g_ref, b_ref, o_ref, buf_ref,
                 *, alpha, n_valid):
    ti = pl.program_id(1)
    tm, d = x_ref.shape
    x = x_ref[...]

    @pl.when(ti == 0)
    def _():
        buf_ref[0:POOL_PAD, :] = hist_ref[0]

    @pl.when(ti > 0)
    def _():
        buf_ref[0:POOL_PAD, :] = prev_ref[...]

    buf_ref[POOL_PAD:POOL_PAD + tm, :] = x
    gd = d // len(POOL_WINDOWS)
    pos = ti * tm + lax.broadcasted_iota(jnp.int32, (tm, 1), 0)
    ys = []
    for g, win in enumerate(POOL_WINDOWS):
        cs = slice(g * gd, (g + 1) * gd)
        acc = x[:, cs]
        for j in range(1, win):
            acc = acc + buf_ref[POOL_PAD - j:POOL_PAD - j + tm, cs]
        count = jnp.minimum(pos + 1 + n_valid, win).astype(F32)
        dlt = acc / count - x[:, cs]
        ys.append(jnp.dot(dlt.astype(BF16), w_ref[g], preferred_element_type=F32))
    y = jnp.concatenate(ys, axis=1) * sc_ref[...]
    o_ref[...] = _layer_norm(alpha * x + y, g_ref[...], b_ref[...])


def _pool(x, hist, w_pool, scale, g, b, alpha, batch, seq, n_valid):
    n, d = x.shape
    tm = _tile(seq, 256)
    nt = seq // tm
    ratio = tm // POOL_PAD
    const2 = lambda bb, t: (0, 0)
    row = lambda bb, t: (bb * nt + t, 0)
    return pl.pallas_call(
        functools.partial(_pool_kernel, alpha=alpha, n_valid=n_valid),
        grid=(batch, nt),
        in_specs=[pl.BlockSpec((tm, d), row),
                  pl.BlockSpec((POOL_PAD, d),
                               lambda bb, t: (jnp.maximum((bb * nt + t) * ratio - 1, 0), 0)),
                  pl.BlockSpec((1, POOL_PAD, d), lambda bb, t: (bb, 0, 0)),
                  pl.BlockSpec(w_pool.shape, lambda bb, t: (0, 0, 0)),
                  pl.BlockSpec((1, d), const2), pl.BlockSpec((1, d), const2),
                  pl.BlockSpec((1, d), const2)],
        out_specs=pl.BlockSpec((tm, d), row),
        out_shape=jax.ShapeDtypeStruct((n, d), F32),
        scratch_shapes=[pltpu.VMEM((POOL_PAD + tm, d), F32)],
        compiler_params=_cparams("parallel", "arbitrary"),
        name="pool",
    )(x, x, hist, w_pool, scale, g, b)


def _moe_route_math(x, wr_hi, wr_lo, br, n_exp, n_grp):
    hi, lo = _split_bf16(x)
    both = jnp.dot(hi, jnp.concatenate([wr_hi, wr_lo], axis=1), preferred_element_type=F32)
    logit = (both[:, :LANES] + both[:, LANES:]
             + jnp.dot(lo, wr_hi, preferred_element_type=F32)) + br
    lane_i = lax.broadcasted_iota(jnp.int32, logit.shape, 1)
    lane = lane_i.astype(F32)
    neg = jnp.float32(-jnp.inf)
    big = jnp.float32(LANES)
    is_grp = (lane_i >= n_exp) & (lane_i < n_exp + n_grp)
    gl = jnp.where(is_grp, logit, neg)
    gmax = jnp.max(gl, axis=-1, keepdims=True)
    g_sel = jnp.min(jnp.where(gl == gmax, lane, big), axis=-1, keepdims=True) - n_exp
    g_prob = 1.0 / jnp.sum(jnp.where(is_grp, jnp.exp(gl - gmax), 0.0), axis=-1, keepdims=True)
    lane_grp = (lane_i // MOE_EXPERTS_PER_GROUP).astype(F32)
    in_grp = (lane_i < n_exp) & (lane_grp == g_sel)
    el = jnp.where(in_grp, logit, neg)
    m1 = jnp.max(el, axis=-1, keepdims=True)
    i1 = jnp.min(jnp.where(el == m1, lane, big), axis=-1, keepdims=True)
    el2 = jnp.where(lane == i1, neg, el)
    m2 = jnp.max(el2, axis=-1, keepdims=True)
    i2 = jnp.min(jnp.where(el2 == m2, lane, big), axis=-1, keepdims=True)
    e2 = jnp.exp(m2 - m1)
    w1 = g_prob / (1.0 + e2)
    w2 = g_prob * e2 / (1.0 + e2)
    return i1, i2, w1, w2


_META_E1, _META_E2, _META_R1, _META_R2, _META_W1, _META_W2 = range(6)


def _moe_route_kernel(x_ref, wrh_ref, wrl_ref, br_ref, meta_ref, cnt_ref, *, n_exp, n_grp):
    tm = MOE_TILE
    lane = lax.broadcasted_iota(jnp.int32, (tm, LANES), 1).astype(F32)
    r = lax.broadcasted_iota(jnp.int32, (tm, tm), 0)
    c = lax.broadcasted_iota(jnp.int32, (tm, tm), 1)
    before = jnp.where(c < r, 1.0, 0.0).astype(BF16)
    metas, counts = [], []
    for t in range(x_ref.shape[0] // tm):
        i1, i2, w1, w2 = _moe_route_math(x_ref[t * tm:(t + 1) * tm, :], wrh_ref[...], wrl_ref[...],
                                         br_ref[...], n_exp, n_grp)
        m1 = jnp.where(lane == i1, 1.0, 0.0)
        m2 = jnp.where(lane == i2, 1.0, 0.0)
        both = m1 + m2
        rank = jnp.dot(before, both.astype(BF16), preferred_element_type=F32)
        r1 = jnp.sum(m1 * rank, axis=-1, keepdims=True)
        r2 = jnp.sum(m2 * rank, axis=-1, keepdims=True)
        meta = jnp.zeros((tm, LANES), F32)
        for idx, col in ((_META_E1, i1), (_META_E2, i2), (_META_R1, r1), (_META_R2, r2),
                         (_META_W1, w1), (_META_W2, w2)):
            meta = jnp.where(lane == idx, col, meta)
        metas.append(meta)
        counts.append(jnp.broadcast_to(jnp.sum(both, axis=0, keepdims=True), cnt_ref.shape[1:]))
    meta_ref[...] = jnp.concatenate(metas, axis=0)
    cnt_ref[...] = jnp.stack(counts, axis=0)


def _moe_route(x, wr_hi, wr_lo, br, n_exp, n_grp):
    n, d = x.shape
    tm = MOE_TILE
    tiles = min(MOE_ROUTE_TILES, n // tm)
    const = lambda i: (0, 0)
    return pl.pallas_call(
        functools.partial(_moe_route_kernel, n_exp=n_exp, n_grp=n_grp),
        grid=(n // (tiles * tm),),
        in_specs=[pl.BlockSpec((tiles * tm, d), lambda i: (i, 0)),
                  pl.BlockSpec(wr_hi.shape, const), pl.BlockSpec(wr_lo.shape, const),
                  pl.BlockSpec((1, LANES), const)],
        out_specs=[pl.BlockSpec((tiles * tm, LANES), lambda i: (i, 0)),
                   pl.BlockSpec((tiles, 8, LANES), lambda i: (i, 0, 0))],
        out_shape=[jax.ShapeDtypeStruct((n, LANES), F32),
                   jax.ShapeDtypeStruct((n // tm, 8, LANES), F32)],
        compiler_params=_cparams("parallel"),
        name="moe_route",
    )(x, wr_hi, wr_lo, br)


def _moe_placement(meta, n_stage, weighted):
    tm = meta.shape[0]
    col = lax.broadcasted_iota(jnp.int32, (tm, n_stage), 1).astype(F32)
    out = None
    for r_idx, w_idx in ((_META_R1, _META_W1), (_META_R2, _META_W2)):
        val = meta[:, w_idx:w_idx + 1] if weighted else 1.0
        term = jnp.where(col == meta[:, r_idx:r_idx + 1], val, 0.0)
        out = term if out is None else out + term
    return out.astype(BF16)


def _moe_segment_copies(tile, n_exp, seg_start, seg_len, stage_base, copy_rows):
    big = 2 * MOE_ALIGN
    for e in range(n_exp):
        idx = tile * n_exp + e
        src, dst, n = stage_base[idx], seg_start[idx], seg_len[idx]

        def piece(c, carry, src=src, dst=dst):
            copy_rows(pl.multiple_of(dst + c * big, MOE_ALIGN), pl.multiple_of(src + c * big, MOE_ALIGN), big)
            return carry

        lax.fori_loop(0, n // big, piece, 0)

        @pl.when(n % big != 0)
        def _(src=src, dst=dst, n=n):
            done = n // big * big
            copy_rows(pl.multiple_of(dst + done, MOE_ALIGN), pl.multiple_of(src + done, MOE_ALIGN), MOE_ALIGN)


def _moe_experts_kernel(seg_start, seg_len, stage_base, off,
                        x_ref, meta_ref, wup_ref, wdn_ref, g_ref, b_ref, o_ref,
                        buf_ref, stg_ref, *, alpha, n_exp, steps_per_unit, hidden):
    u, s = pl.program_id(0), pl.program_id(1)
    tiles, n_stage = stg_ref.shape[0], stg_ref.shape[1]
    tm = x_ref.shape[0] // tiles

    @pl.when((u == 0) & (s == 0))
    def _():
        buf_ref[...] = jnp.zeros_like(buf_ref)

    @pl.when(s < steps_per_unit)
    def _():
        for k in range(tiles):
            place = _moe_placement(meta_ref[k * tm:(k + 1) * tm, :], n_stage, weighted=False)
            xb = x_ref[k * tm:(k + 1) * tm, :].astype(BF16)
            for c0 in range(0, n_stage, 2 * LANES):
                rows = lax.dot_general(place[:, c0:c0 + 2 * LANES], xb, (((0,), (0,)), ((), ())),
                                       preferred_element_type=F32)
                stg_ref[k, c0:c0 + 2 * LANES, :] = rows.astype(BF16)
        for k in range(tiles):
            def copy_rows(unit_row, stage_row, rows, k=k):
                buf_ref[pl.ds(unit_row, rows), :] = stg_ref[k, pl.ds(stage_row, rows), :]

            _moe_segment_copies((u * steps_per_unit + s) * tiles + k, n_exp, seg_start, seg_len,
                                stage_base, copy_rows)

    @pl.when((s >= steps_per_unit) & (s < steps_per_unit + n_exp))
    def _():
        e = s - steps_per_unit
        start = off[u * (n_exp + 1) + e]
        n = off[u * (n_exp + 1) + e + 1] - start

        def ffn_rows(r0, size, n_valid):
            r0 = pl.multiple_of(r0, MOE_ALIGN)
            xs = buf_ref[pl.ds(r0, size), :]
            h = jnp.dot(xs, wup_ref[0], preferred_element_type=F32)
            act = _silu(h[:, :hidden]) * h[:, hidden:]
            y = jnp.dot(act.astype(BF16), wdn_ref[0], preferred_element_type=F32)
            if n_valid is not None:
                row = lax.broadcasted_iota(jnp.int32, y.shape, 0)
                y = jnp.where(row < n_valid, y, xs.astype(F32))
            buf_ref[pl.ds(r0, size), :] = y.astype(BF16)

        n_full = n // MOE_FFN_ROWS

        def full(c, carry):
            ffn_rows(start + c * MOE_FFN_ROWS, MOE_FFN_ROWS, None)
            return carry

        lax.fori_loop(0, n_full, full, 0)
        rem = n - n_full * MOE_FFN_ROWS
        tail = start + n_full * MOE_FFN_ROWS
        for size in range(LANES, MOE_FFN_ROWS + 1, LANES):
            @pl.when((rem > size - LANES) & (rem <= size))
            def _(size=size):
                ffn_rows(tail, size, rem)

    @pl.when(s >= steps_per_unit + n_exp)
    def _():
        t = s - steps_per_unit - n_exp
        for k in range(tiles):
            def copy_rows(unit_row, stage_row, rows, k=k):
                stg_ref[k, pl.ds(stage_row, rows), :] = buf_ref[pl.ds(unit_row, rows), :]

            _moe_segment_copies((u * steps_per_unit + t) * tiles + k, n_exp, seg_start, seg_len,
                                stage_base, copy_rows)
        outs = []
        for k in range(tiles):
            place = _moe_placement(meta_ref[k * tm:(k + 1) * tm, :], n_stage, weighted=True)
            f = jnp.dot(place, stg_ref[k], preferred_element_type=F32)
            outs.append(_layer_norm(alpha * x_ref[k * tm:(k + 1) * tm, :] + f, g_ref[...], b_ref[...]))
        o_ref[...] = jnp.concatenate(outs, axis=0)


def _moe(x, wr_hi, wr_lo, br, w_up, w_dn, g, b, alpha, n_grp):
    n, d = x.shape
    n_exp, _, two_f = w_up.shape
    hidden = two_f // 2
    tm = MOE_TILE
    unit = min(MOE_UNIT, n)
    tpu = unit // tm
    n_tiles, n_units = n // tm, n // unit
    worst_pad = MOE_ALIGN - 1
    n_stage = -(-(2 * tm + n_exp * worst_pad) // (2 * LANES)) * (2 * LANES)
    cap = -(-(2 * unit + n_exp * tpu * worst_pad) // MOE_FFN_ROWS) * MOE_FFN_ROWS + MOE_FFN_ROWS

    meta, cnt = _moe_route(x, wr_hi, wr_lo, br, n_exp, n_grp)
    cnt = cnt[:, 0, :n_exp].astype(jnp.int32)
    seg_len = (cnt + worst_pad) // MOE_ALIGN * MOE_ALIGN
    stage_base = jnp.cumsum(seg_len, axis=1) - seg_len
    sl = seg_len.reshape(n_units, tpu, n_exp)
    unit_len = jnp.sum(sl, axis=1)
    off = jnp.concatenate([jnp.zeros((n_units, 1), jnp.int32), jnp.cumsum(unit_len, axis=1)], axis=1)
    seg_start = off[:, None, :n_exp] + jnp.cumsum(sl, axis=1) - sl
    base_of = jnp.repeat(stage_base.astype(F32), tm, axis=0)
    experts = jnp.arange(n_exp, dtype=F32)
    lane = jnp.arange(LANES)
    for e_idx, r_idx in ((_META_E1, _META_R1), (_META_E2, _META_R2)):
        base = jnp.sum(jnp.where(meta[:, e_idx:e_idx + 1] == experts, base_of, 0.0), axis=1, keepdims=True)
        meta = meta + jnp.where(lane == r_idx, base, 0.0)
    tables = (seg_start.reshape(-1), seg_len.reshape(-1), stage_base.reshape(-1), off.reshape(-1))

    tiles = min(MOE_STEP_TILES, tpu)
    spu = tpu // tiles

    def block_of(u, s):
        t = jnp.where(s < spu, s, jnp.where(s >= spu + n_exp, s - spu - n_exp, spu - 1))
        return u * spu + t

    tile_row = lambda u, s, *_: (block_of(u, s), 0)
    expert = lambda u, s, *_: (jnp.clip(s - spu, 0, n_exp - 1), 0, 0)
    out_row = lambda u, s, *_: (u * spu + jnp.clip(s - spu - n_exp, 0, spu - 1), 0)
    const = lambda u, s, *_: (0, 0)

    return pl.pallas_call(
        functools.partial(_moe_experts_kernel, alpha=alpha, n_exp=n_exp, steps_per_unit=spu,
                          hidden=hidden),
        grid_spec=pltpu.PrefetchScalarGridSpec(
            num_scalar_prefetch=4, grid=(n_units, 2 * spu + n_exp),
            in_specs=[pl.BlockSpec((tiles * tm, d), tile_row), pl.BlockSpec((tiles * tm, LANES), tile_row),
                      pl.BlockSpec((1, d, two_f), expert), pl.BlockSpec((1, hidden, d), expert),
                      pl.BlockSpec((1, d), const), pl.BlockSpec((1, d), const)],
            out_specs=pl.BlockSpec((tiles * tm, d), out_row),
            scratch_shapes=[pltpu.VMEM((cap, d), BF16), pltpu.VMEM((tiles, n_stage, d), BF16)]),
        out_shape=jax.ShapeDtypeStruct((n, d), F32),
        compiler_params=_cparams("arbitrary", "arbitrary"),
        name="moe_experts",
    )(*tables, x, meta, w_up, w_dn, g, b)


def _router_params(w_group, b_group, w_router, b_router):
    d, n_grp = w_group.shape
    n_exp = w_router.shape[1]
    w = jnp.zeros((d, LANES), F32).at[:, :n_exp].set(w_router).at[:, n_exp:n_exp + n_grp].set(w_group)
    hi = w.astype(BF16)
    lo = (w - hi.astype(F32)).astype(BF16)
    br = jnp.zeros((1, LANES), F32).at[0, :n_exp].set(b_router).at[0, n_exp:n_exp + n_grp].set(b_group)
    return hi, lo, br


def _sgu_mix_params(w_s, b_s, t_len, gd):
    groups = w_s.shape[0]
    L = min(t_len, SGU_LEN)
    pos = jnp.arange(L)
    allowed = (pos[None, :] // CHUNK) <= (pos[:, None] // CHUNK)
    w = jnp.where(allowed[None], w_s[:, :L, :L], 0.0)
    rep = SGU_LEN // L
    if rep > 1:
        eye = jnp.eye(rep, dtype=F32)
        w = jnp.einsum("ab,gts->gatbs", eye, w).reshape(groups, SGU_LEN, SGU_LEN)
    bias = jnp.tile(b_s[:, :L], (1, rep))
    bias = jnp.broadcast_to(bias[:, :, None], (groups, SGU_LEN, gd))
    return w.astype(BF16), bias.astype(F32)


def kernel(x_prompt, x_sample, cache_sb_k, cache_sb_v, state_gla, state_pool, sb_w_qkv, sb_w_o, sgu_w_in, sgu_ln_g, sgu_ln_b, sgu_w_s, sgu_b_s, sgu_w_out, gla_w_in, gla_w_g1, gla_w_g2, gla_b_g, gla_norm_g, gla_w_o, pool_w, pool_scale, ln1_g, ln1_b, ln2_g, ln2_b, moe_w_group, moe_b_group, moe_w_router, moe_b_router, moe_w_up, moe_w_down):
    depth = ln1_g.shape[0]
    d = x_prompt.shape[-1]
    alpha = (2.0 * depth) ** 0.25
    sb_heads, sb_hd = cache_sb_k.shape[3], cache_sb_k.shape[4]
    gla_heads, gla_dk, gla_dv = state_gla.shape[2], state_gla.shape[3], state_gla.shape[4]
    n_grp = moe_w_group.shape[-1]
    dq, dvt = gla_heads * gla_dk, gla_heads * gla_dv
    n_hist = state_pool.shape[2]
    sgu_groups = sgu_w_s.shape[1]
    sgu_width = sgu_w_out.shape[1]

    bf = lambda a: a.astype(BF16)
    vec = lambda a: a.reshape(1, -1)

    def run(x3, past):
        batch, seq, _ = x3.shape
        n = batch * seq
        x = x3.reshape(n, d)
        sb_k, sb_v, sgu_v, gla_s, pool_h = [], [], [], [], []
        for i in range(depth):
            m, j = i % 4, i // 4
            g1, b1 = vec(ln1_g[i]), vec(ln1_b[i])
            if m == 0:
                wq, wk, wv = (bf(sb_w_qkv[j][:, c * d:(c + 1) * d]) for c in range(3))
                from_t = lambda a: a.reshape(batch, sb_heads, sb_hd, -1).transpose(0, 3, 1, 2)
                to_t = lambda a: a.transpose(0, 2, 3, 1).reshape(batch, d, -1)
                if past is None:
                    q, kt, vt, ktb, vtb = _sb_qkv(x, wq, wk.T, wv.T, batch, seq, sb_hd ** -0.5)
                    o = _sb_prompt(q, ktb, vtb, batch, seq, sb_hd)
                    sb_k.append(from_t(kt))
                    sb_v.append(from_t(vt))
                else:
                    q, k, v, kb, vb = _proj(
                        x, [wq, wk, wv],
                        [(0, BF16, sb_hd ** -0.5), (1, F32, 1.0), (2, F32, 1.0), (1, BF16, 1.0), (2, BF16, 1.0)],
                        "sb_qkv")
                    o = _sb_sample(q, kb, vb, to_t(past[0][j]), to_t(past[1][j]), batch, seq, sb_hd)
                    sb_k.append(k.reshape(batch, seq, sb_heads, sb_hd))
                    sb_v.append(v.reshape(batch, seq, sb_heads, sb_hd))
                x = _mm_res_ln(o, bf(sb_w_o[j]), x, g1, b1, alpha, "sb_out")
            elif m == 1:
                ws_mix, bs_mix = _sgu_mix_params(sgu_w_s[j], sgu_b_s[j], seq, sgu_width // sgu_groups)
                x, v = _sgu(x, bf(sgu_w_in[j]), vec(sgu_ln_g[j]), vec(sgu_ln_b[j]), ws_mix, bs_mix,
                            bf(sgu_w_out[j]), g1, b1, alpha, past is not None)
                if v is not None:
                    sgu_v.append(v.reshape(batch, seq, sgu_width))
            elif m == 2:
                w_in = gla_w_in[j]
                wg1 = jnp.zeros((d, LANES), F32).at[:, :gla_w_g1.shape[-1]].set(gla_w_g1[j])
                wg2 = jnp.zeros((LANES, dq), F32).at[:gla_w_g2.shape[1]].set(gla_w_g2[j])
                q, k, v, r, la = _gla_proj(
                    x, bf(w_in[:, :dq]), bf(w_in[:, dq:2 * dq]), bf(w_in[:, 2 * dq:2 * dq + dvt]),
                    bf(w_in[:, 2 * dq + dvt:]), bf(wg1), bf(wg2), vec(gla_b_g[j]), gla_dk ** -0.5)
                if past is None:
                    s0 = jnp.zeros((batch, gla_heads, gla_dk, gla_dv), F32)
                    seq_p = seq
                else:
                    s0 = past[2][j]
                    seq_p = -(-seq // SGU_LEN) * SGU_LEN
                    padrows = lambda a: jnp.pad(a.reshape(batch, seq, -1),
                                                ((0, 0), (0, seq_p - seq), (0, 0))).reshape(batch * seq_p, -1)
                    q, k, v, la = padrows(q), padrows(k), padrows(v), padrows(la)
                o, s = _gla_scan(q, k, v, la, s0, batch, seq_p, gla_heads)
                if seq_p != seq:
                    o = o.reshape(batch, seq_p, dvt)[:, :seq].reshape(n, dvt)
                gla_s.append(s)
                x = _gla_out(o, r, vec(gla_norm_g[j]), bf(gla_w_o[j]), x, g1, b1, alpha, gla_heads)
            else:
                if past is None:
                    hist = jnp.zeros((batch, POOL_PAD, d), F32)
                    n_valid = 0
                else:
                    hist = jnp.pad(past[3][j], ((0, 0), (POOL_PAD - n_hist, 0), (0, 0)))
                    n_valid = n_hist
                x_in = x.reshape(batch, seq, d)
                if past is None:
                    pool_h.append(x_in[:, seq - n_hist:])
                else:
                    pool_h.append(jnp.concatenate([past[3][j], x_in], axis=1)[:, -n_hist:])
                x = _pool(x, hist, bf(pool_w[j]), vec(pool_scale[j]), g1, b1, alpha, batch, seq, n_valid)
            wr_hi, wr_lo, br = _router_params(moe_w_group[i], moe_b_group[i], moe_w_router[i], moe_b_router[i])
            x = _moe(x, wr_hi, wr_lo, br, bf(moe_w_up[i]), bf(moe_w_down[i]),
                     vec(ln2_g[i]), vec(ln2_b[i]), alpha, n_grp)
        return x.reshape(batch, seq, d), sb_k, sb_v, sgu_v, gla_s, pool_h

    y_p, kp, vp, _, sp, hp = run(x_prompt, None)
    y_s, ks, vs, us, ss, hs = run(x_sample, (cache_sb_k, cache_sb_v, state_gla, state_pool))
    return (y_p, y_s, jnp.stack(kp), jnp.stack(vp), jnp.stack(ks), jnp.stack(vs), jnp.stack(us),
            jnp.stack(sp), jnp.stack(ss), jnp.stack(hp), jnp.stack(hs))
```
